```python
import numpy as np
import jax
import jax.numpy as jnp
from jax import lax

D_MODEL = 1024
BATCH = 8
SEQ = 4096
DEPTH = 4

HEAD_DIM = 64
ROPE_THETA = 10000.0
EPS = 1e-6
Q_BLOCK = 128
NEG_INF = -1e30
FORCE_SCORE = 1e4
N_MOD = 9
D_FF = 2816
A_HEADS = 4
A_KV_HEADS = 2
A_WINDOW = 128
B_HEADS = 4
B_Q_LORA = 256
B_KV_LORA = 128
B_NOPE = 64
B_ROPE = 32
B_V = 64
C_HEADS = 8
C_KV_HEADS = 2
C_CMP_BLOCK = 32
C_CMP_STRIDE = 16
C_SLC_BLOCK = 64
C_N_SEL = 8
C_WINDOW = 256
C_CMP_HID = 128
C_GATE_HID = 64
IN_WIDTHS = (A_HEADS * HEAD_DIM, A_KV_HEADS * HEAD_DIM, A_KV_HEADS * HEAD_DIM,
             B_Q_LORA, B_KV_LORA, B_ROPE,
             C_HEADS * HEAD_DIM) + (C_KV_HEADS * HEAD_DIM,) * 6 + (C_GATE_HID,)
D_IN = sum(IN_WIDTHS)
D_MIX = A_HEADS * HEAD_DIM + B_HEADS * B_V + C_HEADS * HEAD_DIM

kernel_name = 'hybrid_swa_mla_nsa_macaron_adaln'


def rms_norm(x, g):
    xf = x.astype(jnp.float32)
    y = xf * lax.rsqrt(jnp.mean(xf * xf, axis=-1, keepdims=True) + EPS)
    return (y * g.astype(jnp.float32)).astype(x.dtype)


def rope_tables(positions, dim):
    inv_freq = 1.0 / (ROPE_THETA ** (jnp.arange(0, dim, 2, dtype=jnp.float32) / dim))
    ang = positions.astype(jnp.float32)[..., None] * inv_freq
    return jnp.cos(ang), jnp.sin(ang)


def apply_rope(x, cos, sin):
    xf = x.astype(jnp.float32)
    x1, x2 = jnp.split(xf, 2, axis=-1)
    c, s = cos[:, :, None, :], sin[:, :, None, :]
    return jnp.concatenate([x1 * c - x2 * s, x2 * c + x1 * s], axis=-1).astype(x.dtype)


def swiglu(u, w_in, w_out):
    gate, up = jnp.split(u @ w_in, 2, axis=-1)
    return (jax.nn.silu(gate) * up) @ w_out


def banded_attention(q, k, v, window, sinks=None):
    b, t, h, d = q.shape
    hkv = k.shape[2]
    g = h // hkv
    nb = t // Q_BLOCK
    npv = window // Q_BLOCK
    qb = q.reshape(b, nb, Q_BLOCK, hkv, g, d)

    def band(z):
        zb = z.reshape(b, nb, Q_BLOCK, hkv, z.shape[-1])
        zp = jnp.pad(zb, ((0, 0), (npv, 0), (0, 0), (0, 0), (0, 0)))
        return jnp.concatenate([zp[:, i:i + nb] for i in range(npv + 1)], axis=2)

    kb, vb = band(k), band(v)
    s = jnp.einsum('bnqhgd,bnkhd->bhgnqk', qb, kb).astype(jnp.float32) * (d ** -0.5)
    n_keys = (npv + 1) * Q_BLOCK
    rel = (jnp.arange(Q_BLOCK)[:, None] + npv * Q_BLOCK) - jnp.arange(n_keys)[None, :]
    k_abs = jnp.arange(nb)[:, None] * Q_BLOCK + jnp.arange(n_keys)[None, :] - npv * Q_BLOCK
    mask = ((rel >= 0) & (rel < window))[None] & (k_abs >= 0)[:, None, :]
    s = jnp.where(mask, s, NEG_INF)
    if sinks is None:
        p = jax.nn.softmax(s, axis=-1)
    else:
        sink = jnp.broadcast_to(sinks.astype(jnp.float32).reshape(1, hkv, g, 1, 1, 1), s.shape[:-1] + (1,))
        p = jax.nn.softmax(jnp.concatenate([s, sink], axis=-1), axis=-1)[..., :-1]
    o = jnp.einsum('bhgnqk,bnkhd->bnqhgd', p.astype(v.dtype), vb)
    return o.reshape(b, t, h, v.shape[-1])


def causal_attention(q, k, v):
    b, t, h, dq = q.shape
    nb = t // Q_BLOCK
    scale = dq ** -0.5
    qb = jnp.moveaxis(q.reshape(b, nb, Q_BLOCK, h, dq), 1, 0)
    k_pos = jnp.arange(t)

    def one(args):
        qi, n = args
        s = jnp.einsum('bqhd,bkhd->bhqk', qi, k).astype(jnp.float32) * scale
        q_pos = n * Q_BLOCK + jnp.arange(Q_BLOCK)
        s = jnp.where(k_pos[None, :] <= q_pos[:, None], s, NEG_INF)
        p = jax.nn.softmax(s, axis=-1).astype(v.dtype)
        return jnp.einsum('bhqk,bkhd->bqhd', p, v)

    o = lax.map(one, (qb, jnp.arange(nb)))
    return jnp.moveaxis(o, 0, 1).reshape(b, t, h, v.shape[-1])


def compress_blocks(z, pe, w1, w2):
    b, t, hkv, d = z.shape
    n_chunk = t // C_CMP_STRIDE
    r = C_CMP_BLOCK // C_CMP_STRIDE
    n_cmp = n_chunk - r + 1
    zc = z.reshape(b, n_chunk, C_CMP_STRIDE, hkv, d)
    blocks = jnp.concatenate([zc[:, i:i + n_cmp] for i in range(r)], axis=2)
    blocks = blocks + pe[None, None, :, None, :]
    flat = jnp.moveaxis(blocks, 3, 2).reshape(b, n_cmp, hkv, C_CMP_BLOCK * d)
    return jax.nn.silu(flat @ w1) @ w2


def nsa_compressed_selected(q, q_rot, k_cmp, v_cmp, k_slc, v_slc):
    b, t, h, d = q.shape
    hkv = k_cmp.shape[2]
    g = h // hkv
    n_cmp = k_cmp.shape[1]
    n_slc = t // C_SLC_BLOCK
    n_sel = min(C_N_SEL, n_slc)
    nb = t // Q_BLOCK
    scale = d ** -0.5
    ci = np.arange(n_cmp)[:, None] * C_CMP_STRIDE
    sj = np.arange(n_slc)[None, :] * C_SLC_BLOCK
    overlap = jnp.asarray((ci <= sj + C_SLC_BLOCK - 1) & (ci + C_CMP_BLOCK - 1 >= sj), dtype=jnp.float32)
    cmp_end = jnp.arange(n_cmp) * C_CMP_STRIDE + (C_CMP_BLOCK - 1)
    s_blk = jnp.arange(n_slc)
    kb = jnp.moveaxis(k_slc.reshape(b, n_slc, C_SLC_BLOCK, hkv, d), 3, 1)
    vb = jnp.moveaxis(v_slc.reshape(b, n_slc, C_SLC_BLOCK, hkv, d), 3, 1)
    gather = jax.vmap(jax.vmap(lambda blocks, i: blocks[i]))

    def to_blocks(z):
        return jnp.moveaxis(z.reshape(b, nb, Q_BLOCK, hkv, g, d), 1, 0)

    def one(args):
        qc, qr, n = args
        q_pos = n * Q_BLOCK + jnp.arange(Q_BLOCK)
        s = jnp.einsum('bqhgd,bjhd->bhgqj', qc, k_cmp).astype(jnp.float32) * scale
        vis = cmp_end[None, :] <= q_pos[:, None]
        p = jnp.where(vis, jax.nn.softmax(jnp.where(vis, s, NEG_INF), axis=-1), 0.0)
        o_cmp = jnp.einsum('bhgqj,bjhd->bqhgd', p.astype(v_cmp.dtype), v_cmp)
        imp = jnp.einsum('bhgqj,js->bhqs', p, overlap)
        cur = q_pos // C_SLC_BLOCK
        forced = (s_blk[None, :] == 0) | (s_blk[None, :] == cur[:, None]) | (s_blk[None, :] == cur[:, None] - 1)
        future = s_blk[None, :] > cur[:, None]
        score = jnp.where(future, NEG_INF, jnp.where(forced, FORCE_SCORE, imp))
        top_val, idx = lax.top_k(score, n_sel)
        blk_ok = top_val > 0.5 * NEG_INF
        kg = gather(kb, idx)
        vg = gather(vb, idx)
        s2 = jnp.einsum('bqhgd,bhqnld->bhgqnl', qr, kg).astype(jnp.float32) * scale
        tok = idx[..., None] * C_SLC_BLOCK + jnp.arange(C_SLC_BLOCK)
        ok = blk_ok[..., None] & (tok <= q_pos[None, None, :, None, None])
        s2 = jnp.where(ok[:, :, None], s2, NEG_INF)
        p2 = jax.nn.softmax(s2.reshape(b, hkv, g, Q_BLOCK, -1), axis=-1).reshape(s2.shape)
        o_slc = jnp.einsum('bhgqnl,bhqnld->bqhgd', p2.astype(vg.dtype), vg)
        return o_cmp.reshape(b, Q_BLOCK, h, d), o_slc.reshape(b, Q_BLOCK, h, d)

    o_cmp, o_slc = lax.map(one, (to_blocks(q), to_blocks(q_rot), jnp.arange(nb)))
    o_cmp = jnp.moveaxis(o_cmp, 0, 1).reshape(b, t, h, d)
    o_slc = jnp.moveaxis(o_slc, 0, 1).reshape(b, t, h, d)
    return o_cmp, o_slc


def hybrid_mixer(u, cos64, sin64, cos32, sin32, w_in, w_out, a_sinks, a_qk_g,
                 b_q_lat_g, b_kv_lat_g, b_w_uq, b_w_ukv, b_qk_nope_g, b_qk_rope_g,
                 c_qk_g, c_cmp_pe, c_cmp_w1, c_cmp_w2, c_gate_w, c_gate_b):
    b, t, _ = u.shape
    splits = np.cumsum(IN_WIDTHS)[:-1].tolist()
    (a_q, a_k, a_v, b_cq, b_ckv, b_kr, c_q,
     c_kc, c_vc, c_ks, c_vs, c_kw, c_vw, c_gf) = jnp.split(u @ w_in, splits, axis=-1)

    def heads(z, n):
        return z.reshape(b, t, n, z.shape[-1] // n)

    qa = apply_rope(rms_norm(heads(a_q, A_HEADS), a_qk_g[0]), cos64, sin64)
    ka = apply_rope(rms_norm(heads(a_k, A_KV_HEADS), a_qk_g[1]), cos64, sin64)
    o_a = banded_attention(qa, ka, heads(a_v, A_KV_HEADS), A_WINDOW, a_sinks)

    q_b = heads(rms_norm(b_cq, b_q_lat_g) @ b_w_uq, B_HEADS)
    kv_b = heads(rms_norm(b_ckv, b_kv_lat_g) @ b_w_ukv, B_HEADS)
    q_nope = rms_norm(q_b[..., :B_NOPE], b_qk_nope_g[0])
    q_pe = apply_rope(rms_norm(q_b[..., B_NOPE:], b_qk_rope_g[0]), cos32, sin32)
    k_nope = rms_norm(kv_b[..., :B_NOPE], b_qk_nope_g[1])
    v_b = kv_b[..., B_NOPE:]
    k_pe = apply_rope(rms_norm(b_kr[:, :, None, :], b_qk_rope_g[1]), cos32, sin32)
    k_b = jnp.concatenate([k_nope, jnp.broadcast_to(k_pe, (b, t, B_HEADS, B_ROPE))], axis=-1)
    o_b = causal_attention(jnp.concatenate([q_nope, q_pe], axis=-1), k_b, v_b)

    qc = rms_norm(heads(c_q, C_HEADS), c_qk_g[0])
    qc_rot = apply_rope(qc, cos64, sin64)
    k_cmp = rms_norm(compress_blocks(heads(c_kc, C_KV_HEADS), c_cmp_pe[0], c_cmp_w1[0], c_cmp_w2[0]), c_qk_g[1])
    v_cmp = compress_blocks(heads(c_vc, C_KV_HEADS), c_cmp_pe[1], c_cmp_w1[1], c_cmp_w2[1])
    k_slc = apply_rope(rms_norm(heads(c_ks, C_KV_HEADS), c_qk_g[2]), cos64, sin64)
    k_win = apply_rope(rms_norm(heads(c_kw, C_KV_HEADS), c_qk_g[3]), cos64, sin64)
    o_cmp, o_slc = nsa_compressed_selected(qc, qc_rot, k_cmp, v_cmp, k_slc, heads(c_vs, C_KV_HEADS))
    o_win = banded_attention(qc_rot, k_win, heads(c_vw, C_KV_HEADS), C_WINDOW)
    gates = jax.nn.sigmoid(jax.nn.silu(c_gf) @ c_gate_w + c_gate_b).reshape(b, t, C_HEADS, 3)
    o_c = gates[..., 0:1] * o_cmp + gates[..., 1:2] * o_slc + gates[..., 2:3] * o_win

    o = jnp.concatenate([o_a.reshape(b, t, -1), o_b.reshape(b, t, -1), o_c.reshape(b, t, -1)], axis=-1)
    return o @ w_out


def setup_inputs(seed: int = 0) -> dict:
    key = jax.random.key(seed)
    ks = jax.random.split(key, 25)

    def nrm(k, shape, s):
        return jax.random.normal(k, shape, jnp.float32) * s

    def gain(k, shape):
        return 1.0 + 0.05 * jax.random.normal(k, shape, jnp.float32)

    x = nrm(ks[0], (BATCH, SEQ, D_MODEL), 1.0)
    c = nrm(ks[1], (BATCH, D_MODEL), 1.0)
    offsets = jax.random.randint(ks[2], (BATCH, 1), 0, 1024, dtype=jnp.int32)
    positions = (offsets + jnp.arange(SEQ, dtype=jnp.int32)[None, :]).astype(jnp.int32)
    return {
        'x': x,
        'c': c,
        'positions': positions,
        'ada_w': nrm(ks[3], (DEPTH, D_MODEL, N_MOD * D_MODEL), 0.5 * D_MODEL ** -0.5),
        'ada_b': nrm(ks[4], (DEPTH, N_MOD * D_MODEL), 0.02),
        'norm_g': gain(ks[5], (DEPTH, 3, D_MODEL)),
        'ffn_w_in': nrm(ks[6], (DEPTH, 2, D_MODEL, 2 * D_FF), D_MODEL ** -0.5),
        'ffn_w_out': nrm(ks[7], (DEPTH, 2, D_FF, D_MODEL), D_FF ** -0.5),
        'w_in': nrm(ks[8], (DEPTH, D_MODEL, D_IN), D_MODEL ** -0.5),
        'w_out': nrm(ks[9], (DEPTH, D_MIX, D_MODEL), D_MIX ** -0.5),
        'a_sinks': nrm(ks[10], (DEPTH, A_HEADS), 0.5),
        'a_qk_g': gain(ks[11], (DEPTH, 2, HEAD_DIM)),
        'b_q_lat_g': gain(ks[12], (DEPTH, B_Q_LORA)),
        'b_kv_lat_g': gain(ks[13], (DEPTH, B_KV_LORA)),
        'b_w_uq': nrm(ks[14], (DEPTH, B_Q_LORA, B_HEADS * (B_NOPE + B_ROPE)), B_Q_LORA ** -0.5),
        'b_w_ukv': nrm(ks[15], (DEPTH, B_KV_LORA, B_HEADS * (B_NOPE + B_V)), B_KV_LORA ** -0.5),
        'b_qk_nope_g': gain(ks[16], (DEPTH, 2, B_NOPE)),
        'b_qk_rope_g': gain(ks[17], (DEPTH, 2, B_ROPE)),
        'c_qk_g': gain(ks[18], (DEPTH, 4, HEAD_DIM)),
        'c_cmp_pe': nrm(ks[19], (DEPTH, 2, C_CMP_BLOCK, HEAD_DIM), 0.2),
        'c_cmp_w1': nrm(ks[20], (DEPTH, 2, C_CMP_BLOCK * HEAD_DIM, C_CMP_HID), (C_CMP_BLOCK * HEAD_DIM) ** -0.5),
        'c_cmp_w2': nrm(ks[21], (DEPTH, 2, C_CMP_HID, HEAD_DIM), C_CMP_HID ** -0.5),
        'c_gate_w': nrm(ks[22], (DEPTH, C_GATE_HID, C_HEADS * 3), C_GATE_HID ** -0.5),
        'c_gate_b': nrm(ks[23], (DEPTH, C_HEADS * 3), 0.02),
    }


def reference(x, c, positions, ada_w, ada_b, norm_g, ffn_w_in, ffn_w_out, w_in, w_out,
              a_sinks, a_qk_g, b_q_lat_g, b_kv_lat_g, b_w_uq, b_w_ukv, b_qk_nope_g, b_qk_rope_g,
              c_qk_g, c_cmp_pe, c_cmp_w1, c_cmp_w2, c_gate_w, c_gate_b):
    bsz, _, dm = x.shape
    cos64, sin64 = rope_tables(positions, HEAD_DIM)
    cos32, sin32 = rope_tables(positions, B_ROPE)
    cond = jax.nn.silu(c)
    h = x
    for l in range(DEPTH):
        mod = (cond @ ada_w[l] + ada_b[l]).reshape(bsz, N_MOD, 1, dm)
        sh1, sc1, g1, sh2, sc2, g2, sh3, sc3, g3 = [mod[:, i] for i in range(N_MOD)]
        u = rms_norm(h, norm_g[l, 0]) * (1.0 + sc1) + sh1
        h = h + 0.5 * g1 * swiglu(u, ffn_w_in[l, 0], ffn_w_out[l, 0])
        u = rms_norm(h, norm_g[l, 1]) * (1.0 + sc2) + sh2
        h = h + g2 * hybrid_mixer(u, cos64, sin64, cos32, sin32, w_in[l], w_out[l], a_sinks[l], a_qk_g[l],
                                  b_q_lat_g[l], b_kv_lat_g[l], b_w_uq[l], b_w_ukv[l], b_qk_nope_g[l],
                                  b_qk_rope_g[l], c_qk_g[l], c_cmp_pe[l], c_cmp_w1[l], c_cmp_w2[l],
                                  c_gate_w[l], c_gate_b[l])
        u = rms_norm(h, norm_g[l, 2]) * (1.0 + sc3) + sh3
        h = h + 0.5 * g3 * swiglu(u, ffn_w_in[l, 1], ffn_w_out[l, 1])
    return h
```

```python
import functools

import numpy as np
import jax
import jax.numpy as jnp
from jax import lax
from jax.experimental import pallas as pl
from jax.experimental.pallas import tpu as pltpu

F32 = jnp.float32
BF16 = jnp.bfloat16

LANES = 128
HEAD = 64
EPS = 1e-6
ROPE_THETA = 10000.0
NEG_INF = -1e30
FORCE_SCORE = 1e4
REMOVED = -3e38
N_MOD = 9

A_HEADS, A_KV, A_WINDOW = 4, 2, 128
B_HEADS, B_Q_LORA, B_KV_LORA, B_NOPE, B_ROPE, B_V = 4, 256, 128, 64, 32, 64
C_HEADS, C_KV, C_WINDOW = 8, 2, 256
C_CMP_BLOCK, C_CMP_STRIDE, C_SLC_BLOCK, C_N_SEL = 32, 16, 64, 8
C_CMP_HID, C_GATE_HID = 128, 64

VMEM_LIMIT = 56 * 1024 * 1024

O_AQ, O_AK, O_AV, O_BCQ, O_BCKV, O_KR, O_CQ = 0, 256, 384, 512, 768, 896, 1024
O_KC, O_VC, O_KS, O_VS, O_KW, O_VW, O_GF, W_PROJ = 1536, 1664, 1792, 1920, 2048, 2176, 2304, 2432


def _cparams(n_grid):
    return pltpu.CompilerParams(dimension_semantics=("arbitrary",) * n_grid,
                                vmem_limit_bytes=VMEM_LIMIT)


def _dot(a, b):
    return jnp.dot(a, b, preferred_element_type=F32)


def _dot_t(a, b):
    return lax.dot_general(a, b, (((1,), (1,)), ((), ())), preferred_element_type=F32)


def _silu(x):
    return x * jax.nn.sigmoid(x)


def _half_masks(dtype):
    lane = lax.broadcasted_iota(jnp.int32, (1, LANES), 1)
    lo = jnp.where(lane < HEAD, 1.0, 0.0).astype(dtype)
    hi = jnp.where(lane >= HEAD, 1.0, 0.0).astype(dtype)
    return lo, hi


def _ada_kernel(c_ref, w_ref, b_ref, o_ref):
    cond = _silu(c_ref[...]).astype(BF16)
    o_ref[0] = _dot(cond, w_ref[0].astype(BF16)) + b_ref[0]


def _ada_call(c, ada_w, ada_b):
    depth, d, nm = ada_w.shape
    bsz = c.shape[0]
    tn = 1024
    return pl.pallas_call(
        _ada_kernel,
        grid=(depth, nm // tn),
        in_specs=[pl.BlockSpec((bsz, d), lambda l, j: (0, 0)),
                  pl.BlockSpec((1, d, tn), lambda l, j: (l, 0, j)),
                  pl.BlockSpec((1, 1, tn), lambda l, j: (l, 0, j))],
        out_specs=pl.BlockSpec((1, bsz, tn), lambda l, j: (l, 0, j)),
        out_shape=jax.ShapeDtypeStruct((depth, bsz, nm), F32),
        compiler_params=_cparams(2),
        name="adaln",
    )(c, ada_w, ada_b.reshape(depth, 1, nm))


def _rope_kernel(pos_ref, f_ref, o_ref):
    pos = pos_ref[0]
    lane = lax.broadcasted_iota(jnp.int32, (1, LANES), 1)
    ang64 = pos * f_ref[0:1, :]
    sign64 = jnp.where((lane & 63) < 32, -1.0, 1.0)
    o_ref[0, :, 0:128] = jnp.cos(ang64)
    o_ref[0, :, 128:256] = jnp.sin(ang64) * sign64
    ang32 = pos * f_ref[1:2, :]
    sign32 = jnp.where(lane < 80, -1.0, 1.0)
    o_ref[0, :, 256:384] = jnp.cos(ang32)
    o_ref[0, :, 384:512] = jnp.sin(ang32) * sign32


def _rope_call(positions):
    bsz, s = positions.shape
    ts = min(512, s)
    inv64 = 1.0 / (ROPE_THETA ** (jnp.arange(0, HEAD, 2, dtype=F32) / HEAD))
    inv32 = 1.0 / (ROPE_THETA ** (jnp.arange(0, B_ROPE, 2, dtype=F32) / B_ROPE))
    z = jnp.zeros((32,), F32)
    f = jnp.stack([jnp.tile(inv64, 4),
                   jnp.concatenate([z, z, inv32, inv32, z])])
    pos = positions.astype(F32).reshape(bsz, s, 1)
    return pl.pallas_call(
        _rope_kernel,
        grid=(bsz, s // ts),
        in_specs=[pl.BlockSpec((1, ts, 1), lambda b, i: (b, i, 0)),
                  pl.BlockSpec((2, LANES), lambda b, i: (0, 0))],
        out_specs=pl.BlockSpec((1, ts, 512), lambda b, i: (b, i, 0)),
        out_shape=jax.ShapeDtypeStruct((bsz, s, 512), F32),
        compiler_params=_cparams(2),
        name="rope_tables",
    )(pos, f)


def _modulated_norm(h, g, scale, shift):
    y = h * lax.rsqrt(jnp.mean(h * h, axis=-1, keepdims=True) + EPS)
    return (y * g) * (1.0 + scale) + shift


def _ffn_kernel(h_ref, mod_ref, g_ref, wi_ref, wo_ref, o_ref, act_ref, *, mod_row, d_ff, chunk):
    h = h_ref[0]
    u = _modulated_norm(h, g_ref[...], mod_ref[0, mod_row + 1:mod_row + 2, :],
                        mod_ref[0, mod_row:mod_row + 1, :]).astype(BF16)
    for c in range(0, d_ff, chunk):
        gate = _dot(u, wi_ref[:, c:c + chunk])
        up = _dot(u, wi_ref[:, d_ff + c:d_ff + c + chunk])
        act_ref[:, c:c + chunk] = (_silu(gate) * up).astype(BF16)
    y = _dot(act_ref[...], wo_ref[...])
    o_ref[0] = h + (0.5 * mod_ref[0, mod_row + 2:mod_row + 3, :]) * y


def _ffn_call(h, mod, g, w_in, w_out, mod_row):
    bsz, s, d = h.shape
    d_ff = w_out.shape[0]
    ts = min(512, s)
    chunk = 256
    kern = functools.partial(_ffn_kernel, mod_row=mod_row, d_ff=d_ff, chunk=chunk)
    return pl.pallas_call(
        kern,
        grid=(bsz, s // ts),
        in_specs=[pl.BlockSpec((1, ts, d), lambda b, i: (b, i, 0)),
                  pl.BlockSpec((1, N_MOD, d), lambda b, i: (b, 0, 0)),
                  pl.BlockSpec((1, d), lambda b, i: (0, 0)),
                  pl.BlockSpec((d, 2 * d_ff), lambda b, i: (0, 0), pipeline_mode=pl.Buffered(1)),
                  pl.BlockSpec((d_ff, d), lambda b, i: (0, 0), pipeline_mode=pl.Buffered(1))],
        out_specs=pl.BlockSpec((1, ts, d), lambda b, i: (b, i, 0)),
        out_shape=jax.ShapeDtypeStruct(h.shape, F32),
        scratch_shapes=[pltpu.VMEM((ts, d_ff), BF16)],
        compiler_params=_cparams(2),
        name="ffn",
    )(h, mod, g, w_in, w_out)


def _group_mean_sq(x, bd_ref):
    w = x.shape[-1]
    sq = x * x
    hi = sq.astype(BF16)
    lo = (sq - hi.astype(F32)).astype(BF16)
    pieces = []
    step = 256 if w % 256 == 0 else 128
    for c in range(0, w, step):
        bd = bd_ref[0:step, 0:step]
        pieces.append(_dot(hi[:, c:c + step], bd) + _dot(lo[:, c:c + step], bd))
    return pieces[0] if len(pieces) == 1 else jnp.concatenate(pieces, axis=-1)


def _group_norm(x, bd_ref, gain):
    return x * lax.rsqrt(_group_mean_sq(x, bd_ref) + EPS) * gain


def _rope_slab(x, cos, sin, first_half, fwd, bwd):
    rot = jnp.where(first_half, pltpu.roll(x, fwd, axis=1), pltpu.roll(x, bwd, axis=1))
    return x * cos + rot * sin


def _rope64(x, cos, sin):
    lane = lax.broadcasted_iota(jnp.int32, (1, LANES), 1)
    first = (lane & 63) < 32
    out = [_rope_slab(x[:, c:c + LANES], cos, sin, first, 96, 32) for c in range(0, x.shape[-1], LANES)]
    return out[0] if len(out) == 1 else jnp.concatenate(out, axis=-1)


def _rope32(x, cos, sin):
    lane = lax.broadcasted_iota(jnp.int32, (1, LANES), 1)
    first = lane < 80
    out = [_rope_slab(x[:, c:c + LANES], cos, sin, first, 112, 16) for c in range(0, x.shape[-1], LANES)]
    return out[0] if len(out) == 1 else jnp.concatenate(out, axis=-1)


def _prep_kernel(h_ref, mod_ref, g_ref, w_ref, wuq_ref, wukv_ref, gv_ref, bd64_ref, bdq_ref, tab_ref,
                 qa_ref, ka_ref, va_ref, qb_ref, kb_ref, vb_ref, qc_ref, qcr_ref,
                 kc_ref, vc_ref, ks_ref, vs_ref, kw_ref, vw_ref, gf_ref):
    u = _modulated_norm(h_ref[0], g_ref[...], mod_ref[0, 4:5, :], mod_ref[0, 3:4, :]).astype(BF16)
    cos64, sin64 = tab_ref[0, :, 0:128], tab_ref[0, :, 128:256]
    cos32, sin32 = tab_ref[0, :, 256:384], tab_ref[0, :, 384:512]

    def proj(a, b):
        return _dot(u, w_ref[:, a:b])

    qa = _rope64(_group_norm(proj(O_AQ, O_AK), bd64_ref, gv_ref[0:1, 0:256]), cos64, sin64)
    qa_ref[0] = (qa * 0.125).astype(BF16)
    ka = _rope64(_group_norm(proj(O_AK, O_AV), bd64_ref, gv_ref[1:2, 0:128]), cos64, sin64)
    ka_ref[0] = ka.astype(BF16)
    va_ref[0] = proj(O_AV, O_BCQ).astype(BF16)

    cq = proj(O_BCQ, O_BCKV)
    cq = cq * lax.rsqrt(jnp.mean(cq * cq, axis=-1, keepdims=True) + EPS) * gv_ref[8:9, 0:256]
    qb = _group_norm(_dot(cq.astype(BF16), wuq_ref[...]), bdq_ref, gv_ref[5:6, :])
    qb_ref[0] = _rope32(qb, cos32, sin32).astype(BF16)
    ckv = proj(O_BCKV, O_KR)
    ckv = (ckv * lax.rsqrt(jnp.mean(ckv * ckv, axis=-1, keepdims=True) + EPS) * gv_ref[9:10, 0:128]).astype(BF16)
    k_nope = _group_norm(_dot(ckv, wukv_ref[:, 0:512]), bd64_ref, gv_ref[6:7, :])
    vb_ref[0] = _dot(ckv, wukv_ref[:, 512:768]).astype(BF16)
    k_pe = _rope32(_group_norm(proj(O_KR, O_CQ), bdq_ref, gv_ref[7:8, 0:128]), cos32, sin32)
    for hh in range(B_HEADS):
        kb_ref[0, :, hh * LANES:(hh + 1) * LANES] = (k_nope[:, hh * LANES:(hh + 1) * LANES] + k_pe).astype(BF16)

    qc = _group_norm(proj(O_CQ, O_KC), bd64_ref, gv_ref[2:3, :]) * 0.125
    qc_ref[0] = qc.astype(BF16)
    qcr_ref[0] = _rope64(qc, cos64, sin64).astype(BF16)
    kc_ref[0] = proj(O_KC, O_VC)
    vc_ref[0] = proj(O_VC, O_KS)
    ks = _rope64(_group_norm(proj(O_KS, O_VS), bd64_ref, gv_ref[3:4, 0:128]), cos64, sin64)
    ks_ref[0] = ks.astype(BF16)
    vs_ref[0] = proj(O_VS, O_KW).astype(BF16)
    kw = _rope64(_group_norm(proj(O_KW, O_VW), bd64_ref, gv_ref[4:5, 0:128]), cos64, sin64)
    kw_ref[0] = kw.astype(BF16)
    vw_ref[0] = proj(O_VW, O_GF).astype(BF16)
    gf_ref[0] = proj(O_GF, W_PROJ)


def _prep_call(h, mod, g, w, wuq, wukv, gv, bd64, bdq, tab):
    bsz, s, d = h.shape
    ts = min(512, s)
    widths = [(256, BF16), (128, BF16), (128, BF16), (512, BF16), (512, BF16), (256, BF16),
              (512, BF16), (512, BF16), (128, F32), (128, F32), (128, BF16), (128, BF16),
              (128, BF16), (128, BF16), (128, F32)]

    def const(shape):
        return pl.BlockSpec(shape, lambda b, i: (0,) * len(shape))

    return pl.pallas_call(
        _prep_kernel,
        grid=(bsz, s // ts),
        in_specs=[pl.BlockSpec((1, ts, d), lambda b, i: (b, i, 0)),
                  pl.BlockSpec((1, N_MOD, d), lambda b, i: (b, 0, 0)),
                  const((1, d)), const(w.shape), const(wuq.shape), const(wukv.shape),
                  const(gv.shape), const(bd64.shape), const(bdq.shape),
                  pl.BlockSpec((1, ts, 512), lambda b, i: (b, i, 0))],
        out_specs=[pl.BlockSpec((1, ts, wd), lambda b, i: (b, i, 0)) for wd, _ in widths],
        out_shape=[jax.ShapeDtypeStruct((bsz, s, wd), dt) for wd, dt in widths],
        compiler_params=_cparams(2),
        name="mixer_proj",
    )(h, mod, g, w, wuq, wukv, gv, bd64, bdq, tab)


def _compress_kernel(kc_ref, vc_ref, pe_ref, w1_ref, w2_ref, bd64_ref, g_ref, ko_ref, vo_ref):
    n = kc_ref.shape[1]

    def one(z_ref, idx):
        z = z_ref[0]
        top = _dot((z + pe_ref[idx, 0:1, :]).astype(BF16), w1_ref[idx, 0])
        bot = _dot((z + pe_ref[idx, 1:2, :]).astype(BF16), w1_ref[idx, 1])
        pre = top + pltpu.roll(bot, n - 1, axis=0)
        return _dot(_silu(pre).astype(BF16), w2_ref[idx])

    ko_ref[0] = _group_norm(one(kc_ref, 0), bd64_ref, g_ref[...]).astype(BF16)
    vo_ref[0] = one(vc_ref, 1).astype(BF16)


def _compress_call(kc, vc, pe, w1, w2, bd64, g):
    bsz, n, wz = kc.shape

    def const(shape):
        return pl.BlockSpec(shape, lambda b: (0,) * len(shape))

    return pl.pallas_call(
        _compress_kernel,
        grid=(bsz,),
        in_specs=[pl.BlockSpec((1, n, wz), lambda b: (b, 0, 0)),
                  pl.BlockSpec((1, n, wz), lambda b: (b, 0, 0)),
                  const(pe.shape), const(w1.shape), const(w2.shape), const(bd64.shape), const(g.shape)],
        out_specs=[pl.BlockSpec((1, n, LANES), lambda b: (b, 0, 0))] * 2,
        out_shape=[jax.ShapeDtypeStruct((bsz, n, LANES), BF16)] * 2,
        compiler_params=_cparams(1),
        name="nsa_compress",
    )(kc, vc, pe, w1, w2, bd64, g)


def _window_kernel(*refs, window, tq, n_slab, n_group, use_sink):
    if use_sink:
        sink_ref, q_ref, k_ref, v_ref, o_ref = refs
    else:
        q_ref, k_ref, v_ref, o_ref = refs
    span = window + tq
    q0 = pl.program_id(1) * tq
    start = pl.multiple_of(jnp.maximum(q0 - window, 0), LANES)
    kwin = k_ref[0, pl.ds(start, span), :]
    vwin = v_ref[0, pl.ds(start, span), :]
    rel = (q0 + lax.broadcasted_iota(jnp.int32, (tq, span), 0)) - \
          (start + lax.broadcasted_iota(jnp.int32, (tq, span), 1))
    visible = (rel >= 0) & (rel < window)
    m_lo, m_hi = _half_masks(BF16)
    lane = lax.broadcasted_iota(jnp.int32, (1, LANES), 1)
    for s in range(n_slab):
        qs = q_ref[0, :, s * LANES:(s + 1) * LANES]
        outs = []
        for kv, hm in enumerate((m_lo, m_hi)):
            sc = jnp.where(visible, _dot_t(qs * hm, kwin), NEG_INF)
            m = jnp.max(sc, axis=-1, keepdims=True)
            if use_sink:
                sink = sink_ref[kv * n_group + s]
                m = jnp.maximum(m, sink)
            p = jnp.exp(sc - m)
            denom = jnp.sum(p, axis=-1, keepdims=True)
            if use_sink:
                denom = denom + jnp.exp(sink - m)
            outs.append(_dot(p.astype(BF16), vwin) / denom)
        o_ref[0, :, s * LANES:(s + 1) * LANES] = jnp.where(lane < HEAD, outs[0], outs[1]).astype(o_ref.dtype)


def _window_call(q, k, v, sinks, window, out_dtype, name):
    bsz, s, wq = q.shape
    tq = min(256, s)
    n_slab = wq // LANES
    use_sink = sinks is not None
    kern = functools.partial(_window_kernel, window=window, tq=tq, n_slab=n_slab, n_group=n_slab,
                             use_sink=use_sink)
    in_specs = [pl.BlockSpec((1, tq, wq), lambda b, i: (b, i, 0)),
                pl.BlockSpec((1, s, LANES), lambda b, i: (b, 0, 0)),
                pl.BlockSpec((1, s, LANES), lambda b, i: (b, 0, 0))]
    args = [q, k, v]
    if use_sink:
        in_specs = [pl.BlockSpec(memory_space=pltpu.SMEM)] + in_specs
        args = [sinks] + args
    return pl.pallas_call(
        kern,
        grid=(bsz, s // tq),
        in_specs=in_specs,
        out_specs=pl.BlockSpec((1, tq, wq), lambda b, i: (b, i, 0)),
        out_shape=jax.ShapeDtypeStruct((bsz, s, wq), out_dtype),
        compiler_params=_cparams(2),
        name=name,
    )(*args)


def _online_step(sc, vt, m, l, acc):
    m_new = jnp.maximum(m, jnp.max(sc, axis=-1, keepdims=True))
    alpha = jnp.exp(m - m_new)
    p = jnp.exp(sc - m_new)
    l = alpha * l + jnp.sum(p, axis=-1, keepdims=True)
    acc = alpha * acc + _dot(p.astype(BF16), vt)
    return m_new, l, acc


def _mla_kernel(q_ref, k_ref, v_ref, o_ref, *, tq, scale):
    qi = pl.program_id(1)
    lane = lax.broadcasted_iota(jnp.int32, (1, LANES), 1)
    causal = lax.broadcasted_iota(jnp.int32, (tq, tq), 1) <= lax.broadcasted_iota(jnp.int32, (tq, tq), 0)
    outs = []
    for hh in range(B_HEADS):
        qh = q_ref[0, :, hh * LANES:(hh + 1) * LANES]
        kcol = slice(hh * LANES, (hh + 1) * LANES)
        vcol = slice((hh // 2) * LANES, (hh // 2 + 1) * LANES)

        def tile(j, mask):
            rows = pl.ds(pl.multiple_of(j * tq, tq), tq)
            sc = _dot_t(qh, k_ref[0, rows, kcol]) * scale
            if mask is not None:
                sc = jnp.where(mask, sc, NEG_INF)
            return sc, v_ref[0, rows, vcol]

        def body(j, carry):
            sc, vt = tile(j, None)
            return _online_step(sc, vt, *carry)

        init = (jnp.full((tq, 1), NEG_INF, F32), jnp.zeros((tq, 1), F32), jnp.zeros((tq, LANES), F32))
        carry = lax.fori_loop(0, qi, body, init)
        sc, vt = tile(qi, causal)
        _, l, acc = _online_step(sc, vt, *carry)
        outs.append(acc / l)
    for s in range(B_HEADS // 2):
        o_ref[0, :, s * LANES:(s + 1) * LANES] = \
            jnp.where(lane < HEAD, outs[2 * s], outs[2 * s + 1]).astype(o_ref.dtype)


def _mla_call(q, k, v):
    bsz, s, _ = q.shape
    tq = min(256, s)
    kern = functools.partial(_mla_kernel, tq=tq, scale=float((B_NOPE + B_ROPE) ** -0.5))
    return pl.pallas_call(
        kern,
        grid=(bsz, s // tq),
        in_specs=[pl.BlockSpec((1, tq, 512), lambda b, i: (b, i, 0)),
                  pl.BlockSpec((1, s, 512), lambda b, i: (b, 0, 0)),
                  pl.BlockSpec((1, s, 256), lambda b, i: (b, 0, 0))],
        out_specs=pl.BlockSpec((1, tq, 256), lambda b, i: (b, i, 0)),
        out_shape=jax.ShapeDtypeStruct((bsz, s, 256), BF16),
        compiler_params=_cparams(2),
        name="mla_attention",
    )(q, k, v)


def _cmp_select_kernel(q_ref, k_ref, v_ref, ov_ref, o_ref, sel_ref, *, tq):
    n = k_ref.shape[1]
    t = pl.program_id(1) * tq + lax.broadcasted_iota(jnp.int32, (tq, n), 0)
    j = lax.broadcasted_iota(jnp.int32, (tq, n), 1)
    visible = (j * C_CMP_STRIDE + (C_CMP_BLOCK - 1)) <= t
    kc, vc = k_ref[0], v_ref[0]
    m_lo, m_hi = _half_masks(BF16)
    lane = lax.broadcasted_iota(jnp.int32, (1, LANES), 1)
    p_sum = [jnp.zeros((tq, n), F32), jnp.zeros((tq, n), F32)]
    for s in range(C_HEADS // C_KV):
        qs = q_ref[0, :, s * LANES:(s + 1) * LANES]
        outs = []
        for kv, hm in enumerate((m_lo, m_hi)):
            sc = jnp.where(visible, _dot_t(qs * hm, kc), NEG_INF)
            m = jnp.max(sc, axis=-1, keepdims=True)
            p = jnp.where(visible, jnp.exp(sc - m), 0.0)
            denom = jnp.sum(p, axis=-1, keepdims=True)
            p = p * jnp.where(denom > 0.0, 1.0 / denom, 0.0)
            p_sum[kv] = p_sum[kv] + p
            outs.append(_dot(p.astype(BF16), vc))
        o_ref[0, :, s * LANES:(s + 1) * LANES] = jnp.where(lane < HEAD, outs[0], outs[1])

    imp = jnp.zeros((tq, LANES), F32)
    for kv in range(C_KV):
        hi = p_sum[kv].astype(BF16)
        lo = (p_sum[kv] - hi.astype(F32)).astype(BF16)
        imp = imp + _dot(hi, ov_ref[kv]) + _dot(lo, ov_ref[kv])
    tl = pl.program_id(1) * tq + lax.broadcasted_iota(jnp.int32, (tq, LANES), 0)
    lane_f = lax.broadcasted_iota(jnp.int32, (tq, LANES), 1)
    blk = lane_f & (HEAD - 1)
    cur = tl >> 6
    forced = (blk == 0) | (blk == cur) | (blk == cur - 1)
    future = blk > cur
    score = jnp.where(future, NEG_INF, jnp.where(forced, FORCE_SCORE, imp))
    sel = jnp.zeros((tq, LANES), F32)
    lane_id = lane_f.astype(F32)
    for kv in range(C_KV):
        in_group = (lane_f >= kv * HEAD) & (lane_f < (kv + 1) * HEAD)
        sc = jnp.where(in_group, score, REMOVED)
        for _ in range(C_N_SEL):
            best = jnp.max(sc, axis=-1, keepdims=True)
            first = jnp.min(jnp.where(sc == best, lane_id, 2.0 * LANES), axis=-1, keepdims=True)
            hit = lane_id == first
            sel = jnp.where(hit, 1.0, sel)
            sc = jnp.where(hit, REMOVED, sc)
    sel_ref[0] = jnp.where(future, 0.0, sel).astype(BF16)


def _cmp_select_call(q, kcmp, vcmp, ov):
    bsz, s, wq = q.shape
    n = kcmp.shape[1]
    tq = min(256, s)
    kern = functools.partial(_cmp_select_kernel, tq=tq)
    return pl.pallas_call(
        kern,
        grid=(bsz, s // tq),
        in_specs=[pl.BlockSpec((1, tq, wq), lambda b, i: (b, i, 0)),
                  pl.BlockSpec((1, n, LANES), lambda b, i: (b, 0, 0)),
                  pl.BlockSpec((1, n, LANES), lambda b, i: (b, 0, 0)),
                  pl.BlockSpec(ov.shape, lambda b, i: (0, 0, 0))],
        out_specs=[pl.BlockSpec((1, tq, wq), lambda b, i: (b, i, 0)),
                   pl.BlockSpec((1, tq, LANES), lambda b, i: (b, i, 0))],
        out_shape=[jax.ShapeDtypeStruct((bsz, s, wq), F32),
                   jax.ShapeDtypeStruct((bsz, s, LANES), BF16)],
        compiler_params=_cparams(2),
        name="nsa_cmp_select",
    )(q, kcmp, vcmp, ov)


def _slc_kernel(q_ref, k_ref, v_ref, sel_ref, e_ref, o_ref, *, tq):
    qi = pl.program_id(1)
    lane = lax.broadcasted_iota(jnp.int32, (1, LANES), 1)
    causal = lax.broadcasted_iota(jnp.int32, (tq, tq), 1) <= lax.broadcasted_iota(jnp.int32, (tq, tq), 0)
    m_lo, m_hi = _half_masks(BF16)
    sel = sel_ref[0]
    for s in range(C_HEADS // C_KV):
        qs = q_ref[0, :, s * LANES:(s + 1) * LANES]
        outs = []
        for hm in (m_lo, m_hi):
            qh = qs * hm
            sel_h = sel * hm

            def tile(j, diag):
                rows = pl.ds(pl.multiple_of(j * tq, tq), tq)
                chosen = _dot(sel_h, e_ref[j]) > 0.5
                if diag:
                    chosen = chosen & causal
                sc = jnp.where(chosen, _dot_t(qh, k_ref[0, rows, :]), NEG_INF)
                return sc, v_ref[0, rows, :]

            def body(j, carry):
                sc, vt = tile(j, False)
                return _online_step(sc, vt, *carry)

            init = (jnp.full((tq, 1), NEG_INF, F32), jnp.zeros((tq, 1), F32), jnp.zeros((tq, LANES), F32))
            carry = lax.fori_loop(0, qi, body, init)
            sc, vt = tile(qi, True)
            _, l, acc = _online_step(sc, vt, *carry)
            outs.append(acc / l)
        o_ref[0, :, s * LANES:(s + 1) * LANES] = jnp.where(lane < HEAD, outs[0], outs[1])


def _slc_call(q, k, v, sel, e):
    bsz, s, wq = q.shape
    tq = e.shape[2]
    kern = functools.partial(_slc_kernel, tq=tq)
    return pl.pallas_call(
        kern,
        grid=(bsz, s // tq),
        in_specs=[pl.BlockSpec((1, tq, wq), lambda b, i: (b, i, 0)),
                  pl.BlockSpec((1, s, LANES), lambda b, i: (b, 0, 0)),
                  pl.BlockSpec((1, s, LANES), lambda b, i: (b, 0, 0)),
                  pl.BlockSpec((1, tq, LANES), lambda b, i: (b, i, 0)),
                  pl.BlockSpec(e.shape, lambda b, i: (0, 0, 0))],
        out_specs=pl.BlockSpec((1, tq, wq), lambda b, i: (b, i, 0)),
        out_shape=jax.ShapeDtypeStruct((bsz, s, wq), F32),
        compiler_params=_cparams(2),
        name="nsa_selected",
    )(q, k, v, sel, e)


def _merge_kernel(h_ref, mod_ref, oa_ref, ob_ref, ocmp_ref, oslc_ref, owin_ref, gf_ref, wg_ref, bg_ref,
                  wo_ref, o_ref, cat_ref):
    wc = ocmp_ref.shape[2]
    gates = jax.nn.sigmoid(_dot(_silu(gf_ref[0]).astype(BF16), wg_ref[...]) + bg_ref[...])
    oc = gates[:, 0:wc] * ocmp_ref[0] + gates[:, wc:2 * wc] * oslc_ref[0] + gates[:, 2 * wc:3 * wc] * owin_ref[0]
    cat_ref[:, 0:256] = oa_ref[0]
    cat_ref[:, 256:512] = ob_ref[0]
    cat_ref[:, 512:512 + wc] = oc.astype(BF16)
    o_ref[0] = h_ref[0] + mod_ref[0, 5:6, :] * _dot(cat_ref[...], wo_ref[...])


def _merge_call(h, mod, oa, ob, ocmp, oslc, owin, gf, wg, bg, wo):
    bsz, s, d = h.shape
    ts = min(512, s)

    def tok(wd):
        return pl.BlockSpec((1, ts, wd), lambda b, i: (b, i, 0))

    def const(shape):
        return pl.BlockSpec(shape, lambda b, i: (0,) * len(shape))

    return pl.pallas_call(
        _merge_kernel,
        grid=(bsz, s // ts),
        in_specs=[tok(d), pl.BlockSpec((1, N_MOD, d), lambda b, i: (b, 0, 0)),
                  tok(256), tok(256), tok(512), tok(512), tok(512), tok(LANES),
                  const(wg.shape), const(bg.shape), const(wo.shape)],
        out_specs=tok(d),
        out_shape=jax.ShapeDtypeStruct(h.shape, F32),
        scratch_shapes=[pltpu.VMEM((ts, wo.shape[0]), BF16)],
        compiler_params=_cparams(2),
        name="mixer_merge",
    )(h, mod, oa, ob, ocmp, oslc, owin, gf, wg, bg, wo)


def _cols(w, pieces):
    out = [jnp.zeros((w.shape[0], p), w.dtype) if isinstance(p, int) else w[:, p[0]:p[1]] for p in pieces]
    return jnp.concatenate(out, axis=1)


def _head_cols(base, order):
    return [(base + hd * HEAD, base + (hd + 1) * HEAD) for hd in order]


A_ORDER = (0, 2, 1, 3)
C_ORDER = (0, 4, 1, 5, 2, 6, 3, 7)


def _proj_weight(w):
    pieces = (_head_cols(0, A_ORDER) + [(256, 384), (384, 512), (512, 768), (768, 896)]
              + [64, (896, 928), 32] + _head_cols(928, C_ORDER)
              + [(1440 + 128 * i, 1568 + 128 * i) for i in range(6)] + [(2208, 2272), 64])
    return _cols(w, pieces).astype(BF16)


def _mla_weights(w_uq, w_ukv):
    dq = B_NOPE + B_ROPE
    uq = []
    for hd in range(B_HEADS):
        uq += [(hd * dq, hd * dq + dq), 32]
    dk = B_NOPE + B_V
    uk = []
    for hd in range(B_HEADS):
        uk += [(hd * dk, hd * dk + B_NOPE), 64]
    uv = [(hd * dk + B_NOPE, (hd + 1) * dk) for hd in range(B_HEADS)]
    return _cols(w_uq, uq).astype(BF16), _cols(w_ukv, uk + uv).astype(BF16)


def _tile_to(v, reps, width=512):
    t = jnp.tile(v, reps)
    return jnp.pad(t, (0, width - t.shape[0]))


def _gain_rows(a_qk_g, b_q_lat_g, b_kv_lat_g, b_qk_nope_g, b_qk_rope_g, c_qk_g):
    z32, z64 = jnp.zeros((32,), F32), jnp.zeros((64,), F32)
    rows = [_tile_to(a_qk_g[0], 4), _tile_to(a_qk_g[1], 2), _tile_to(c_qk_g[0], 8),
            _tile_to(c_qk_g[2], 2), _tile_to(c_qk_g[3], 2),
            jnp.tile(jnp.concatenate([b_qk_nope_g[0], b_qk_rope_g[0], z32]), 4),
            jnp.tile(jnp.concatenate([b_qk_nope_g[1], z64]), 4),
            _tile_to(jnp.concatenate([z64, b_qk_rope_g[1], z32]), 1),
            _tile_to(b_q_lat_g, 1), _tile_to(b_kv_lat_g, 1)]
    rows += [jnp.zeros((512,), F32)] * (16 - len(rows))
    return jnp.stack(rows)


def _block_diag_consts():
    i = np.arange(256)
    bd64 = np.where((i[:, None] // 64) == (i[None, :] // 64), 1.0 / 64, 0.0)
    j = i % 128
    grp = np.where(j < 64, 0, np.where(j < 96, 1, 2)) + 3 * (i // 128)
    size = np.where(j < 64, 64.0, 32.0)
    bdq = np.where(grp[:, None] == grp[None, :], 1.0 / size[None, :], 0.0)
    return jnp.asarray(bd64, BF16), jnp.asarray(bdq, BF16)


def _compress_weights(pe, w1, w2):
    n_tok = C_CMP_STRIDE
    pe_rows = jnp.broadcast_to(pe.reshape(2, 2, n_tok, 1, HEAD), (2, 2, n_tok, C_KV, HEAD))
    pe_rows = pe_rows.reshape(2, 2, n_tok * C_KV * HEAD)
    w1r = w1.reshape(2, 2, n_tok, HEAD, C_CMP_HID)
    eye = jnp.eye(C_KV, dtype=F32)
    w1big = jnp.einsum('xytdj,kq->xytkdqj', w1r, eye).reshape(2, 2, n_tok * C_KV * HEAD, C_KV * C_CMP_HID)
    w2big = jnp.einsum('xjd,kq->xkjqd', w2, eye).reshape(2, C_KV * C_CMP_HID, C_KV * HEAD)
    return pe_rows, w1big.astype(BF16), w2big.astype(BF16)


def _overlap_const(n_chunk):
    n_slc = n_chunk * C_CMP_STRIDE // C_SLC_BLOCK
    ci = np.arange(n_chunk)[:, None] * C_CMP_STRIDE
    sj = np.arange(n_slc)[None, :] * C_SLC_BLOCK
    ov = ((ci <= sj + C_SLC_BLOCK - 1) & (ci + C_CMP_BLOCK - 1 >= sj)).astype(np.float32)
    ov[n_chunk - 1] = 0.0
    out = np.zeros((C_KV, n_chunk, LANES), np.float32)
    for kv in range(C_KV):
        out[kv, :, kv * HEAD:kv * HEAD + n_slc] = ov
    return jnp.asarray(out, BF16)


def _expand_const(s, tk):
    r = np.arange(LANES)[None, :, None] % HEAD
    c = (np.arange(s // tk)[:, None, None] * tk + np.arange(tk)[None, None, :]) // C_SLC_BLOCK
    return jnp.asarray((r == c).astype(np.float32), BF16)


def _gate_weights(c_gate_w, c_gate_b):
    head_of_lane = np.repeat(np.asarray(C_ORDER), HEAD)
    cols = np.concatenate([head_of_lane * 3 + k for k in range(3)])
    wg = jnp.concatenate([c_gate_w[:, cols], jnp.zeros((LANES - C_GATE_HID, cols.size), F32)], axis=0)
    return wg.astype(BF16), c_gate_b[cols].reshape(1, -1)


def _out_weight(w_out):
    rows = ([(hd * HEAD, (hd + 1) * HEAD) for hd in A_ORDER] + [(256, 512)]
            + [(512 + hd * HEAD, 512 + (hd + 1) * HEAD) for hd in C_ORDER])
    return jnp.concatenate([w_out[a:b] for a, b in rows], axis=0).astype(BF16)


def kernel(x, c, positions, ada_w, ada_b, norm_g, ffn_w_in, ffn_w_out, w_in, w_out, a_sinks, a_qk_g,
           b_q_lat_g, b_kv_lat_g, b_w_uq, b_w_ukv, b_qk_nope_g, b_qk_rope_g, c_qk_g, c_cmp_pe, c_cmp_w1,
           c_cmp_w2, c_gate_w, c_gate_b):
    bsz, s, d = x.shape
    depth = ada_w.shape[0]
    n_chunk = s // C_CMP_STRIDE
    tq = min(256, s)
    assert s % tq == 0 and s >= C_WINDOW + tq and s // C_SLC_BLOCK <= HEAD

    mod_all = _ada_call(c, ada_w, ada_b).reshape(depth, bsz, N_MOD, d)
    tab = _rope_call(positions)
    bd64, bdq = _block_diag_consts()
    ov = _overlap_const(n_chunk)
    e = _expand_const(s, tq)

    h = x
    for l in range(depth):
        mod = mod_all[l]
        h = _ffn_call(h, mod, norm_g[l, 0:1], ffn_w_in[l, 0].astype(BF16), ffn_w_out[l, 0].astype(BF16), 0)

        wuq, wukv = _mla_weights(b_w_uq[l], b_w_ukv[l])
        gv = _gain_rows(a_qk_g[l], b_q_lat_g[l], b_kv_lat_g[l], b_qk_nope_g[l], b_qk_rope_g[l], c_qk_g[l])
        (qa, ka, va, qb, kb, vb, qc, qcr, kc, vc, ks, vs, kw, vw, gf) = _prep_call(
            h, mod, norm_g[l, 1:2], _proj_weight(w_in[l]), wuq, wukv, gv, bd64, bdq, tab)

        pe_rows, w1big, w2big = _compress_weights(c_cmp_pe[l], c_cmp_w1[l], c_cmp_w2[l])
        kcmp, vcmp = _compress_call(kc.reshape(bsz, n_chunk, C_CMP_STRIDE * LANES),
                                    vc.reshape(bsz, n_chunk, C_CMP_STRIDE * LANES),
                                    pe_rows, w1big, w2big, bd64, jnp.tile(c_qk_g[l, 1], 2).reshape(1, LANES))

        oa = _window_call(qa, ka, va, a_sinks[l], A_WINDOW, BF16, "swa_attention")
        ob = _mla_call(qb, kb, vb)
        ocmp, sel = _cmp_select_call(qc, kcmp, vcmp, ov)
        oslc = _slc_call(qcr, ks, vs, sel, e)
        owin = _window_call(qcr, kw, vw, None, C_WINDOW, F32, "nsa_window")

        wg, bg = _gate_weights(c_gate_w[l], c_gate_b[l])
        h = _merge_call(h, mod, oa, ob, ocmp, oslc, owin, gf, wg, bg, _out_weight(w_out[l]))

        h = _ffn_call(h, mod, norm_g[l, 2:3], ffn_w_in[l, 1].astype(BF16), ffn_w_out[l, 1].astype(BF16), 6)
    return h
```

```python
import functools

import numpy as np
import jax
import jax.numpy as jnp
from jax import lax
from jax.experimental import pallas as pl
from jax.experimental.pallas import tpu as pltpu

F32 = jnp.float32
BF16 = jnp.bfloat16

LANES = 128
HEAD = 64
EPS = 1e-6
ROPE_THETA = 10000.0
NEG_INF = -1e30
FORCE_SCORE = 1e4
REMOVED = -3e38
N_MOD = 9

A_HEADS, A_KV, A_WINDOW = 4, 2, 128
B_HEADS, B_Q_LORA, B_KV_LORA, B_NOPE, B_ROPE, B_V = 4, 256, 128, 64, 32, 64
C_HEADS, C_KV, C_WINDOW = 8, 2, 256
C_CMP_BLOCK, C_CMP_STRIDE, C_SLC_BLOCK, C_N_SEL = 32, 16, 64, 8
C_CMP_HID, C_GATE_HID = 128, 64

VMEM_LIMIT = 56 * 1024 * 1024

O_AQ, O_AK, O_AV, O_BCQ, O_BCKV, O_KR, O_CQ = 0, 256, 384, 512, 768, 896, 1024
O_KC, O_VC, O_KS, O_VS, O_KW, O_VW, O_GF, W_PROJ = 1536, 1664, 1792, 1920, 2048, 2176, 2304, 2432


def _cparams(n_grid):
    return pltpu.CompilerParams(dimension_semantics=("arbitrary",) * n_grid,
                                vmem_limit_bytes=VMEM_LIMIT)


def _dot(a, b):
    return jnp.dot(a, b, preferred_element_type=F32)


def _dot_t(a, b):
    return lax.dot_general(a, b, (((1,), (1,)), ((), ())), preferred_element_type=F32)


def _silu(x):
    return x * jax.nn.sigmoid(x)


def _half_masks(dtype):
    lane = lax.broadcasted_iota(jnp.int32, (1, LANES), 1)
    lo = jnp.where(lane < HEAD, 1.0, 0.0).astype(dtype)
    hi = jnp.where(lane >= HEAD, 1.0, 0.0).astype(dtype)
    return lo, hi


def _ada_kernel(c_ref, w_ref, b_ref, o_ref):
    cond = _silu(c_ref[...]).astype(BF16)
    o_ref[0] = _dot(cond, w_ref[0].astype(BF16)) + b_ref[0]


def _ada_call(c, ada_w, ada_b):
    depth, d, nm = ada_w.shape
    bsz = c.shape[0]
    tn = 1024
    return pl.pallas_call(
        _ada_kernel,
        grid=(depth, nm // tn),
        in_specs=[pl.BlockSpec((bsz, d), lambda l, j: (0, 0)),
                  pl.BlockSpec((1, d, tn), lambda l, j: (l, 0, j)),
                  pl.BlockSpec((1, 1, tn), lambda l, j: (l, 0, j))],
        out_specs=pl.BlockSpec((1, bsz, tn), lambda l, j: (l, 0, j)),
        out_shape=jax.ShapeDtypeStruct((depth, bsz, nm), F32),
        compiler_params=_cparams(2),
        name="adaln",
    )(c, ada_w, ada_b.reshape(depth, 1, nm))


def _rope_kernel(pos_ref, f_ref, o_ref):
    pos = pos_ref[0]
    lane = lax.broadcasted_iota(jnp.int32, (1, LANES), 1)
    ang64 = pos * f_ref[0:1, :]
    sign64 = jnp.where((lane & 63) < 32, -1.0, 1.0)
    o_ref[0, :, 0:128] = jnp.cos(ang64)
    o_ref[0, :, 128:256] = jnp.sin(ang64) * sign64
    ang32 = pos * f_ref[1:2, :]
    sign32 = jnp.where(lane < 80, -1.0, 1.0)
    o_ref[0, :, 256:384] = jnp.cos(ang32)
    o_ref[0, :, 384:512] = jnp.sin(ang32) * sign32


def _rope_call(positions):
    bsz, s = positions.shape
    ts = min(512, s)
    inv64 = 1.0 / (ROPE_THETA ** (jnp.arange(0, HEAD, 2, dtype=F32) / HEAD))
    inv32 = 1.0 / (ROPE_THETA ** (jnp.arange(0, B_ROPE, 2, dtype=F32) / B_ROPE))
    z = jnp.zeros((32,), F32)
    f = jnp.stack([jnp.tile(inv64, 4),
                   jnp.concatenate([z, z, inv32, inv32, z])])
    pos = positions.astype(F32).reshape(bsz, s, 1)
    return pl.pallas_call(
        _rope_kernel,
        grid=(bsz, s // ts),
        in_specs=[pl.BlockSpec((1, ts, 1), lambda b, i: (b, i, 0)),
                  pl.BlockSpec((2, LANES), lambda b, i: (0, 0))],
        out_specs=pl.BlockSpec((1, ts, 512), lambda b, i: (b, i, 0)),
        out_shape=jax.ShapeDtypeStruct((bsz, s, 512), F32),
        compiler_params=_cparams(2),
        name="rope_tables",
    )(pos, f)


def _modulated_norm(h, g, scale, shift):
    y = h * lax.rsqrt(jnp.mean(h * h, axis=-1, keepdims=True) + EPS)
    return (y * g) * (1.0 + scale) + shift


def _ffn_kernel(h_ref, mod_ref, g_ref, wi_ref, wo_ref, o_ref, act_ref, *, mod_row, d_ff, chunk):
    h = h_ref[0]
    u = _modulated_norm(h, g_ref[...], mod_ref[0, mod_row + 1:mod_row + 2, :],
                        mod_ref[0, mod_row:mod_row + 1, :]).astype(BF16)
    for c in range(0, d_ff, chunk):
        gate = _dot(u, wi_ref[:, c:c + chunk])
        up = _dot(u, wi_ref[:, d_ff + c:d_ff + c + chunk])
        act_ref[:, c:c + chunk] = (_silu(gate) * up).astype(BF16)
    y = _dot(act_ref[...], wo_ref[...])
    o_ref[0] = h + (0.5 * mod_ref[0, mod_row + 2:mod_row + 3, :]) * y


def _ffn_call(h, mod, g, w_in, w_out, mod_row):
    bsz, s, d = h.shape
    d_ff = w_out.shape[0]
    ts = min(512, s)
    chunk = 256
    kern = functools.partial(_ffn_kernel, mod_row=mod_row, d_ff=d_ff, chunk=chunk)
    return pl.pallas_call(
        kern,
        grid=(bsz, s // ts),
        in_specs=[pl.BlockSpec((1, ts, d), lambda b, i: (b, i, 0)),
                  pl.BlockSpec((1, N_MOD, d), lambda b, i: (b, 0, 0)),
                  pl.BlockSpec((1, d), lambda b, i: (0, 0)),
                  pl.BlockSpec((d, 2 * d_ff), lambda b, i: (0, 0), pipeline_mode=pl.Buffered(1)),
                  pl.BlockSpec((d_ff, d), lambda b, i: (0, 0), pipeline_mode=pl.Buffered(1))],
        out_specs=pl.BlockSpec((1, ts, d), lambda b, i: (b, i, 0)),
        out_shape=jax.ShapeDtypeStruct(h.shape, F32),
        scratch_shapes=[pltpu.VMEM((ts, d_ff), BF16)],
        compiler_params=_cparams(2),
        name="ffn",
    )(h, mod, g, w_in, w_out)


def _group_mean_sq(x, bd_ref):
    w = x.shape[-1]
    sq = x * x
    hi = sq.astype(BF16)
    lo = (sq - hi.astype(F32)).astype(BF16)
    pieces = []
    step = 256 if w % 256 == 0 else 128
    for c in range(0, w, step):
        bd = bd_ref[0:step, 0:step]
        pieces.append(_dot(hi[:, c:c + step], bd) + _dot(lo[:, c:c + step], bd))
    return pieces[0] if len(pieces) == 1 else jnp.concatenate(pieces, axis=-1)


def _group_norm(x, bd_ref, gain):
    return x * lax.rsqrt(_group_mean_sq(x, bd_ref) + EPS) * gain


def _rope_slab(x, cos, sin, first_half, fwd, bwd):
    rot = jnp.where(first_half, pltpu.roll(x, fwd, axis=1), pltpu.roll(x, bwd, axis=1))
    return x * cos + rot * sin


def _rope64(x, cos, sin):
    lane = lax.broadcasted_iota(jnp.int32, (1, LANES), 1)
    first = (lane & 63) < 32
    out = [_rope_slab(x[:, c:c + LANES], cos, sin, first, 96, 32) for c in range(0, x.shape[-1], LANES)]
    return out[0] if len(out) == 1 else jnp.concatenate(out, axis=-1)


def _rope32(x, cos, sin):
    lane = lax.broadcasted_iota(jnp.int32, (1, LANES), 1)
    first = lane < 80
    out = [_rope_slab(x[:, c:c + LANES], cos, sin, first, 112, 16) for c in range(0, x.shape[-1], LANES)]
    return out[0] if len(out) == 1 else jnp.concatenate(out, axis=-1)


def _prep_kernel(h_ref, mod_ref, g_ref, w_ref, wuq_ref, wukv_ref, gv_ref, bd64_ref, bdq_ref, tab_ref,
                 qa_ref, ka_ref, va_ref, qb_ref, kb_ref, vb_ref, qc_ref, qcr_ref,
                 kc_ref, vc_ref, ks_ref, vs_ref, kw_ref, vw_ref, gf_ref):
    u = _modulated_norm(h_ref[0], g_ref[...], mod_ref[0, 4:5, :], mod_ref[0, 3:4, :]).astype(BF16)
    cos64, sin64 = tab_ref[0, :, 0:128], tab_ref[0, :, 128:256]
    cos32, sin32 = tab_ref[0, :, 256:384], tab_ref[0, :, 384:512]

    def proj(a, b):
        return _dot(u, w_ref[:, a:b])

    qa = _rope64(_group_norm(proj(O_AQ, O_AK), bd64_ref, gv_ref[0:1, 0:256]), cos64, sin64)
    qa_ref[0] = (qa * 0.125).astype(BF16)
    ka = _rope64(_group_norm(proj(O_AK, O_AV), bd64_ref, gv_ref[1:2, 0:128]), cos64, sin64)
    ka_ref[0] = ka.astype(BF16)
    va_ref[0] = proj(O_AV, O_BCQ).astype(BF16)

    cq = proj(O_BCQ, O_BCKV)
    cq = cq * lax.rsqrt(jnp.mean(cq * cq, axis=-1, keepdims=True) + EPS) * gv_ref[8:9, 0:256]
    qb = _group_norm(_dot(cq.astype(BF16), wuq_ref[...]), bdq_ref, gv_ref[5:6, :])
    qb_ref[0] = _rope32(qb, cos32, sin32).astype(BF16)
    ckv = proj(O_BCKV, O_KR)
    ckv = (ckv * lax.rsqrt(jnp.mean(ckv * ckv, axis=-1, keepdims=True) + EPS) * gv_ref[9:10, 0:128]).astype(BF16)
    k_nope = _group_norm(_dot(ckv, wukv_ref[:, 0:512]), bd64_ref, gv_ref[6:7, :])
    vb_ref[0] = _dot(ckv, wukv_ref[:, 512:768]).astype(BF16)
    k_pe = _rope32(_group_norm(proj(O_KR, O_CQ), bdq_ref, gv_ref[7:8, 0:128]), cos32, sin32)
    for hh in range(B_HEADS):
        kb_ref[0, :, hh * LANES:(hh + 1) * LANES] = (k_nope[:, hh * LANES:(hh + 1) * LANES] + k_pe).astype(BF16)

    qc = _group_norm(proj(O_CQ, O_KC), bd64_ref, gv_ref[2:3, :]) * 0.125
    qc_ref[0] = qc.astype(BF16)
    qcr_ref[0] = _rope64(qc, cos64, sin64).astype(BF16)
    kc_ref[0] = proj(O_KC, O_VC)
    vc_ref[0] = proj(O_VC, O_KS)
    ks = _rope64(_group_norm(proj(O_KS, O_VS), bd64_ref, gv_ref[3:4, 0:128]), cos64, sin64)
    ks_ref[0] = ks.astype(BF16)
    vs_ref[0] = proj(O_VS, O_KW).astype(BF16)
    kw = _rope64(_group_norm(proj(O_KW, O_VW), bd64_ref, gv_ref[4:5, 0:128]), cos64, sin64)
    kw_ref[0] = kw.astype(BF16)
    vw_ref[0] = proj(O_VW, O_GF).astype(BF16)
    gf_ref[0] = proj(O_GF, W_PROJ)


def _prep_call(h, mod, g, w, wuq, wukv, gv, bd64, bdq, tab):
    bsz, s, d = h.shape
    ts = min(512, s)
    widths = [(256, BF16), (128, BF16), (128, BF16), (512, BF16), (512, BF16), (256, BF16),
              (512, BF16), (512, BF16), (128, F32), (128, F32), (128, BF16), (128, BF16),
              (128, BF16), (128, BF16), (128, F32)]

    def const(shape):
        return pl.BlockSpec(shape, lambda b, i: (0,) * len(shape))

    return pl.pallas_call(
        _prep_kernel,
        grid=(bsz, s // ts),
        in_specs=[pl.BlockSpec((1, ts, d), lambda b, i: (b, i, 0)),
                  pl.BlockSpec((1, N_MOD, d), lambda b, i: (b, 0, 0)),
                  const((1, d)), const(w.shape), const(wuq.shape), const(wukv.shape),
                  const(gv.shape), const(bd64.shape), const(bdq.shape),
                  pl.BlockSpec((1, ts, 512), lambda b, i: (b, i, 0))],
        out_specs=[pl.BlockSpec((1, ts, wd), lambda b, i: (b, i, 0)) for wd, _ in widths],
        out_shape=[jax.ShapeDtypeStruct((bsz, s, wd), dt) for wd, dt in widths],
        compiler_params=_cparams(2),
        name="mixer_proj",
    )(h, mod, g, w, wuq, wukv, gv, bd64, bdq, tab)


def _compress_kernel(kc_ref, vc_ref, pe_ref, w1_ref, w2_ref, bd64_ref, g_ref, ko_ref, vo_ref):
    n = kc_ref.shape[1]

    def one(z_ref, idx):
        z = z_ref[0]
        top = _dot((z + pe_ref[idx, 0:1, :]).astype(BF16), w1_ref[idx, 0])
        bot = _dot((z + pe_ref[idx, 1:2, :]).astype(BF16), w1_ref[idx, 1])
        pre = top + pltpu.roll(bot, n - 1, axis=0)
        return _dot(_silu(pre).astype(BF16), w2_ref[idx])

    ko_ref[0] = _group_norm(one(kc_ref, 0), bd64_ref, g_ref[...]).astype(BF16)
    vo_ref[0] = one(vc_ref, 1).astype(BF16)


def _compress_call(kc, vc, pe, w1, w2, bd64, g):
    bsz, n, wz = kc.shape

    def const(shape):
        return pl.BlockSpec(shape, lambda b: (0,) * len(shape))

    return pl.pallas_call(
        _compress_kernel,
        grid=(bsz,),
        in_specs=[pl.BlockSpec((1, n, wz), lambda b: (b, 0, 0)),
                  pl.BlockSpec((1, n, wz), lambda b: (b, 0, 0)),
                  const(pe.shape), const(w1.shape), const(w2.shape), const(bd64.shape), const(g.shape)],
        out_specs=[pl.BlockSpec((1, n, LANES), lambda b: (b, 0, 0))] * 2,
        out_shape=[jax.ShapeDtypeStruct((bsz, n, LANES), BF16)] * 2,
        compiler_params=_cparams(1),
        name="nsa_compress",
    )(kc, vc, pe, w1, w2, bd64, g)


def _window_kernel(*refs, window, tq, n_slab, n_group, use_sink):
    if use_sink:
        sink_ref, q_ref, k_ref, v_ref, o_ref = refs
    else:
        q_ref, k_ref, v_ref, o_ref = refs
    span = window + tq
    q0 = pl.program_id(1) * tq
    start = pl.multiple_of(jnp.maximum(q0 - window, 0), LANES)
    kwin = k_ref[0, pl.ds(start, span), :]
    vwin = v_ref[0, pl.ds(start, span), :]
    rel = (q0 + lax.broadcasted_iota(jnp.int32, (tq, span), 0)) - \
          (start + lax.broadcasted_iota(jnp.int32, (tq, span), 1))
    visible = (rel >= 0) & (rel < window)
    m_lo, m_hi = _half_masks(BF16)
    lane = lax.broadcasted_iota(jnp.int32, (1, LANES), 1)
    for s in range(n_slab):
        qs = q_ref[0, :, s * LANES:(s + 1) * LANES]
        outs = []
        for kv, hm in enumerate((m_lo, m_hi)):
            sc = jnp.where(visible, _dot_t(qs * hm, kwin), NEG_INF)
            m = jnp.max(sc, axis=-1, keepdims=True)
            if use_sink:
                sink = sink_ref[kv * n_group + s]
                m = jnp.maximum(m, sink)
            p = jnp.exp(sc - m)
            denom = jnp.sum(p, axis=-1, keepdims=True)
            if use_sink:
                denom = denom + jnp.exp(sink - m)
            outs.append(_dot(p.astype(BF16), vwin) / denom)
        o_ref[0, :, s * LANES:(s + 1) * LANES] = jnp.where(lane < HEAD, outs[0], outs[1]).astype(o_ref.dtype)


def _window_call(q, k, v, sinks, window, out_dtype, name):
    bsz, s, wq = q.shape
    tq = min(256, s)
    n_slab = wq // LANES
    use_sink = sinks is not None
    kern = functools.partial(_window_kernel, window=window, tq=tq, n_slab=n_slab, n_group=n_slab,
                             use_sink=use_sink)
    in_specs = [pl.BlockSpec((1, tq, wq), lambda b, i: (b, i, 0)),
                pl.BlockSpec((1, s, LANES), lambda b, i: (b, 0, 0)),
                pl.BlockSpec((1, s, LANES), lambda b, i: (b, 0, 0))]
    args = [q, k, v]
    if use_sink:
        in_specs = [pl.BlockSpec(memory_space=pltpu.SMEM)] + in_specs
        args = [sinks] + args
    return pl.pallas_call(
        kern,
        grid=(bsz, s // tq),
        in_specs=in_specs,
        out_specs=pl.BlockSpec((1, tq, wq), lambda b, i: (b, i, 0)),
        out_shape=jax.ShapeDtypeStruct((bsz, s, wq), out_dtype),
        compiler_params=_cparams(2),
        name=name,
    )(*args)


LOG2E = 1.4426950408889634


FLASH_TQ, FLASH_TK = 128, 512


def _softmax_state(n_heads, tq):
    return ((jnp.full((tq, 1), NEG_INF, F32), jnp.zeros((tq, LANES), F32)),) * n_heads


def _online_step(state, sc, vt, c):
    m_prev, acc = state
    m_new = jnp.maximum(m_prev, jnp.max(sc, axis=-1, keepdims=True))
    alpha = jnp.exp2((m_prev - m_new) * c)
    p = jnp.exp2((sc - m_new) * c)
    return m_new, alpha * acc + _dot(p.astype(BF16), vt)


def _normalised(acc, value_half):
    lane = lax.broadcasted_iota(jnp.int32, (1, LANES), 1)
    denom_lane = (1 - value_half) * HEAD
    denom = jnp.sum(jnp.where(lane == denom_lane, acc, 0.0), axis=-1, keepdims=True)
    return acc / denom


def _causal_tile(q0, k0, tq, tk):
    return (k0 + lax.broadcasted_iota(jnp.int32, (tq, tk), 1)) <= (q0 + lax.broadcasted_iota(jnp.int32, (tq, tk), 0))


def _mla_kernel(q_ref, k_ref, v_ref, o_ref, *, tq, tk, c):
    q0 = pl.program_id(1) * tq
    n_full = q0 // tk
    lane = lax.broadcasted_iota(jnp.int32, (1, LANES), 1)
    m_lo, m_hi = _half_masks(BF16)

    def step(j, states, diagonal):
        rows = pl.ds(pl.multiple_of(j * tk, tk), tk)
        if diagonal:
            causal = _causal_tile(q0, j * tk, tq, tk)
        scs = [_dot_t(q_ref[0, :, hh * LANES:(hh + 1) * LANES], k_ref[0, rows, hh * LANES:(hh + 1) * LANES])
               for hh in range(B_HEADS)]
        out = []
        for hh in range(B_HEADS):
            sc = jnp.where(causal, scs[hh], NEG_INF) if diagonal else scs[hh]
            v = v_ref[0, rows, (hh // 2) * LANES:(hh // 2 + 1) * LANES]
            vt = v * m_lo + m_hi if hh % 2 == 0 else v * m_hi + m_lo
            out.append(_online_step(states[hh], sc, vt, c))
        return tuple(out)

    states = lax.fori_loop(0, n_full, lambda j, st: step(j, st, False), _softmax_state(B_HEADS, tq))
    states = step(n_full, states, True)
    for s in range(B_HEADS // 2):
        o_lo = _normalised(states[2 * s][1], 0)
        o_hi = _normalised(states[2 * s + 1][1], 1)
        o_ref[0, :, s * LANES:(s + 1) * LANES] = jnp.where(lane < HEAD, o_lo, o_hi).astype(o_ref.dtype)


def _mla_call(q, k, v):
    bsz, s, _ = q.shape
    tq, tk = FLASH_TQ, FLASH_TK
    kern = functools.partial(_mla_kernel, tq=tq, tk=tk, c=float((B_NOPE + B_ROPE) ** -0.5) * LOG2E)
    return pl.pallas_call(
        kern,
        grid=(bsz, s // tq),
        in_specs=[pl.BlockSpec((1, tq, 512), lambda b, i: (b, i, 0)),
                  pl.BlockSpec((1, s, 512), lambda b, i: (b, 0, 0)),
                  pl.BlockSpec((1, s, 256), lambda b, i: (b, 0, 0))],
        out_specs=pl.BlockSpec((1, tq, 256), lambda b, i: (b, i, 0)),
        out_shape=jax.ShapeDtypeStruct((bsz, s, 256), BF16),
        compiler_params=_cparams(2),
        name="mla_attention",
    )(q, k, v)


def _cmp_select_kernel(q_ref, k_ref, v_ref, ov_ref, o_ref, sel_ref, *, tq):
    n = k_ref.shape[1]
    t = pl.program_id(1) * tq + lax.broadcasted_iota(jnp.int32, (tq, n), 0)
    j = lax.broadcasted_iota(jnp.int32, (tq, n), 1)
    visible = (j * C_CMP_STRIDE + (C_CMP_BLOCK - 1)) <= t
    kc, vc = k_ref[0], v_ref[0]
    m_lo, m_hi = _half_masks(BF16)
    lane = lax.broadcasted_iota(jnp.int32, (1, LANES), 1)
    p_sum = [jnp.zeros((tq, n), F32), jnp.zeros((tq, n), F32)]
    for s in range(C_HEADS // C_KV):
        qs = q_ref[0, :, s * LANES:(s + 1) * LANES]
        outs = []
        for kv, hm in enumerate((m_lo, m_hi)):
            sc = jnp.where(visible, _dot_t(qs * hm, kc), NEG_INF)
            m = jnp.max(sc, axis=-1, keepdims=True)
            p = jnp.where(visible, jnp.exp(sc - m), 0.0)
            denom = jnp.sum(p, axis=-1, keepdims=True)
            p = p * jnp.where(denom > 0.0, 1.0 / denom, 0.0)
            p_sum[kv] = p_sum[kv] + p
            outs.append(_dot(p.astype(BF16), vc))
        o_ref[0, :, s * LANES:(s + 1) * LANES] = jnp.where(lane < HEAD, outs[0], outs[1])

    imp = jnp.zeros((tq, LANES), F32)
    for kv in range(C_KV):
        hi = p_sum[kv].astype(BF16)
        lo = (p_sum[kv] - hi.astype(F32)).astype(BF16)
        imp = imp + _dot(hi, ov_ref[kv]) + _dot(lo, ov_ref[kv])
    tl = pl.program_id(1) * tq + lax.broadcasted_iota(jnp.int32, (tq, LANES), 0)
    lane_f = lax.broadcasted_iota(jnp.int32, (tq, LANES), 1)
    blk = lane_f & (HEAD - 1)
    cur = tl >> 6
    forced = (blk == 0) | (blk == cur) | (blk == cur - 1)
    future = blk > cur
    score = jnp.where(future, NEG_INF, jnp.where(forced, FORCE_SCORE, imp))
    sel = jnp.zeros((tq, LANES), F32)
    lane_id = lane_f.astype(F32)
    for kv in range(C_KV):
        in_group = (lane_f >= kv * HEAD) & (lane_f < (kv + 1) * HEAD)
        sc = jnp.where(in_group, score, REMOVED)
        for _ in range(C_N_SEL):
            best = jnp.max(sc, axis=-1, keepdims=True)
            first = jnp.min(jnp.where(sc == best, lane_id, 2.0 * LANES), axis=-1, keepdims=True)
            hit = lane_id == first
            sel = jnp.where(hit, 1.0, sel)
            sc = jnp.where(hit, REMOVED, sc)
    sel_ref[0] = jnp.where(future, 0.0, sel).astype(BF16)


def _cmp_select_call(q, kcmp, vcmp, ov):
    bsz, s, wq = q.shape
    n = kcmp.shape[1]
    tq = min(256, s)
    kern = functools.partial(_cmp_select_kernel, tq=tq)
    return pl.pallas_call(
        kern,
        grid=(bsz, s // tq),
        in_specs=[pl.BlockSpec((1, tq, wq), lambda b, i: (b, i, 0)),
                  pl.BlockSpec((1, n, LANES), lambda b, i: (b, 0, 0)),
                  pl.BlockSpec((1, n, LANES), lambda b, i: (b, 0, 0)),
                  pl.BlockSpec(ov.shape, lambda b, i: (0, 0, 0))],
        out_specs=[pl.BlockSpec((1, tq, wq), lambda b, i: (b, i, 0)),
                   pl.BlockSpec((1, tq, LANES), lambda b, i: (b, i, 0))],
        out_shape=[jax.ShapeDtypeStruct((bsz, s, wq), F32),
                   jax.ShapeDtypeStruct((bsz, s, LANES), BF16)],
        compiler_params=_cparams(2),
        name="nsa_cmp_select",
    )(q, kcmp, vcmp, ov)


def _slc_kernel(q_ref, k_ref, v_ref, sel_ref, e_ref, o_ref, *, tq, tk):
    q0 = pl.program_id(1) * tq
    n_full = q0 // tk
    lane = lax.broadcasted_iota(jnp.int32, (1, LANES), 1)
    m_lo, m_hi = _half_masks(BF16)
    n_slab = C_HEADS // C_KV
    outs = []
    for kv, (hm, other) in enumerate(((m_lo, m_hi), (m_hi, m_lo))):
        sel_h = sel_ref[0] * hm
        qm = [q_ref[0, :, s * LANES:(s + 1) * LANES] * hm for s in range(n_slab)]

        def step(j, states, diagonal):
            rows = pl.ds(pl.multiple_of(j * tk, tk), tk)
            kt = k_ref[0, rows, :]
            vt = v_ref[0, rows, :] * hm + other
            chosen = _dot(sel_h, e_ref[j]) > 0.5
            if diagonal:
                chosen = chosen & _causal_tile(q0, j * tk, tq, tk)
            scs = [_dot_t(qm[s], kt) for s in range(n_slab)]
            return tuple(_online_step(states[s], jnp.where(chosen, scs[s], NEG_INF), vt, LOG2E)
                         for s in range(n_slab))

        states = lax.fori_loop(0, n_full, lambda j, st: step(j, st, False), _softmax_state(n_slab, tq))
        states = step(n_full, states, True)
        outs.append([_normalised(states[s][1], kv) for s in range(n_slab)])
    for s in range(n_slab):
        o_ref[0, :, s * LANES:(s + 1) * LANES] = jnp.where(lane < HEAD, outs[0][s], outs[1][s])


def _slc_call(q, k, v, sel, e):
    bsz, s, wq = q.shape
    tq, tk = FLASH_TQ, e.shape[2]
    kern = functools.partial(_slc_kernel, tq=tq, tk=tk)
    return pl.pallas_call(
        kern,
        grid=(bsz, s // tq),
        in_specs=[pl.BlockSpec((1, tq, wq), lambda b, i: (b, i, 0)),
                  pl.BlockSpec((1, s, LANES), lambda b, i: (b, 0, 0)),
                  pl.BlockSpec((1, s, LANES), lambda b, i: (b, 0, 0)),
                  pl.BlockSpec((1, tq, LANES), lambda b, i: (b, i, 0)),
                  pl.BlockSpec(e.shape, lambda b, i: (0, 0, 0))],
        out_specs=pl.BlockSpec((1, tq, wq), lambda b, i: (b, i, 0)),
        out_shape=jax.ShapeDtypeStruct((bsz, s, wq), F32),
        compiler_params=_cparams(2),
        name="nsa_selected",
    )(q, k, v, sel, e)


def _merge_kernel(h_ref, mod_ref, oa_ref, ob_ref, ocmp_ref, oslc_ref, owin_ref, gf_ref, wg_ref, bg_ref,
                  wo_ref, o_ref, cat_ref):
    wc = ocmp_ref.shape[2]
    gates = jax.nn.sigmoid(_dot(_silu(gf_ref[0]).astype(BF16), wg_ref[...]) + bg_ref[...])
    oc = gates[:, 0:wc] * ocmp_ref[0] + gates[:, wc:2 * wc] * oslc_ref[0] + gates[:, 2 * wc:3 * wc] * owin_ref[0]
    cat_ref[:, 0:256] = oa_ref[0]
    cat_ref[:, 256:512] = ob_ref[0]
    cat_ref[:, 512:512 + wc] = oc.astype(BF16)
    o_ref[0] = h_ref[0] + mod_ref[0, 5:6, :] * _dot(cat_ref[...], wo_ref[...])


def _merge_call(h, mod, oa, ob, ocmp, oslc, owin, gf, wg, bg, wo):
    bsz, s, d = h.shape
    ts = min(512, s)

    def tok(wd):
        return pl.BlockSpec((1, ts, wd), lambda b, i: (b, i, 0))

    def const(shape):
        return pl.BlockSpec(shape, lambda b, i: (0,) * len(shape))

    return pl.pallas_call(
        _merge_kernel,
        grid=(bsz, s // ts),
        in_specs=[tok(d), pl.BlockSpec((1, N_MOD, d), lambda b, i: (b, 0, 0)),
                  tok(256), tok(256), tok(512), tok(512), tok(512), tok(LANES),
                  const(wg.shape), const(bg.shape), const(wo.shape)],
        out_specs=tok(d),
        out_shape=jax.ShapeDtypeStruct(h.shape, F32),
        scratch_shapes=[pltpu.VMEM((ts, wo.shape[0]), BF16)],
        compiler_params=_cparams(2),
        name="mixer_merge",
    )(h, mod, oa, ob, ocmp, oslc, owin, gf, wg, bg, wo)


def _cols(w, pieces):
    out = [jnp.zeros((w.shape[0], p), w.dtype) if isinstance(p, int) else w[:, p[0]:p[1]] for p in pieces]
    return jnp.concatenate(out, axis=1)


def _head_cols(base, order):
    return [(base + hd * HEAD, base + (hd + 1) * HEAD) for hd in order]


A_ORDER = (0, 2, 1, 3)
C_ORDER = (0, 4, 1, 5, 2, 6, 3, 7)


def _proj_weight(w):
    pieces = (_head_cols(0, A_ORDER) + [(256, 384), (384, 512), (512, 768), (768, 896)]
              + [64, (896, 928), 32] + _head_cols(928, C_ORDER)
              + [(1440 + 128 * i, 1568 + 128 * i) for i in range(6)] + [(2208, 2272), 64])
    return _cols(w, pieces).astype(BF16)


def _mla_weights(w_uq, w_ukv):
    dq = B_NOPE + B_ROPE
    uq = []
    for hd in range(B_HEADS):
        uq += [(hd * dq, hd * dq + dq), 32]
    dk = B_NOPE + B_V
    uk = []
    for hd in range(B_HEADS):
        uk += [(hd * dk, hd * dk + B_NOPE), 64]
    uv = [(hd * dk + B_NOPE, (hd + 1) * dk) for hd in range(B_HEADS)]
    return _cols(w_uq, uq).astype(BF16), _cols(w_ukv, uk + uv).astype(BF16)


def _tile_to(v, reps, width=512):
    t = jnp.tile(v, reps)
    return jnp.pad(t, (0, width - t.shape[0]))


def _gain_rows(a_qk_g, b_q_lat_g, b_kv_lat_g, b_qk_nope_g, b_qk_rope_g, c_qk_g):
    z32, z64 = jnp.zeros((32,), F32), jnp.zeros((64,), F32)
    rows = [_tile_to(a_qk_g[0], 4), _tile_to(a_qk_g[1], 2), _tile_to(c_qk_g[0], 8),
            _tile_to(c_qk_g[2], 2), _tile_to(c_qk_g[3], 2),
            jnp.tile(jnp.concatenate([b_qk_nope_g[0], b_qk_rope_g[0], z32]), 4),
            jnp.tile(jnp.concatenate([b_qk_nope_g[1], z64]), 4),
            _tile_to(jnp.concatenate([z64, b_qk_rope_g[1], z32]), 1),
            _tile_to(b_q_lat_g, 1), _tile_to(b_kv_lat_g, 1)]
    rows += [jnp.zeros((512,), F32)] * (16 - len(rows))
    return jnp.stack(rows)


def _block_diag_consts():
    i = np.arange(256)
    bd64 = np.where((i[:, None] // 64) == (i[None, :] // 64), 1.0 / 64, 0.0)
    j = i % 128
    grp = np.where(j < 64, 0, np.where(j < 96, 1, 2)) + 3 * (i // 128)
    size = np.where(j < 64, 64.0, 32.0)
    bdq = np.where(grp[:, None] == grp[None, :], 1.0 / size[None, :], 0.0)
    return jnp.asarray(bd64, BF16), jnp.asarray(bdq, BF16)


def _compress_weights(pe, w1, w2):
    n_tok = C_CMP_STRIDE
    pe_rows = jnp.broadcast_to(pe.reshape(2, 2, n_tok, 1, HEAD), (2, 2, n_tok, C_KV, HEAD))
    pe_rows = pe_rows.reshape(2, 2, n_tok * C_KV * HEAD)
    w1r = w1.reshape(2, 2, n_tok, HEAD, C_CMP_HID)
    eye = jnp.eye(C_KV, dtype=F32)
    w1big = jnp.einsum('xytdj,kq->xytkdqj', w1r, eye).reshape(2, 2, n_tok * C_KV * HEAD, C_KV * C_CMP_HID)
    w2big = jnp.einsum('xjd,kq->xkjqd', w2, eye).reshape(2, C_KV * C_CMP_HID, C_KV * HEAD)
    return pe_rows, w1big.astype(BF16), w2big.astype(BF16)


def _overlap_const(n_chunk):
    n_slc = n_chunk * C_CMP_STRIDE // C_SLC_BLOCK
    ci = np.arange(n_chunk)[:, None] * C_CMP_STRIDE
    sj = np.arange(n_slc)[None, :] * C_SLC_BLOCK
    ov = ((ci <= sj + C_SLC_BLOCK - 1) & (ci + C_CMP_BLOCK - 1 >= sj)).astype(np.float32)
    ov[n_chunk - 1] = 0.0
    out = np.zeros((C_KV, n_chunk, LANES), np.float32)
    for kv in range(C_KV):
        out[kv, :, kv * HEAD:kv * HEAD + n_slc] = ov
    return jnp.asarray(out, BF16)


def _expand_const(s, tk):
    r = np.arange(LANES)[None, :, None] % HEAD
    c = (np.arange(s // tk)[:, None, None] * tk + np.arange(tk)[None, None, :]) // C_SLC_BLOCK
    return jnp.asarray((r == c).astype(np.float32), BF16)


def _gate_weights(c_gate_w, c_gate_b):
    head_of_lane = np.repeat(np.asarray(C_ORDER), HEAD)
    cols = np.concatenate([head_of_lane * 3 + k for k in range(3)])
    wg = jnp.concatenate([c_gate_w[:, cols], jnp.zeros((LANES - C_GATE_HID, cols.size), F32)], axis=0)
    return wg.astype(BF16), c_gate_b[cols].reshape(1, -1)


def _out_weight(w_out):
    rows = ([(hd * HEAD, (hd + 1) * HEAD) for hd in A_ORDER] + [(256, 512)]
            + [(512 + hd * HEAD, 512 + (hd + 1) * HEAD) for hd in C_ORDER])
    return jnp.concatenate([w_out[a:b] for a, b in rows], axis=0).astype(BF16)


def kernel(x, c, positions, ada_w, ada_b, norm_g, ffn_w_in, ffn_w_out, w_in, w_out, a_sinks, a_qk_g,
           b_q_lat_g, b_kv_lat_g, b_w_uq, b_w_ukv, b_qk_nope_g, b_qk_rope_g, c_qk_g, c_cmp_pe, c_cmp_w1,
           c_cmp_w2, c_gate_w, c_gate_b):
    bsz, s, d = x.shape
    depth = ada_w.shape[0]
    n_chunk = s // C_CMP_STRIDE
    tq = min(256, s)
    assert s % tq == 0 and s % FLASH_TK == 0 and s >= C_WINDOW + tq and s // C_SLC_BLOCK <= HEAD

    mod_all = _ada_call(c, ada_w, ada_b).reshape(depth, bsz, N_MOD, d)
    tab = _rope_call(positions)
    bd64, bdq = _block_diag_consts()
    ov = _overlap_const(n_chunk)
    e = _expand_const(s, FLASH_TK)

    h = x
    for l in range(depth):
        mod = mod_all[l]
        h = _ffn_call(h, mod, norm_g[l, 0:1], ffn_w_in[l, 0].astype(BF16), ffn_w_out[l, 0].astype(BF16), 0)

        wuq, wukv = _mla_weights(b_w_uq[l], b_w_ukv[l])
        gv = _gain_rows(a_qk_g[l], b_q_lat_g[l], b_kv_lat_g[l], b_qk_nope_g[l], b_qk_rope_g[l], c_qk_g[l])
        (qa, ka, va, qb, kb, vb, qc, qcr, kc, vc, ks, vs, kw, vw, gf) = _prep_call(
            h, mod, norm_g[l, 1:2], _proj_weight(w_in[l]), wuq, wukv, gv, bd64, bdq, tab)

        pe_rows, w1big, w2big = _compress_weights(c_cmp_pe[l], c_cmp_w1[l], c_cmp_w2[l])
        kcmp, vcmp = _compress_call(kc.reshape(bsz, n_chunk, C_CMP_STRIDE * LANES),
                                    vc.reshape(bsz, n_chunk, C_CMP_STRIDE * LANES),
                                    pe_rows, w1big, w2big, bd64, jnp.tile(c_qk_g[l, 1], 2).reshape(1, LANES))

        oa = _window_call(qa, ka, va, a_sinks[l], A_WINDOW, BF16, "swa_attention")
        ob = _mla_call(qb, kb, vb)
        ocmp, sel = _cmp_select_call(qc, kcmp, vcmp, ov)
        oslc = _slc_call(qcr, ks, vs, sel, e)
        owin = _window_call(qcr, kw, vw, None, C_WINDOW, F32, "nsa_window")

        wg, bg = _gate_weights(c_gate_w[l], c_gate_b[l])
        h = _merge_call(h, mod, oa, ob, ocmp, oslc, owin, gf, wg, bg, _out_weight(w_out[l]))

        h = _ffn_call(h, mod, norm_g[l, 2:3], ffn_w_in[l, 1].astype(BF16), ffn_w_out[l, 1].astype(BF16), 6)
    return h
```

```python
import functools

import numpy as np
import jax
import jax.numpy as jnp
from jax import lax
from jax.experimental import pallas as pl
from jax.experimental.pallas import tpu as pltpu

F32 = jnp.float32
BF16 = jnp.bfloat16

LANES = 128
HEAD = 64
EPS = 1e-6
ROPE_THETA = 10000.0
NEG_INF = -1e30
FORCE_SCORE = 1e4
REMOVED = -3e38
N_MOD = 9

A_HEADS, A_KV, A_WINDOW = 4, 2, 128
B_HEADS, B_Q_LORA, B_KV_LORA, B_NOPE, B_ROPE, B_V = 4, 256, 128, 64, 32, 64
C_HEADS, C_KV, C_WINDOW = 8, 2, 256
C_CMP_BLOCK, C_CMP_STRIDE, C_SLC_BLOCK, C_N_SEL = 32, 16, 64, 8
C_CMP_HID, C_GATE_HID = 128, 64

VMEM_LIMIT = 56 * 1024 * 1024

O_AQ, O_AK, O_AV, O_BCQ, O_BCKV, O_KR, O_CQ = 0, 256, 384, 512, 768, 896, 1024
O_KC, O_VC, O_KS, O_VS, O_KW, O_VW, O_GF, W_PROJ = 1536, 1664, 1792, 1920, 2048, 2176, 2304, 2432


def _cparams(n_grid):
    return pltpu.CompilerParams(dimension_semantics=("arbitrary",) * n_grid,
                                vmem_limit_bytes=VMEM_LIMIT)


def _dot(a, b):
    return jnp.dot(a, b, preferred_element_type=F32)


def _dot_t(a, b):
    return lax.dot_general(a, b, (((1,), (1,)), ((), ())), preferred_element_type=F32)


def _silu(x):
    return x * jax.nn.sigmoid(x)


def _half_masks(dtype):
    lane = lax.broadcasted_iota(jnp.int32, (1, LANES), 1)
    lo = jnp.where(lane < HEAD, 1.0, 0.0).astype(dtype)
    hi = jnp.where(lane >= HEAD, 1.0, 0.0).astype(dtype)
    return lo, hi


def _ada_kernel(c_ref, w_ref, b_ref, o_ref):
    cond = _silu(c_ref[...]).astype(BF16)
    o_ref[0] = _dot(cond, w_ref[0].astype(BF16)) + b_ref[0]


def _ada_call(c, ada_w, ada_b):
    depth, d, nm = ada_w.shape
    bsz = c.shape[0]
    tn = 1024
    return pl.pallas_call(
        _ada_kernel,
        grid=(depth, nm // tn),
        in_specs=[pl.BlockSpec((bsz, d), lambda l, j: (0, 0)),
                  pl.BlockSpec((1, d, tn), lambda l, j: (l, 0, j)),
                  pl.BlockSpec((1, 1, tn), lambda l, j: (l, 0, j))],
        out_specs=pl.BlockSpec((1, bsz, tn), lambda l, j: (l, 0, j)),
        out_shape=jax.ShapeDtypeStruct((depth, bsz, nm), F32),
        compiler_params=_cparams(2),
        name="adaln",
    )(c, ada_w, ada_b.reshape(depth, 1, nm))


def _rope_kernel(pos_ref, f_ref, o_ref):
    pos = pos_ref[0]
    lane = lax.broadcasted_iota(jnp.int32, (1, LANES), 1)
    ang64 = pos * f_ref[0:1, :]
    sign64 = jnp.where((lane & 63) < 32, -1.0, 1.0)
    o_ref[0, :, 0:128] = jnp.cos(ang64)
    o_ref[0, :, 128:256] = jnp.sin(ang64) * sign64
    ang32 = pos * f_ref[1:2, :]
    sign32 = jnp.where(lane < 80, -1.0, 1.0)
    o_ref[0, :, 256:384] = jnp.cos(ang32)
    o_ref[0, :, 384:512] = jnp.sin(ang32) * sign32


def _rope_call(positions):
    bsz, s = positions.shape
    ts = min(512, s)
    inv64 = 1.0 / (ROPE_THETA ** (jnp.arange(0, HEAD, 2, dtype=F32) / HEAD))
    inv32 = 1.0 / (ROPE_THETA ** (jnp.arange(0, B_ROPE, 2, dtype=F32) / B_ROPE))
    z = jnp.zeros((32,), F32)
    f = jnp.stack([jnp.tile(inv64, 4),
                   jnp.concatenate([z, z, inv32, inv32, z])])
    pos = positions.astype(F32).reshape(bsz, s, 1)
    return pl.pallas_call(
        _rope_kernel,
        grid=(bsz, s // ts),
        in_specs=[pl.BlockSpec((1, ts, 1), lambda b, i: (b, i, 0)),
                  pl.BlockSpec((2, LANES), lambda b, i: (0, 0))],
        out_specs=pl.BlockSpec((1, ts, 512), lambda b, i: (b, i, 0)),
        out_shape=jax.ShapeDtypeStruct((bsz, s, 512), F32),
        compiler_params=_cparams(2),
        name="rope_tables",
    )(pos, f)


def _modulated_norm(h, g, scale, shift):
    y = h * lax.rsqrt(jnp.mean(h * h, axis=-1, keepdims=True) + EPS)
    return (y * g) * (1.0 + scale) + shift


def _ffn_kernel(h_ref, mod_ref, g_ref, wi_ref, wo_ref, o_ref, act_ref, *, mod_row, d_ff, chunk):
    h = h_ref[0]
    u = _modulated_norm(h, g_ref[...], mod_ref[0, mod_row + 1:mod_row + 2, :],
                        mod_ref[0, mod_row:mod_row + 1, :]).astype(BF16)
    for c in range(0, d_ff, chunk):
        gate = _dot(u, wi_ref[:, c:c + chunk])
        up = _dot(u, wi_ref[:, d_ff + c:d_ff + c + chunk])
        act_ref[:, c:c + chunk] = (_silu(gate) * up).astype(BF16)
    y = _dot(act_ref[...], wo_ref[...])
    o_ref[0] = h + (0.5 * mod_ref[0, mod_row + 2:mod_row + 3, :]) * y


def _ffn_call(h, mod, g, w_in, w_out, mod_row):
    bsz, s, d = h.shape
    d_ff = w_out.shape[0]
    ts = min(512, s)
    chunk = 256
    kern = functools.partial(_ffn_kernel, mod_row=mod_row, d_ff=d_ff, chunk=chunk)
    return pl.pallas_call(
        kern,
        grid=(bsz, s // ts),
        in_specs=[pl.BlockSpec((1, ts, d), lambda b, i: (b, i, 0)),
                  pl.BlockSpec((1, N_MOD, d), lambda b, i: (b, 0, 0)),
                  pl.BlockSpec((1, d), lambda b, i: (0, 0)),
                  pl.BlockSpec((d, 2 * d_ff), lambda b, i: (0, 0), pipeline_mode=pl.Buffered(1)),
                  pl.BlockSpec((d_ff, d), lambda b, i: (0, 0), pipeline_mode=pl.Buffered(1))],
        out_specs=pl.BlockSpec((1, ts, d), lambda b, i: (b, i, 0)),
        out_shape=jax.ShapeDtypeStruct(h.shape, F32),
        scratch_shapes=[pltpu.VMEM((ts, d_ff), BF16)],
        compiler_params=_cparams(2),
        name="ffn",
    )(h, mod, g, w_in, w_out)


def _group_mean_sq(x, bd_ref):
    w = x.shape[-1]
    sq = x * x
    hi = sq.astype(BF16)
    lo = (sq - hi.astype(F32)).astype(BF16)
    pieces = []
    step = 256 if w % 256 == 0 else 128
    for c in range(0, w, step):
        bd = bd_ref[0:step, 0:step]
        pieces.append(_dot(hi[:, c:c + step], bd) + _dot(lo[:, c:c + step], bd))
    return pieces[0] if len(pieces) == 1 else jnp.concatenate(pieces, axis=-1)


def _group_norm(x, bd_ref, gain):
    return x * lax.rsqrt(_group_mean_sq(x, bd_ref) + EPS) * gain


def _rope_slab(x, cos, sin, first_half, fwd, bwd):
    rot = jnp.where(first_half, pltpu.roll(x, fwd, axis=1), pltpu.roll(x, bwd, axis=1))
    return x * cos + rot * sin


def _rope64(x, cos, sin):
    lane = lax.broadcasted_iota(jnp.int32, (1, LANES), 1)
    first = (lane & 63) < 32
    out = [_rope_slab(x[:, c:c + LANES], cos, sin, first, 96, 32) for c in range(0, x.shape[-1], LANES)]
    return out[0] if len(out) == 1 else jnp.concatenate(out, axis=-1)


def _rope32(x, cos, sin):
    lane = lax.broadcasted_iota(jnp.int32, (1, LANES), 1)
    first = lane < 80
    out = [_rope_slab(x[:, c:c + LANES], cos, sin, first, 112, 16) for c in range(0, x.shape[-1], LANES)]
    return out[0] if len(out) == 1 else jnp.concatenate(out, axis=-1)


def _with_ones(v):
    lane = lax.broadcasted_iota(jnp.int32, (1, LANES), 1)
    out = []
    for c in range(0, v.shape[-1], LANES):
        slab = v[:, c:c + LANES]
        out += [jnp.where(lane < HEAD, slab, 1.0), jnp.where(lane < HEAD, 1.0, slab)]
    return jnp.concatenate(out, axis=-1).astype(BF16)


def _prep_kernel(h_ref, mod_ref, g_ref, w_ref, wuq_ref, wukv_ref, gv_ref, bd64_ref, bdq_ref, tab_ref,
                 qa_ref, ka_ref, va_ref, qb_ref, kb_ref, vb_ref, qc_ref, qcr_ref,
                 kc_ref, vc_ref, ks_ref, vs_ref, kw_ref, vw_ref, gf_ref):
    u = _modulated_norm(h_ref[0], g_ref[...], mod_ref[0, 4:5, :], mod_ref[0, 3:4, :]).astype(BF16)
    cos64, sin64 = tab_ref[0, :, 0:128], tab_ref[0, :, 128:256]
    cos32, sin32 = tab_ref[0, :, 256:384], tab_ref[0, :, 384:512]

    def proj(a, b):
        return _dot(u, w_ref[:, a:b])

    qa = _rope64(_group_norm(proj(O_AQ, O_AK), bd64_ref, gv_ref[0:1, 0:256]), cos64, sin64)
    qa_ref[0] = (qa * 0.125).astype(BF16)
    ka = _rope64(_group_norm(proj(O_AK, O_AV), bd64_ref, gv_ref[1:2, 0:128]), cos64, sin64)
    ka_ref[0] = ka.astype(BF16)
    va_ref[0] = _with_ones(proj(O_AV, O_BCQ))

    cq = proj(O_BCQ, O_BCKV)
    cq = cq * lax.rsqrt(jnp.mean(cq * cq, axis=-1, keepdims=True) + EPS) * gv_ref[8:9, 0:256]
    qb = _group_norm(_dot(cq.astype(BF16), wuq_ref[...]), bdq_ref, gv_ref[5:6, :])
    qb_ref[0] = _rope32(qb, cos32, sin32).astype(BF16)
    ckv = proj(O_BCKV, O_KR)
    ckv = (ckv * lax.rsqrt(jnp.mean(ckv * ckv, axis=-1, keepdims=True) + EPS) * gv_ref[9:10, 0:128]).astype(BF16)
    k_nope = _group_norm(_dot(ckv, wukv_ref[:, 0:512]), bd64_ref, gv_ref[6:7, :])
    vb_ref[0] = _with_ones(_dot(ckv, wukv_ref[:, 512:768]))
    k_pe = _rope32(_group_norm(proj(O_KR, O_CQ), bdq_ref, gv_ref[7:8, 0:128]), cos32, sin32)
    for hh in range(B_HEADS):
        kb_ref[0, :, hh * LANES:(hh + 1) * LANES] = (k_nope[:, hh * LANES:(hh + 1) * LANES] + k_pe).astype(BF16)

    qc = _group_norm(proj(O_CQ, O_KC), bd64_ref, gv_ref[2:3, :]) * 0.125
    qc_ref[0] = qc.astype(BF16)
    qcr_ref[0] = _rope64(qc, cos64, sin64).astype(BF16)
    kc_ref[0] = proj(O_KC, O_VC)
    vc_ref[0] = proj(O_VC, O_KS)
    ks = _rope64(_group_norm(proj(O_KS, O_VS), bd64_ref, gv_ref[3:4, 0:128]), cos64, sin64)
    ks_ref[0] = ks.astype(BF16)
    vs_ref[0] = _with_ones(proj(O_VS, O_KW))
    kw = _rope64(_group_norm(proj(O_KW, O_VW), bd64_ref, gv_ref[4:5, 0:128]), cos64, sin64)
    kw_ref[0] = kw.astype(BF16)
    vw_ref[0] = _with_ones(proj(O_VW, O_GF))
    gf_ref[0] = proj(O_GF, W_PROJ)


def _prep_call(h, mod, g, w, wuq, wukv, gv, bd64, bdq, tab):
    bsz, s, d = h.shape
    ts = min(512, s)
    widths = [(256, BF16), (128, BF16), (256, BF16), (512, BF16), (512, BF16), (512, BF16),
              (512, BF16), (512, BF16), (128, F32), (128, F32), (128, BF16), (256, BF16),
              (128, BF16), (256, BF16), (128, F32)]

    def const(shape):
        return pl.BlockSpec(shape, lambda b, i: (0,) * len(shape))

    return pl.pallas_call(
        _prep_kernel,
        grid=(bsz, s // ts),
        in_specs=[pl.BlockSpec((1, ts, d), lambda b, i: (b, i, 0)),
                  pl.BlockSpec((1, N_MOD, d), lambda b, i: (b, 0, 0)),
                  const((1, d)), const(w.shape), const(wuq.shape), const(wukv.shape),
                  const(gv.shape), const(bd64.shape), const(bdq.shape),
                  pl.BlockSpec((1, ts, 512), lambda b, i: (b, i, 0))],
        out_specs=[pl.BlockSpec((1, ts, wd), lambda b, i: (b, i, 0)) for wd, _ in widths],
        out_shape=[jax.ShapeDtypeStruct((bsz, s, wd), dt) for wd, dt in widths],
        compiler_params=_cparams(2),
        name="mixer_proj",
    )(h, mod, g, w, wuq, wukv, gv, bd64, bdq, tab)


def _compress_kernel(kc_ref, vc_ref, pe_ref, w1_ref, w2_ref, bd64_ref, g_ref, ko_ref, vo_ref):
    n = kc_ref.shape[1]

    def one(z_ref, idx):
        z = z_ref[0]
        top = _dot((z + pe_ref[idx, 0:1, :]).astype(BF16), w1_ref[idx, 0])
        bot = _dot((z + pe_ref[idx, 1:2, :]).astype(BF16), w1_ref[idx, 1])
        pre = top + pltpu.roll(bot, n - 1, axis=0)
        return _dot(_silu(pre).astype(BF16), w2_ref[idx])

    ko_ref[0] = _group_norm(one(kc_ref, 0), bd64_ref, g_ref[...]).astype(BF16)
    vo_ref[0] = one(vc_ref, 1).astype(BF16)


def _compress_call(kc, vc, pe, w1, w2, bd64, g):
    bsz, n, wz = kc.shape

    def const(shape):
        return pl.BlockSpec(shape, lambda b: (0,) * len(shape))

    return pl.pallas_call(
        _compress_kernel,
        grid=(bsz,),
        in_specs=[pl.BlockSpec((1, n, wz), lambda b: (b, 0, 0)),
                  pl.BlockSpec((1, n, wz), lambda b: (b, 0, 0)),
                  const(pe.shape), const(w1.shape), const(w2.shape), const(bd64.shape), const(g.shape)],
        out_specs=[pl.BlockSpec((1, n, LANES), lambda b: (b, 0, 0))] * 2,
        out_shape=[jax.ShapeDtypeStruct((bsz, n, LANES), BF16)] * 2,
        compiler_params=_cparams(1),
        name="nsa_compress",
    )(kc, vc, pe, w1, w2, bd64, g)


def _window_kernel(*refs, window, tq, n_slab, n_group, use_sink):
    if use_sink:
        sink_ref, q_ref, k_ref, v_ref, o_ref = refs
    else:
        q_ref, k_ref, v_ref, o_ref = refs
    span = window + tq
    q0 = pl.program_id(1) * tq
    start = pl.multiple_of(jnp.maximum(q0 - window, 0), LANES)
    kwin = k_ref[0, pl.ds(start, span), :]
    rel = (q0 + lax.broadcasted_iota(jnp.int32, (tq, span), 0)) - \
          (start + lax.broadcasted_iota(jnp.int32, (tq, span), 1))
    visible = (rel >= 0) & (rel < window)
    half_masks = _half_masks(BF16)
    lane = lax.broadcasted_iota(jnp.int32, (1, LANES), 1)
    scs = [[_dot_t(q_ref[0, :, s * LANES:(s + 1) * LANES] * hm, kwin) for hm in half_masks] for s in range(n_slab)]
    for s in range(n_slab):
        outs = []
        for kv in range(2):
            sc = jnp.where(visible, scs[s][kv], NEG_INF)
            m = jnp.max(sc, axis=-1, keepdims=True)
            if use_sink:
                sink = sink_ref[kv * n_group + s]
                m = jnp.maximum(m, sink)
            p = jnp.exp2((sc - m) * LOG2E)
            acc = _dot(p.astype(BF16), v_ref[0, pl.ds(start, span), kv * LANES:(kv + 1) * LANES])
            denom = jnp.sum(jnp.where(lane == (1 - kv) * HEAD, acc, 0.0), axis=-1, keepdims=True)
            if use_sink:
                denom = denom + jnp.exp2((sink - m) * LOG2E)
            outs.append(acc / denom)
        o_ref[0, :, s * LANES:(s + 1) * LANES] = jnp.where(lane < HEAD, outs[0], outs[1]).astype(o_ref.dtype)


def _window_call(q, k, v, sinks, window, out_dtype, name):
    bsz, s, wq = q.shape
    tq = WINDOW_TQ
    n_slab = wq // LANES
    use_sink = sinks is not None
    kern = functools.partial(_window_kernel, window=window, tq=tq, n_slab=n_slab, n_group=n_slab,
                             use_sink=use_sink)
    in_specs = [pl.BlockSpec((1, tq, wq), lambda b, i: (b, i, 0)),
                pl.BlockSpec((1, s, LANES), lambda b, i: (b, 0, 0)),
                pl.BlockSpec((1, s, 2 * LANES), lambda b, i: (b, 0, 0))]
    args = [q, k, v]
    if use_sink:
        in_specs = [pl.BlockSpec(memory_space=pltpu.SMEM)] + in_specs
        args = [sinks] + args
    return pl.pallas_call(
        kern,
        grid=(bsz, s // tq),
        in_specs=in_specs,
        out_specs=pl.BlockSpec((1, tq, wq), lambda b, i: (b, i, 0)),
        out_shape=jax.ShapeDtypeStruct((bsz, s, wq), out_dtype),
        compiler_params=_cparams(2),
        name=name,
    )(*args)


LOG2E = 1.4426950408889634


FLASH_TQ, FLASH_TK = 256, 512
WINDOW_TQ = 128


def _softmax_state(n_heads, tq):
    return ((jnp.full((tq, 1), NEG_INF, F32), jnp.zeros((tq, LANES), F32)),) * n_heads


def _online_step(state, sc, vt, c):
    m_prev, acc = state
    m_new = jnp.maximum(m_prev, jnp.max(sc, axis=-1, keepdims=True))
    alpha = jnp.exp2((m_prev - m_new) * c)
    p = jnp.exp2((sc - m_new) * c)
    return m_new, alpha * acc + _dot(p.astype(BF16), vt)


def _normalised(acc, value_half):
    lane = lax.broadcasted_iota(jnp.int32, (1, LANES), 1)
    denom_lane = (1 - value_half) * HEAD
    denom = jnp.sum(jnp.where(lane == denom_lane, acc, 0.0), axis=-1, keepdims=True)
    return acc / denom


def _causal_tile(q0, k0, tq, tk):
    return (k0 + lax.broadcasted_iota(jnp.int32, (tq, tk), 1)) <= (q0 + lax.broadcasted_iota(jnp.int32, (tq, tk), 0))


def _mla_kernel(q_ref, k_ref, v_ref, o_ref, *, tq, tk, c):
    q0 = pl.program_id(1) * tq
    n_full = q0 // tk
    lane = lax.broadcasted_iota(jnp.int32, (1, LANES), 1)

    def step(j, states, diagonal):
        rows = pl.ds(pl.multiple_of(j * tk, tk), tk)
        if diagonal:
            causal = _causal_tile(q0, j * tk, tq, tk)
        scs = [_dot_t(q_ref[0, :, hh * LANES:(hh + 1) * LANES], k_ref[0, rows, hh * LANES:(hh + 1) * LANES])
               for hh in range(B_HEADS)]
        out = []
        for hh in range(B_HEADS):
            sc = jnp.where(causal, scs[hh], NEG_INF) if diagonal else scs[hh]
            out.append(_online_step(states[hh], sc, v_ref[0, rows, hh * LANES:(hh + 1) * LANES], c))
        return tuple(out)

    states = lax.fori_loop(0, n_full, lambda j, st: step(j, st, False), _softmax_state(B_HEADS, tq))
    states = step(n_full, states, True)
    for s in range(B_HEADS // 2):
        o_lo = _normalised(states[2 * s][1], 0)
        o_hi = _normalised(states[2 * s + 1][1], 1)
        o_ref[0, :, s * LANES:(s + 1) * LANES] = jnp.where(lane < HEAD, o_lo, o_hi).astype(o_ref.dtype)


def _mla_call(q, k, v):
    bsz, s, _ = q.shape
    tq, tk = FLASH_TQ, FLASH_TK
    kern = functools.partial(_mla_kernel, tq=tq, tk=tk, c=float((B_NOPE + B_ROPE) ** -0.5) * LOG2E)
    return pl.pallas_call(
        kern,
        grid=(bsz, s // tq),
        in_specs=[pl.BlockSpec((1, tq, 512), lambda b, i: (b, i, 0)),
                  pl.BlockSpec((1, s, 512), lambda b, i: (b, 0, 0)),
                  pl.BlockSpec((1, s, 512), lambda b, i: (b, 0, 0))],
        out_specs=pl.BlockSpec((1, tq, 256), lambda b, i: (b, i, 0)),
        out_shape=jax.ShapeDtypeStruct((bsz, s, 256), BF16),
        compiler_params=_cparams(2),
        name="mla_attention",
    )(q, k, v)


def _cmp_select_kernel(q_ref, k_ref, v_ref, ov_ref, o_ref, sel_ref, *, tq):
    n = k_ref.shape[1]
    t = pl.program_id(1) * tq + lax.broadcasted_iota(jnp.int32, (tq, n), 0)
    j = lax.broadcasted_iota(jnp.int32, (tq, n), 1)
    visible = (j * C_CMP_STRIDE + (C_CMP_BLOCK - 1)) <= t
    kc, vc = k_ref[0], v_ref[0]
    m_lo, m_hi = _half_masks(BF16)
    lane = lax.broadcasted_iota(jnp.int32, (1, LANES), 1)
    p_sum = [jnp.zeros((tq, n), F32), jnp.zeros((tq, n), F32)]
    scs = [[_dot_t(q_ref[0, :, s * LANES:(s + 1) * LANES] * hm, kc) for hm in (m_lo, m_hi)]
           for s in range(C_HEADS // C_KV)]
    for s in range(C_HEADS // C_KV):
        outs = []
        for kv in range(C_KV):
            sc = jnp.where(visible, scs[s][kv], NEG_INF)
            m = jnp.max(sc, axis=-1, keepdims=True)
            p = jnp.where(visible, jnp.exp(sc - m), 0.0)
            denom = jnp.sum(p, axis=-1, keepdims=True)
            p = p * jnp.where(denom > 0.0, 1.0 / denom, 0.0)
            p_sum[kv] = p_sum[kv] + p
            outs.append(_dot(p.astype(BF16), vc))
        o_ref[0, :, s * LANES:(s + 1) * LANES] = jnp.where(lane < HEAD, outs[0], outs[1])

    imp = jnp.zeros((tq, LANES), F32)
    for kv in range(C_KV):
        hi = p_sum[kv].astype(BF16)
        lo = (p_sum[kv] - hi.astype(F32)).astype(BF16)
        imp = imp + _dot(hi, ov_ref[kv]) + _dot(lo, ov_ref[kv])
    tl = pl.program_id(1) * tq + lax.broadcasted_iota(jnp.int32, (tq, LANES), 0)
    lane_f = lax.broadcasted_iota(jnp.int32, (tq, LANES), 1)
    blk = lane_f & (HEAD - 1)
    cur = tl >> 6
    forced = (blk == 0) | (blk == cur) | (blk == cur - 1)
    future = blk > cur
    score = jnp.where(future, NEG_INF, jnp.where(forced, FORCE_SCORE, imp))
    sel = jnp.zeros((tq, LANES), F32)
    lane_id = lane_f.astype(F32)
    for kv in range(C_KV):
        in_group = (lane_f >= kv * HEAD) & (lane_f < (kv + 1) * HEAD)
        sc = jnp.where(in_group, score, REMOVED)
        for _ in range(C_N_SEL):
            best = jnp.max(sc, axis=-1, keepdims=True)
            first = jnp.min(jnp.where(sc == best, lane_id, 2.0 * LANES), axis=-1, keepdims=True)
            hit = lane_id == first
            sel = jnp.where(hit, 1.0, sel)
            sc = jnp.where(hit, REMOVED, sc)
    sel_ref[0] = jnp.where(future, 0.0, sel).astype(BF16)


def _cmp_select_call(q, kcmp, vcmp, ov):
    bsz, s, wq = q.shape
    n = kcmp.shape[1]
    tq = min(256, s)
    kern = functools.partial(_cmp_select_kernel, tq=tq)
    return pl.pallas_call(
        kern,
        grid=(bsz, s // tq),
        in_specs=[pl.BlockSpec((1, tq, wq), lambda b, i: (b, i, 0)),
                  pl.BlockSpec((1, n, LANES), lambda b, i: (b, 0, 0)),
                  pl.BlockSpec((1, n, LANES), lambda b, i: (b, 0, 0)),
                  pl.BlockSpec(ov.shape, lambda b, i: (0, 0, 0))],
        out_specs=[pl.BlockSpec((1, tq, wq), lambda b, i: (b, i, 0)),
                   pl.BlockSpec((1, tq, LANES), lambda b, i: (b, i, 0))],
        out_shape=[jax.ShapeDtypeStruct((bsz, s, wq), F32),
                   jax.ShapeDtypeStruct((bsz, s, LANES), BF16)],
        compiler_params=_cparams(2),
        name="nsa_cmp_select",
    )(q, kcmp, vcmp, ov)


def _slc_kernel(q_ref, k_ref, v_ref, sel_ref, e_ref, o_ref, *, tq, tk):
    q0 = pl.program_id(1) * tq
    n_full = q0 // tk
    lane = lax.broadcasted_iota(jnp.int32, (1, LANES), 1)
    m_lo, m_hi = _half_masks(BF16)
    n_slab = C_HEADS // C_KV
    outs = []
    for kv, hm in enumerate((m_lo, m_hi)):
        sel_h = sel_ref[0] * hm
        qm = [q_ref[0, :, s * LANES:(s + 1) * LANES] * hm for s in range(n_slab)]

        def step(j, states, diagonal):
            rows = pl.ds(pl.multiple_of(j * tk, tk), tk)
            kt = k_ref[0, rows, :]
            vt = v_ref[0, rows, kv * LANES:(kv + 1) * LANES]
            chosen = _dot(sel_h, e_ref[j]) > 0.5
            if diagonal:
                chosen = chosen & _causal_tile(q0, j * tk, tq, tk)
            scs = [_dot_t(qm[s], kt) for s in range(n_slab)]
            return tuple(_online_step(states[s], jnp.where(chosen, scs[s], NEG_INF), vt, LOG2E)
                         for s in range(n_slab))

        states = lax.fori_loop(0, n_full, lambda j, st: step(j, st, False), _softmax_state(n_slab, tq))
        states = step(n_full, states, True)
        outs.append([_normalised(states[s][1], kv) for s in range(n_slab)])
    for s in range(n_slab):
        o_ref[0, :, s * LANES:(s + 1) * LANES] = jnp.where(lane < HEAD, outs[0][s], outs[1][s])


def _slc_call(q, k, v, sel, e):
    bsz, s, wq = q.shape
    tq, tk = FLASH_TQ, e.shape[2]
    kern = functools.partial(_slc_kernel, tq=tq, tk=tk)
    return pl.pallas_call(
        kern,
        grid=(bsz, s // tq),
        in_specs=[pl.BlockSpec((1, tq, wq), lambda b, i: (b, i, 0)),
                  pl.BlockSpec((1, s, LANES), lambda b, i: (b, 0, 0)),
                  pl.BlockSpec((1, s, 2 * LANES), lambda b, i: (b, 0, 0)),
                  pl.BlockSpec((1, tq, LANES), lambda b, i: (b, i, 0)),
                  pl.BlockSpec(e.shape, lambda b, i: (0, 0, 0))],
        out_specs=pl.BlockSpec((1, tq, wq), lambda b, i: (b, i, 0)),
        out_shape=jax.ShapeDtypeStruct((bsz, s, wq), F32),
        compiler_params=_cparams(2),
        name="nsa_selected",
    )(q, k, v, sel, e)


def _merge_kernel(h_ref, mod_ref, oa_ref, ob_ref, ocmp_ref, oslc_ref, owin_ref, gf_ref, wg_ref, bg_ref,
                  wo_ref, o_ref, cat_ref):
    wc = ocmp_ref.shape[2]
    gates = jax.nn.sigmoid(_dot(_silu(gf_ref[0]).astype(BF16), wg_ref[...]) + bg_ref[...])
    oc = gates[:, 0:wc] * ocmp_ref[0] + gates[:, wc:2 * wc] * oslc_ref[0] + gates[:, 2 * wc:3 * wc] * owin_ref[0]
    cat_ref[:, 0:256] = oa_ref[0]
    cat_ref[:, 256:512] = ob_ref[0]
    cat_ref[:, 512:512 + wc] = oc.astype(BF16)
    o_ref[0] = h_ref[0] + mod_ref[0, 5:6, :] * _dot(cat_ref[...], wo_ref[...])


def _merge_call(h, mod, oa, ob, ocmp, oslc, owin, gf, wg, bg, wo):
    bsz, s, d = h.shape
    ts = min(512, s)

    def tok(wd):
        return pl.BlockSpec((1, ts, wd), lambda b, i: (b, i, 0))

    def const(shape):
        return pl.BlockSpec(shape, lambda b, i: (0,) * len(shape))

    return pl.pallas_call(
        _merge_kernel,
        grid=(bsz, s // ts),
        in_specs=[tok(d), pl.BlockSpec((1, N_MOD, d), lambda b, i: (b, 0, 0)),
                  tok(256), tok(256), tok(512), tok(512), tok(512), tok(LANES),
                  const(wg.shape), const(bg.shape), const(wo.shape)],
        out_specs=tok(d),
        out_shape=jax.ShapeDtypeStruct(h.shape, F32),
        scratch_shapes=[pltpu.VMEM((ts, wo.shape[0]), BF16)],
        compiler_params=_cparams(2),
        name="mixer_merge",
    )(h, mod, oa, ob, ocmp, oslc, owin, gf, wg, bg, wo)


def _cols(w, pieces):
    out = [jnp.zeros((w.shape[0], p), w.dtype) if isinstance(p, int) else w[:, p[0]:p[1]] for p in pieces]
    return jnp.concatenate(out, axis=1)


def _head_cols(base, order):
    return [(base + hd * HEAD, base + (hd + 1) * HEAD) for hd in order]


A_ORDER = (0, 2, 1, 3)
C_ORDER = (0, 4, 1, 5, 2, 6, 3, 7)


def _proj_weight(w):
    pieces = (_head_cols(0, A_ORDER) + [(256, 384), (384, 512), (512, 768), (768, 896)]
              + [64, (896, 928), 32] + _head_cols(928, C_ORDER)
              + [(1440 + 128 * i, 1568 + 128 * i) for i in range(6)] + [(2208, 2272), 64])
    return _cols(w, pieces).astype(BF16)


def _mla_weights(w_uq, w_ukv):
    dq = B_NOPE + B_ROPE
    uq = []
    for hd in range(B_HEADS):
        uq += [(hd * dq, hd * dq + dq), 32]
    dk = B_NOPE + B_V
    uk = []
    for hd in range(B_HEADS):
        uk += [(hd * dk, hd * dk + B_NOPE), 64]
    uv = [(hd * dk + B_NOPE, (hd + 1) * dk) for hd in range(B_HEADS)]
    return _cols(w_uq, uq).astype(BF16), _cols(w_ukv, uk + uv).astype(BF16)


def _tile_to(v, reps, width=512):
    t = jnp.tile(v, reps)
    return jnp.pad(t, (0, width - t.shape[0]))


def _gain_rows(a_qk_g, b_q_lat_g, b_kv_lat_g, b_qk_nope_g, b_qk_rope_g, c_qk_g):
    z32, z64 = jnp.zeros((32,), F32), jnp.zeros((64,), F32)
    rows = [_tile_to(a_qk_g[0], 4), _tile_to(a_qk_g[1], 2), _tile_to(c_qk_g[0], 8),
            _tile_to(c_qk_g[2], 2), _tile_to(c_qk_g[3], 2),
            jnp.tile(jnp.concatenate([b_qk_nope_g[0], b_qk_rope_g[0], z32]), 4),
            jnp.tile(jnp.concatenate([b_qk_nope_g[1], z64]), 4),
            _tile_to(jnp.concatenate([z64, b_qk_rope_g[1], z32]), 1),
            _tile_to(b_q_lat_g, 1), _tile_to(b_kv_lat_g, 1)]
    rows += [jnp.zeros((512,), F32)] * (16 - len(rows))
    return jnp.stack(rows)


def _block_diag_consts():
    i = np.arange(256)
    bd64 = np.where((i[:, None] // 64) == (i[None, :] // 64), 1.0 / 64, 0.0)
    j = i % 128
    grp = np.where(j < 64, 0, np.where(j < 96, 1, 2)) + 3 * (i // 128)
    size = np.where(j < 64, 64.0, 32.0)
    bdq = np.where(grp[:, None] == grp[None, :], 1.0 / size[None, :], 0.0)
    return jnp.asarray(bd64, BF16), jnp.asarray(bdq, BF16)


def _compress_weights(pe, w1, w2):
    n_tok = C_CMP_STRIDE
    pe_rows = jnp.broadcast_to(pe.reshape(2, 2, n_tok, 1, HEAD), (2, 2, n_tok, C_KV, HEAD))
    pe_rows = pe_rows.reshape(2, 2, n_tok * C_KV * HEAD)
    w1r = w1.reshape(2, 2, n_tok, HEAD, C_CMP_HID)
    eye = jnp.eye(C_KV, dtype=F32)
    w1big = jnp.einsum('xytdj,kq->xytkdqj', w1r, eye).reshape(2, 2, n_tok * C_KV * HEAD, C_KV * C_CMP_HID)
    w2big = jnp.einsum('xjd,kq->xkjqd', w2, eye).reshape(2, C_KV * C_CMP_HID, C_KV * HEAD)
    return pe_rows, w1big.astype(BF16), w2big.astype(BF16)


def _overlap_const(n_chunk):
    n_slc = n_chunk * C_CMP_STRIDE // C_SLC_BLOCK
    ci = np.arange(n_chunk)[:, None] * C_CMP_STRIDE
    sj = np.arange(n_slc)[None, :] * C_SLC_BLOCK
    ov = ((ci <= sj + C_SLC_BLOCK - 1) & (ci + C_CMP_BLOCK - 1 >= sj)).astype(np.float32)
    ov[n_chunk - 1] = 0.0
    out = np.zeros((C_KV, n_chunk, LANES), np.float32)
    for kv in range(C_KV):
        out[kv, :, kv * HEAD:kv * HEAD + n_slc] = ov
    return jnp.asarray(out, BF16)


def _expand_const(s, tk):
    r = np.arange(LANES)[None, :, None] % HEAD
    c = (np.arange(s // tk)[:, None, None] * tk + np.arange(tk)[None, None, :]) // C_SLC_BLOCK
    return jnp.asarray((r == c).astype(np.float32), BF16)


def _gate_weights(c_gate_w, c_gate_b):
    head_of_lane = np.repeat(np.asarray(C_ORDER), HEAD)
    cols = np.concatenate([head_of_lane * 3 + k for k in range(3)])
    wg = jnp.concatenate([c_gate_w[:, cols], jnp.zeros((LANES - C_GATE_HID, cols.size), F32)], axis=0)
    return wg.astype(BF16), c_gate_b[cols].reshape(1, -1)


def _out_weight(w_out):
    rows = ([(hd * HEAD, (hd + 1) * HEAD) for hd in A_ORDER] + [(256, 512)]
            + [(512 + hd * HEAD, 512 + (hd + 1) * HEAD) for hd in C_ORDER])
    return jnp.concatenate([w_out[a:b] for a, b in rows], axis=0).astype(BF16)


def kernel(x, c, positions, ada_w, ada_b, norm_g, ffn_w_in, ffn_w_out, w_in, w_out, a_sinks, a_qk_g,
           b_q_lat_g, b_kv_lat_g, b_w_uq, b_w_ukv, b_qk_nope_g, b_qk_rope_g, c_qk_g, c_cmp_pe, c_cmp_w1,
           c_cmp_w2, c_gate_w, c_gate_b):
    bsz, s, d = x.shape
    depth = ada_w.shape[0]
    n_chunk = s // C_CMP_STRIDE
    tq = min(256, s)
    assert s % tq == 0 and s % FLASH_TK == 0 and s >= C_WINDOW + tq and s // C_SLC_BLOCK <= HEAD

    mod_all = _ada_call(c, ada_w, ada_b).reshape(depth, bsz, N_MOD, d)
    tab = _rope_call(positions)
    bd64, bdq = _block_diag_consts()
    ov = _overlap_const(n_chunk)
    e = _expand_const(s, FLASH_TK)

    h = x
    for l in range(depth):
        mod = mod_all[l]
        h = _ffn_call(h, mod, norm_g[l, 0:1], ffn_w_in[l, 0].astype(BF16), ffn_w_out[l, 0].astype(BF16), 0)

        wuq, wukv = _mla_weights(b_w_uq[l], b_w_ukv[l])
        gv = _gain_rows(a_qk_g[l], b_q_lat_g[l], b_kv_lat_g[l], b_qk_nope_g[l], b_qk_rope_g[l], c_qk_g[l])
        (qa, ka, va, qb, kb, vb, qc, qcr, kc, vc, ks, vs, kw, vw, gf) = _prep_call(
            h, mod, norm_g[l, 1:2], _proj_weight(w_in[l]), wuq, wukv, gv, bd64, bdq, tab)

        pe_rows, w1big, w2big = _compress_weights(c_cmp_pe[l], c_cmp_w1[l], c_cmp_w2[l])
        kcmp, vcmp = _compress_call(kc.reshape(bsz, n_chunk, C_CMP_STRIDE * LANES),
                                    vc.reshape(bsz, n_chunk, C_CMP_STRIDE * LANES),
                                    pe_rows, w1big, w2big, bd64, jnp.tile(c_qk_g[l, 1], 2).reshape(1, LANES))

        oa = _window_call(qa, ka, va, a_sinks[l], A_WINDOW, BF16, "swa_attention")
        ob = _mla_call(qb, kb, vb)
        ocmp, sel = _cmp_select_call(qc, kcmp, vcmp, ov)
        oslc = _slc_call(qcr, ks, vs, sel, e)
        owin = _window_call(qcr, kw, vw, None, C_WINDOW, F32, "nsa_window")

        wg, bg = _gate_weights(c_gate_w[l], c_gate_b[l])
        h = _merge_call(h, mod, oa, ob, ocmp, oslc, owin, gf, wg, bg, _out_weight(w_out[l]))

        h = _ffn_call(h, mod, norm_g[l, 2:3], ffn_w_in[l, 1].astype(BF16), ffn_w_out[l, 1].astype(BF16), 6)
    return h
```

```python
import functools

import numpy as np
import jax
import jax.numpy as jnp
from jax import lax
from jax.experimental import pallas as pl
from jax.experimental.pallas import tpu as pltpu

F32 = jnp.float32
BF16 = jnp.bfloat16

LANES = 128
HEAD = 64
EPS = 1e-6
ROPE_THETA = 10000.0
NEG_INF = -1e30
FORCE_SCORE = 1e4
REMOVED = -3e38
N_MOD = 9

A_HEADS, A_KV, A_WINDOW = 4, 2, 128
B_HEADS, B_Q_LORA, B_KV_LORA, B_NOPE, B_ROPE, B_V = 4, 256, 128, 64, 32, 64
C_HEADS, C_KV, C_WINDOW = 8, 2, 256
C_CMP_BLOCK, C_CMP_STRIDE, C_SLC_BLOCK, C_N_SEL = 32, 16, 64, 8
C_CMP_HID, C_GATE_HID = 128, 64

VMEM_LIMIT = 56 * 1024 * 1024

O_AQ, O_AK, O_AV, O_BCQ, O_BCKV, O_KR, O_CQ = 0, 256, 384, 512, 768, 896, 1024
O_KC, O_VC, O_KS, O_VS, O_KW, O_VW, O_GF, W_PROJ = 1536, 1664, 1792, 1920, 2048, 2176, 2304, 2432


def _cparams(n_grid):
    return pltpu.CompilerParams(dimension_semantics=("arbitrary",) * n_grid,
                                vmem_limit_bytes=VMEM_LIMIT)


def _dot(a, b):
    return jnp.dot(a, b, preferred_element_type=F32)


def _dot_t(a, b):
    return lax.dot_general(a, b, (((1,), (1,)), ((), ())), preferred_element_type=F32)


def _silu(x):
    return x * jax.nn.sigmoid(x)


def _half_masks(dtype):
    lane = lax.broadcasted_iota(jnp.int32, (1, LANES), 1)
    lo = jnp.where(lane < HEAD, 1.0, 0.0).astype(dtype)
    hi = jnp.where(lane >= HEAD, 1.0, 0.0).astype(dtype)
    return lo, hi


def _ada_kernel(c_ref, w_ref, b_ref, o_ref):
    cond = _silu(c_ref[...]).astype(BF16)
    o_ref[0] = _dot(cond, w_ref[0].astype(BF16)) + b_ref[0]


def _ada_call(c, ada_w, ada_b):
    depth, d, nm = ada_w.shape
    bsz = c.shape[0]
    tn = 1024
    return pl.pallas_call(
        _ada_kernel,
        grid=(depth, nm // tn),
        in_specs=[pl.BlockSpec((bsz, d), lambda l, j: (0, 0)),
                  pl.BlockSpec((1, d, tn), lambda l, j: (l, 0, j)),
                  pl.BlockSpec((1, 1, tn), lambda l, j: (l, 0, j))],
        out_specs=pl.BlockSpec((1, bsz, tn), lambda l, j: (l, 0, j)),
        out_shape=jax.ShapeDtypeStruct((depth, bsz, nm), F32),
        compiler_params=_cparams(2),
        name="adaln",
    )(c, ada_w, ada_b.reshape(depth, 1, nm))


def _rope_kernel(pos_ref, f_ref, o_ref):
    pos = pos_ref[0]
    lane = lax.broadcasted_iota(jnp.int32, (1, LANES), 1)
    ang64 = pos * f_ref[0:1, :]
    sign64 = jnp.where((lane & 63) < 32, -1.0, 1.0)
    o_ref[0, :, 0:128] = jnp.cos(ang64)
    o_ref[0, :, 128:256] = jnp.sin(ang64) * sign64
    ang32 = pos * f_ref[1:2, :]
    sign32 = jnp.where(lane < 80, -1.0, 1.0)
    o_ref[0, :, 256:384] = jnp.cos(ang32)
    o_ref[0, :, 384:512] = jnp.sin(ang32) * sign32


def _rope_call(positions):
    bsz, s = positions.shape
    ts = min(512, s)
    inv64 = 1.0 / (ROPE_THETA ** (jnp.arange(0, HEAD, 2, dtype=F32) / HEAD))
    inv32 = 1.0 / (ROPE_THETA ** (jnp.arange(0, B_ROPE, 2, dtype=F32) / B_ROPE))
    z = jnp.zeros((32,), F32)
    f = jnp.stack([jnp.tile(inv64, 4),
                   jnp.concatenate([z, z, inv32, inv32, z])])
    pos = positions.astype(F32).reshape(bsz, s, 1)
    return pl.pallas_call(
        _rope_kernel,
        grid=(bsz, s // ts),
        in_specs=[pl.BlockSpec((1, ts, 1), lambda b, i: (b, i, 0)),
                  pl.BlockSpec((2, LANES), lambda b, i: (0, 0))],
        out_specs=pl.BlockSpec((1, ts, 512), lambda b, i: (b, i, 0)),
        out_shape=jax.ShapeDtypeStruct((bsz, s, 512), F32),
        compiler_params=_cparams(2),
        name="rope_tables",
    )(pos, f)


def _modulated_norm(h, g, scale, shift):
    y = h * lax.rsqrt(jnp.mean(h * h, axis=-1, keepdims=True) + EPS)
    return (y * g) * (1.0 + scale) + shift


def _ffn_kernel(h_ref, mod_ref, g_ref, wi_ref, wo_ref, o_ref, act_ref, *, mod_row, d_ff, chunk):
    h = h_ref[0]
    u = _modulated_norm(h, g_ref[...], mod_ref[0, mod_row + 1:mod_row + 2, :],
                        mod_ref[0, mod_row:mod_row + 1, :]).astype(BF16)
    for c in range(0, d_ff, chunk):
        gate = _dot(u, wi_ref[:, c:c + chunk])
        up = _dot(u, wi_ref[:, d_ff + c:d_ff + c + chunk])
        act_ref[:, c:c + chunk] = (_silu(gate) * up).astype(BF16)
    y = _dot(act_ref[...], wo_ref[...])
    o_ref[0] = h + (0.5 * mod_ref[0, mod_row + 2:mod_row + 3, :]) * y


def _ffn_call(h, mod, g, w_in, w_out, mod_row):
    bsz, s, d = h.shape
    d_ff = w_out.shape[0]
    ts = min(512, s)
    chunk = 256
    kern = functools.partial(_ffn_kernel, mod_row=mod_row, d_ff=d_ff, chunk=chunk)
    return pl.pallas_call(
        kern,
        grid=(bsz, s // ts),
        in_specs=[pl.BlockSpec((1, ts, d), lambda b, i: (b, i, 0)),
                  pl.BlockSpec((1, N_MOD, d), lambda b, i: (b, 0, 0)),
                  pl.BlockSpec((1, d), lambda b, i: (0, 0)),
                  pl.BlockSpec((d, 2 * d_ff), lambda b, i: (0, 0), pipeline_mode=pl.Buffered(1)),
                  pl.BlockSpec((d_ff, d), lambda b, i: (0, 0), pipeline_mode=pl.Buffered(1))],
        out_specs=pl.BlockSpec((1, ts, d), lambda b, i: (b, i, 0)),
        out_shape=jax.ShapeDtypeStruct(h.shape, F32),
        scratch_shapes=[pltpu.VMEM((ts, d_ff), BF16)],
        compiler_params=_cparams(2),
        name="ffn",
    )(h, mod, g, w_in, w_out)


def _group_mean_sq(x, bd_ref):
    w = x.shape[-1]
    sq = x * x
    hi = sq.astype(BF16)
    lo = (sq - hi.astype(F32)).astype(BF16)
    pieces = []
    step = 256 if w % 256 == 0 else 128
    for c in range(0, w, step):
        bd = bd_ref[0:step, 0:step]
        pieces.append(_dot(hi[:, c:c + step], bd) + _dot(lo[:, c:c + step], bd))
    return pieces[0] if len(pieces) == 1 else jnp.concatenate(pieces, axis=-1)


def _group_norm(x, bd_ref, gain):
    return x * lax.rsqrt(_group_mean_sq(x, bd_ref) + EPS) * gain


def _rope_slab(x, cos, sin, first_half, fwd, bwd):
    rot = jnp.where(first_half, pltpu.roll(x, fwd, axis=1), pltpu.roll(x, bwd, axis=1))
    return x * cos + rot * sin


def _rope64(x, cos, sin):
    lane = lax.broadcasted_iota(jnp.int32, (1, LANES), 1)
    first = (lane & 63) < 32
    out = [_rope_slab(x[:, c:c + LANES], cos, sin, first, 96, 32) for c in range(0, x.shape[-1], LANES)]
    return out[0] if len(out) == 1 else jnp.concatenate(out, axis=-1)


def _rope32(x, cos, sin):
    lane = lax.broadcasted_iota(jnp.int32, (1, LANES), 1)
    first = lane < 80
    out = [_rope_slab(x[:, c:c + LANES], cos, sin, first, 112, 16) for c in range(0, x.shape[-1], LANES)]
    return out[0] if len(out) == 1 else jnp.concatenate(out, axis=-1)


def _with_ones(v):
    lane = lax.broadcasted_iota(jnp.int32, (1, LANES), 1)
    out = []
    for c in range(0, v.shape[-1], LANES):
        slab = v[:, c:c + LANES]
        out += [jnp.where(lane < HEAD, slab, 1.0), jnp.where(lane < HEAD, 1.0, slab)]
    return jnp.concatenate(out, axis=-1).astype(BF16)


def _prep_kernel(h_ref, mod_ref, g_ref, w_ref, wuq_ref, wukv_ref, gv_ref, bd64_ref, bdq_ref, tab_ref,
                 qa_ref, ka_ref, va_ref, qb_ref, kb_ref, vb_ref, qc_ref, qcr_ref,
                 kc_ref, vc_ref, ks_ref, vs_ref, kw_ref, vw_ref, gf_ref):
    u = _modulated_norm(h_ref[0], g_ref[...], mod_ref[0, 4:5, :], mod_ref[0, 3:4, :]).astype(BF16)
    cos64, sin64 = tab_ref[0, :, 0:128], tab_ref[0, :, 128:256]
    cos32, sin32 = tab_ref[0, :, 256:384], tab_ref[0, :, 384:512]

    proj_all = _dot(u, w_ref[...])

    def proj(a, b):
        return proj_all[:, a:b]

    qa = _rope64(_group_norm(proj(O_AQ, O_AK), bd64_ref, gv_ref[0:1, 0:256]), cos64, sin64)
    qa_ref[0] = (qa * 0.125).astype(BF16)
    ka = _rope64(_group_norm(proj(O_AK, O_AV), bd64_ref, gv_ref[1:2, 0:128]), cos64, sin64)
    ka_ref[0] = ka.astype(BF16)
    va_ref[0] = _with_ones(proj(O_AV, O_BCQ))

    cq = proj(O_BCQ, O_BCKV)
    cq = cq * lax.rsqrt(jnp.mean(cq * cq, axis=-1, keepdims=True) + EPS) * gv_ref[8:9, 0:256]
    qb = _group_norm(_dot(cq.astype(BF16), wuq_ref[...]), bdq_ref, gv_ref[5:6, :])
    qb_ref[0] = _rope32(qb, cos32, sin32).astype(BF16)
    ckv = proj(O_BCKV, O_KR)
    ckv = (ckv * lax.rsqrt(jnp.mean(ckv * ckv, axis=-1, keepdims=True) + EPS) * gv_ref[9:10, 0:128]).astype(BF16)
    k_nope = _group_norm(_dot(ckv, wukv_ref[:, 0:512]), bd64_ref, gv_ref[6:7, :])
    vb_ref[0] = _with_ones(_dot(ckv, wukv_ref[:, 512:768]))
    k_pe = _rope32(_group_norm(proj(O_KR, O_CQ), bdq_ref, gv_ref[7:8, 0:128]), cos32, sin32)
    for hh in range(B_HEADS):
        kb_ref[0, :, hh * LANES:(hh + 1) * LANES] = (k_nope[:, hh * LANES:(hh + 1) * LANES] + k_pe).astype(BF16)

    qc = _group_norm(proj(O_CQ, O_KC), bd64_ref, gv_ref[2:3, :]) * 0.125
    qc_ref[0] = qc.astype(BF16)
    qcr_ref[0] = _rope64(qc, cos64, sin64).astype(BF16)
    kc_ref[0] = proj(O_KC, O_VC)
    vc_ref[0] = proj(O_VC, O_KS)
    ks = _rope64(_group_norm(proj(O_KS, O_VS), bd64_ref, gv_ref[3:4, 0:128]), cos64, sin64)
    ks_ref[0] = ks.astype(BF16)
    vs_ref[0] = _with_ones(proj(O_VS, O_KW))
    kw = _rope64(_group_norm(proj(O_KW, O_VW), bd64_ref, gv_ref[4:5, 0:128]), cos64, sin64)
    kw_ref[0] = kw.astype(BF16)
    vw_ref[0] = _with_ones(proj(O_VW, O_GF))
    gf_ref[0] = proj(O_GF, W_PROJ)


def _prep_call(h, mod, g, w, wuq, wukv, gv, bd64, bdq, tab):
    bsz, s, d = h.shape
    ts = min(512, s)
    widths = [(256, BF16), (128, BF16), (256, BF16), (512, BF16), (512, BF16), (512, BF16),
              (512, BF16), (512, BF16), (128, F32), (128, F32), (128, BF16), (256, BF16),
              (128, BF16), (256, BF16), (128, F32)]

    def const(shape):
        return pl.BlockSpec(shape, lambda b, i: (0,) * len(shape))

    return pl.pallas_call(
        _prep_kernel,
        grid=(bsz, s // ts),
        in_specs=[pl.BlockSpec((1, ts, d), lambda b, i: (b, i, 0)),
                  pl.BlockSpec((1, N_MOD, d), lambda b, i: (b, 0, 0)),
                  const((1, d)), const(w.shape), const(wuq.shape), const(wukv.shape),
                  const(gv.shape), const(bd64.shape), const(bdq.shape),
                  pl.BlockSpec((1, ts, 512), lambda b, i: (b, i, 0))],
        out_specs=[pl.BlockSpec((1, ts, wd), lambda b, i: (b, i, 0)) for wd, _ in widths],
        out_shape=[jax.ShapeDtypeStruct((bsz, s, wd), dt) for wd, dt in widths],
        compiler_params=_cparams(2),
        name="mixer_proj",
    )(h, mod, g, w, wuq, wukv, gv, bd64, bdq, tab)


def _compress_kernel(kc_ref, vc_ref, pe_ref, w1_ref, w2_ref, bd64_ref, g_ref, ko_ref, vo_ref):
    n = kc_ref.shape[1]

    def one(z_ref, idx):
        z = z_ref[0]
        top = _dot((z + pe_ref[idx, 0:1, :]).astype(BF16), w1_ref[idx, 0])
        bot = _dot((z + pe_ref[idx, 1:2, :]).astype(BF16), w1_ref[idx, 1])
        pre = top + pltpu.roll(bot, n - 1, axis=0)
        return _dot(_silu(pre).astype(BF16), w2_ref[idx])

    ko_ref[0] = _group_norm(one(kc_ref, 0), bd64_ref, g_ref[...]).astype(BF16)
    vo_ref[0] = one(vc_ref, 1).astype(BF16)


def _compress_call(kc, vc, pe, w1, w2, bd64, g):
    bsz, n, wz = kc.shape

    def const(shape):
        return pl.BlockSpec(shape, lambda b: (0,) * len(shape))

    return pl.pallas_call(
        _compress_kernel,
        grid=(bsz,),
        in_specs=[pl.BlockSpec((1, n, wz), lambda b: (b, 0, 0)),
                  pl.BlockSpec((1, n, wz), lambda b: (b, 0, 0)),
                  const(pe.shape), const(w1.shape), const(w2.shape), const(bd64.shape), const(g.shape)],
        out_specs=[pl.BlockSpec((1, n, LANES), lambda b: (b, 0, 0))] * 2,
        out_shape=[jax.ShapeDtypeStruct((bsz, n, LANES), BF16)] * 2,
        compiler_params=_cparams(1),
        name="nsa_compress",
    )(kc, vc, pe, w1, w2, bd64, g)


def _window_kernel(*refs, window, tq, n_slab, n_group, use_sink):
    if use_sink:
        sink_ref, q_ref, k_ref, v_ref, o_ref = refs
    else:
        q_ref, k_ref, v_ref, o_ref = refs
    span = window + tq
    q0 = pl.program_id(1) * tq
    start = pl.multiple_of(jnp.maximum(q0 - window, 0), LANES)
    kwin = k_ref[0, pl.ds(start, span), :]
    rel = (q0 + lax.broadcasted_iota(jnp.int32, (tq, span), 0)) - \
          (start + lax.broadcasted_iota(jnp.int32, (tq, span), 1))
    visible = (rel >= 0) & (rel < window)
    half_masks = _half_masks(BF16)
    lane = lax.broadcasted_iota(jnp.int32, (1, LANES), 1)
    scs = [[_dot_t(q_ref[0, :, s * LANES:(s + 1) * LANES] * hm, kwin) for hm in half_masks] for s in range(n_slab)]
    for s in range(n_slab):
        outs = []
        for kv in range(2):
            sc = jnp.where(visible, scs[s][kv], NEG_INF)
            m = jnp.max(sc, axis=-1, keepdims=True)
            if use_sink:
                sink = sink_ref[kv * n_group + s]
                m = jnp.maximum(m, sink)
            p = jnp.exp2((sc - m) * LOG2E)
            acc = _dot(p.astype(BF16), v_ref[0, pl.ds(start, span), kv * LANES:(kv + 1) * LANES])
            denom = jnp.sum(jnp.where(lane == (1 - kv) * HEAD, acc, 0.0), axis=-1, keepdims=True)
            if use_sink:
                denom = denom + jnp.exp2((sink - m) * LOG2E)
            outs.append(acc / denom)
        o_ref[0, :, s * LANES:(s + 1) * LANES] = jnp.where(lane < HEAD, outs[0], outs[1]).astype(o_ref.dtype)


def _window_call(q, k, v, sinks, window, out_dtype, name):
    bsz, s, wq = q.shape
    tq = WINDOW_TQ
    n_slab = wq // LANES
    use_sink = sinks is not None
    kern = functools.partial(_window_kernel, window=window, tq=tq, n_slab=n_slab, n_group=n_slab,
                             use_sink=use_sink)
    in_specs = [pl.BlockSpec((1, tq, wq), lambda b, i: (b, i, 0)),
                pl.BlockSpec((1, s, LANES), lambda b, i: (b, 0, 0)),
                pl.BlockSpec((1, s, 2 * LANES), lambda b, i: (b, 0, 0))]
    args = [q, k, v]
    if use_sink:
        in_specs = [pl.BlockSpec(memory_space=pltpu.SMEM)] + in_specs
        args = [sinks] + args
    return pl.pallas_call(
        kern,
        grid=(bsz, s // tq),
        in_specs=in_specs,
        out_specs=pl.BlockSpec((1, tq, wq), lambda b, i: (b, i, 0)),
        out_shape=jax.ShapeDtypeStruct((bsz, s, wq), out_dtype),
        compiler_params=_cparams(2),
        name=name,
    )(*args)


LOG2E = 1.4426950408889634


FLASH_TQ, FLASH_TK = 256, 512
WINDOW_TQ = 128


def _softmax_state(n_heads, tq):
    return ((jnp.full((tq, 1), NEG_INF, F32), jnp.zeros((tq, LANES), F32)),) * n_heads


def _online_step(state, sc, vt, c):
    m_prev, acc = state
    m_new = jnp.maximum(m_prev, jnp.max(sc, axis=-1, keepdims=True))
    alpha = jnp.exp2((m_prev - m_new) * c)
    p = jnp.exp2((sc - m_new) * c)
    return m_new, alpha * acc + _dot(p.astype(BF16), vt)


def _normalised(acc, value_half):
    lane = lax.broadcasted_iota(jnp.int32, (1, LANES), 1)
    denom_lane = (1 - value_half) * HEAD
    denom = jnp.sum(jnp.where(lane == denom_lane, acc, 0.0), axis=-1, keepdims=True)
    return acc / denom


def _causal_tile(q0, k0, tq, tk):
    return (k0 + lax.broadcasted_iota(jnp.int32, (tq, tk), 1)) <= (q0 + lax.broadcasted_iota(jnp.int32, (tq, tk), 0))


def _mla_kernel(q_ref, k_ref, v_ref, o_ref, *, tq, tk, c):
    q0 = pl.program_id(1) * tq
    n_full = q0 // tk
    lane = lax.broadcasted_iota(jnp.int32, (1, LANES), 1)

    def step(j, states, diagonal):
        rows = pl.ds(pl.multiple_of(j * tk, tk), tk)
        if diagonal:
            causal = _causal_tile(q0, j * tk, tq, tk)
        scs = [_dot_t(q_ref[0, :, hh * LANES:(hh + 1) * LANES], k_ref[0, rows, hh * LANES:(hh + 1) * LANES])
               for hh in range(B_HEADS)]
        out = []
        for hh in range(B_HEADS):
            sc = jnp.where(causal, scs[hh], NEG_INF) if diagonal else scs[hh]
            out.append(_online_step(states[hh], sc, v_ref[0, rows, hh * LANES:(hh + 1) * LANES], c))
        return tuple(out)

    states = lax.fori_loop(0, n_full, lambda j, st: step(j, st, False), _softmax_state(B_HEADS, tq))
    states = step(n_full, states, True)
    for s in range(B_HEADS // 2):
        o_lo = _normalised(states[2 * s][1], 0)
        o_hi = _normalised(states[2 * s + 1][1], 1)
        o_ref[0, :, s * LANES:(s + 1) * LANES] = jnp.where(lane < HEAD, o_lo, o_hi).astype(o_ref.dtype)


def _mla_call(q, k, v):
    bsz, s, _ = q.shape
    tq, tk = FLASH_TQ, FLASH_TK
    kern = functools.partial(_mla_kernel, tq=tq, tk=tk, c=float((B_NOPE + B_ROPE) ** -0.5) * LOG2E)
    return pl.pallas_call(
        kern,
        grid=(bsz, s // tq),
        in_specs=[pl.BlockSpec((1, tq, 512), lambda b, i: (b, i, 0)),
                  pl.BlockSpec((1, s, 512), lambda b, i: (b, 0, 0)),
                  pl.BlockSpec((1, s, 512), lambda b, i: (b, 0, 0))],
        out_specs=pl.BlockSpec((1, tq, 256), lambda b, i: (b, i, 0)),
        out_shape=jax.ShapeDtypeStruct((bsz, s, 256), BF16),
        compiler_params=_cparams(2),
        name="mla_attention",
    )(q, k, v)


def _cmp_select_kernel(q_ref, k_ref, v_ref, ov_ref, o_ref, sel_ref, *, tq):
    n = k_ref.shape[1]
    t = pl.program_id(1) * tq + lax.broadcasted_iota(jnp.int32, (tq, n), 0)
    j = lax.broadcasted_iota(jnp.int32, (tq, n), 1)
    visible = (j * C_CMP_STRIDE + (C_CMP_BLOCK - 1)) <= t
    kc, vc = k_ref[0], v_ref[0]
    m_lo, m_hi = _half_masks(BF16)
    lane = lax.broadcasted_iota(jnp.int32, (1, LANES), 1)
    p_sum = [jnp.zeros((tq, n), F32), jnp.zeros((tq, n), F32)]
    scs = [[_dot_t(q_ref[0, :, s * LANES:(s + 1) * LANES] * hm, kc) for hm in (m_lo, m_hi)]
           for s in range(C_HEADS // C_KV)]
    for s in range(C_HEADS // C_KV):
        outs = []
        for kv in range(C_KV):
            sc = jnp.where(visible, scs[s][kv], NEG_INF)
            m = jnp.max(sc, axis=-1, keepdims=True)
            p = jnp.where(visible, jnp.exp(sc - m), 0.0)
            denom = jnp.sum(p, axis=-1, keepdims=True)
            p = p * jnp.where(denom > 0.0, 1.0 / denom, 0.0)
            p_sum[kv] = p_sum[kv] + p
            outs.append(_dot(p.astype(BF16), vc))
        o_ref[0, :, s * LANES:(s + 1) * LANES] = jnp.where(lane < HEAD, outs[0], outs[1])

    imp = jnp.zeros((tq, LANES), F32)
    for kv in range(C_KV):
        hi = p_sum[kv].astype(BF16)
        lo = (p_sum[kv] - hi.astype(F32)).astype(BF16)
        imp = imp + _dot(hi, ov_ref[kv]) + _dot(lo, ov_ref[kv])
    tl = pl.program_id(1) * tq + lax.broadcasted_iota(jnp.int32, (tq, LANES), 0)
    lane_f = lax.broadcasted_iota(jnp.int32, (tq, LANES), 1)
    blk = lane_f & (HEAD - 1)
    cur = tl >> 6
    forced = (blk == 0) | (blk == cur) | (blk == cur - 1)
    future = blk > cur
    score = jnp.where(future, NEG_INF, jnp.where(forced, FORCE_SCORE, imp))
    n_forced = 3
    sel = jnp.where(forced, 1.0, 0.0)
    score = jnp.where(forced, REMOVED, score)
    lane_id = lane_f.astype(F32)
    for kv in range(C_KV):
        in_group = (lane_f >= kv * HEAD) & (lane_f < (kv + 1) * HEAD)
        sc = jnp.where(in_group, score, REMOVED)
        for _ in range(C_N_SEL - n_forced):
            best = jnp.max(sc, axis=-1, keepdims=True)
            first = jnp.min(jnp.where(sc == best, lane_id, 2.0 * LANES), axis=-1, keepdims=True)
            hit = lane_id == first
            sel = jnp.where(hit, 1.0, sel)
            sc = jnp.where(hit, REMOVED, sc)
    sel_ref[0] = jnp.where(future, 0.0, sel).astype(BF16)


def _cmp_select_call(q, kcmp, vcmp, ov):
    bsz, s, wq = q.shape
    n = kcmp.shape[1]
    tq = min(256, s)
    kern = functools.partial(_cmp_select_kernel, tq=tq)
    return pl.pallas_call(
        kern,
        grid=(bsz, s // tq),
        in_specs=[pl.BlockSpec((1, tq, wq), lambda b, i: (b, i, 0)),
                  pl.BlockSpec((1, n, LANES), lambda b, i: (b, 0, 0)),
                  pl.BlockSpec((1, n, LANES), lambda b, i: (b, 0, 0)),
                  pl.BlockSpec(ov.shape, lambda b, i: (0, 0, 0))],
        out_specs=[pl.BlockSpec((1, tq, wq), lambda b, i: (b, i, 0)),
                   pl.BlockSpec((1, tq, LANES), lambda b, i: (b, i, 0))],
        out_shape=[jax.ShapeDtypeStruct((bsz, s, wq), F32),
                   jax.ShapeDtypeStruct((bsz, s, LANES), BF16)],
        compiler_params=_cparams(2),
        name="nsa_cmp_select",
    )(q, kcmp, vcmp, ov)


def _slc_kernel(q_ref, k_ref, v_ref, sel_ref, e_ref, o_ref, *, tq, tk):
    q0 = pl.program_id(1) * tq
    n_full = q0 // tk
    lane = lax.broadcasted_iota(jnp.int32, (1, LANES), 1)
    m_lo, m_hi = _half_masks(BF16)
    n_slab = C_HEADS // C_KV
    outs = []
    for kv, hm in enumerate((m_lo, m_hi)):
        sel_h = sel_ref[0] * hm
        qm = [q_ref[0, :, s * LANES:(s + 1) * LANES] * hm for s in range(n_slab)]

        def step(j, states, diagonal):
            rows = pl.ds(pl.multiple_of(j * tk, tk), tk)
            kt = k_ref[0, rows, :]
            vt = v_ref[0, rows, kv * LANES:(kv + 1) * LANES]
            chosen = _dot(sel_h, e_ref[j]) > 0.5
            if diagonal:
                chosen = chosen & _causal_tile(q0, j * tk, tq, tk)
            scs = [_dot_t(qm[s], kt) for s in range(n_slab)]
            return tuple(_online_step(states[s], jnp.where(chosen, scs[s], NEG_INF), vt, LOG2E)
                         for s in range(n_slab))

        states = lax.fori_loop(0, n_full, lambda j, st: step(j, st, False), _softmax_state(n_slab, tq))
        states = step(n_full, states, True)
        outs.append([_normalised(states[s][1], kv) for s in range(n_slab)])
    for s in range(n_slab):
        o_ref[0, :, s * LANES:(s + 1) * LANES] = jnp.where(lane < HEAD, outs[0][s], outs[1][s])


def _slc_call(q, k, v, sel, e):
    bsz, s, wq = q.shape
    tq, tk = FLASH_TQ, e.shape[2]
    kern = functools.partial(_slc_kernel, tq=tq, tk=tk)
    return pl.pallas_call(
        kern,
        grid=(bsz, s // tq),
        in_specs=[pl.BlockSpec((1, tq, wq), lambda b, i: (b, i, 0)),
                  pl.BlockSpec((1, s, LANES), lambda b, i: (b, 0, 0)),
                  pl.BlockSpec((1, s, 2 * LANES), lambda b, i: (b, 0, 0)),
                  pl.BlockSpec((1, tq, LANES), lambda b, i: (b, i, 0)),
                  pl.BlockSpec(e.shape, lambda b, i: (0, 0, 0))],
        out_specs=pl.BlockSpec((1, tq, wq), lambda b, i: (b, i, 0)),
        out_shape=jax.ShapeDtypeStruct((bsz, s, wq), F32),
        compiler_params=_cparams(2),
        name="nsa_selected",
    )(q, k, v, sel, e)


def _merge_kernel(h_ref, mod_ref, oa_ref, ob_ref, ocmp_ref, oslc_ref, owin_ref, gf_ref, wg_ref, bg_ref,
                  wo_ref, o_ref, cat_ref):
    wc = ocmp_ref.shape[2]
    gates = jax.nn.sigmoid(_dot(_silu(gf_ref[0]).astype(BF16), wg_ref[...]) + bg_ref[...])
    oc = gates[:, 0:wc] * ocmp_ref[0] + gates[:, wc:2 * wc] * oslc_ref[0] + gates[:, 2 * wc:3 * wc] * owin_ref[0]
    cat_ref[:, 0:256] = oa_ref[0]
    cat_ref[:, 256:512] = ob_ref[0]
    cat_ref[:, 512:512 + wc] = oc.astype(BF16)
    o_ref[0] = h_ref[0] + mod_ref[0, 5:6, :] * _dot(cat_ref[...], wo_ref[...])


def _merge_call(h, mod, oa, ob, ocmp, oslc, owin, gf, wg, bg, wo):
    bsz, s, d = h.shape
    ts = min(512, s)

    def tok(wd):
        return pl.BlockSpec((1, ts, wd), lambda b, i: (b, i, 0))

    def const(shape):
        return pl.BlockSpec(shape, lambda b, i: (0,) * len(shape))

    return pl.pallas_call(
        _merge_kernel,
        grid=(bsz, s // ts),
        in_specs=[tok(d), pl.BlockSpec((1, N_MOD, d), lambda b, i: (b, 0, 0)),
                  tok(256), tok(256), tok(512), tok(512), tok(512), tok(LANES),
                  const(wg.shape), const(bg.shape), const(wo.shape)],
        out_specs=tok(d),
        out_shape=jax.ShapeDtypeStruct(h.shape, F32),
        scratch_shapes=[pltpu.VMEM((ts, wo.shape[0]), BF16)],
        compiler_params=_cparams(2),
        name="mixer_merge",
    )(h, mod, oa, ob, ocmp, oslc, owin, gf, wg, bg, wo)


def _cols(w, pieces):
    out = [jnp.zeros((w.shape[0], p), w.dtype) if isinstance(p, int) else w[:, p[0]:p[1]] for p in pieces]
    return jnp.concatenate(out, axis=1)


def _head_cols(base, order):
    return [(base + hd * HEAD, base + (hd + 1) * HEAD) for hd in order]


A_ORDER = (0, 2, 1, 3)
C_ORDER = (0, 4, 1, 5, 2, 6, 3, 7)


def _proj_weight(w):
    pieces = (_head_cols(0, A_ORDER) + [(256, 384), (384, 512), (512, 768), (768, 896)]
              + [64, (896, 928), 32] + _head_cols(928, C_ORDER)
              + [(1440 + 128 * i, 1568 + 128 * i) for i in range(6)] + [(2208, 2272), 64])
    return _cols(w, pieces).astype(BF16)


def _mla_weights(w_uq, w_ukv):
    dq = B_NOPE + B_ROPE
    uq = []
    for hd in range(B_HEADS):
        uq += [(hd * dq, hd * dq + dq), 32]
    dk = B_NOPE + B_V
    uk = []
    for hd in range(B_HEADS):
        uk += [(hd * dk, hd * dk + B_NOPE), 64]
    uv = [(hd * dk + B_NOPE, (hd + 1) * dk) for hd in range(B_HEADS)]
    return _cols(w_uq, uq).astype(BF16), _cols(w_ukv, uk + uv).astype(BF16)


def _tile_to(v, reps, width=512):
    t = jnp.tile(v, reps)
    return jnp.pad(t, (0, width - t.shape[0]))


def _gain_rows(a_qk_g, b_q_lat_g, b_kv_lat_g, b_qk_nope_g, b_qk_rope_g, c_qk_g):
    z32, z64 = jnp.zeros((32,), F32), jnp.zeros((64,), F32)
    rows = [_tile_to(a_qk_g[0], 4), _tile_to(a_qk_g[1], 2), _tile_to(c_qk_g[0], 8),
            _tile_to(c_qk_g[2], 2), _tile_to(c_qk_g[3], 2),
            jnp.tile(jnp.concatenate([b_qk_nope_g[0], b_qk_rope_g[0], z32]), 4),
            jnp.tile(jnp.concatenate([b_qk_nope_g[1], z64]), 4),
            _tile_to(jnp.concatenate([z64, b_qk_rope_g[1], z32]), 1),
            _tile_to(b_q_lat_g, 1), _tile_to(b_kv_lat_g, 1)]
    rows += [jnp.zeros((512,), F32)] * (16 - len(rows))
    return jnp.stack(rows)


def _block_diag_consts():
    i = np.arange(256)
    bd64 = np.where((i[:, None] // 64) == (i[None, :] // 64), 1.0 / 64, 0.0)
    j = i % 128
    grp = np.where(j < 64, 0, np.where(j < 96, 1, 2)) + 3 * (i // 128)
    size = np.where(j < 64, 64.0, 32.0)
    bdq = np.where(grp[:, None] == grp[None, :], 1.0 / size[None, :], 0.0)
    return jnp.asarray(bd64, BF16), jnp.asarray(bdq, BF16)


def _compress_weights(pe, w1, w2):
    n_tok = C_CMP_STRIDE
    pe_rows = jnp.broadcast_to(pe.reshape(2, 2, n_tok, 1, HEAD), (2, 2, n_tok, C_KV, HEAD))
    pe_rows = pe_rows.reshape(2, 2, n_tok * C_KV * HEAD)
    w1r = w1.reshape(2, 2, n_tok, HEAD, C_CMP_HID)
    eye = jnp.eye(C_KV, dtype=F32)
    w1big = jnp.einsum('xytdj,kq->xytkdqj', w1r, eye).reshape(2, 2, n_tok * C_KV * HEAD, C_KV * C_CMP_HID)
    w2big = jnp.einsum('xjd,kq->xkjqd', w2, eye).reshape(2, C_KV * C_CMP_HID, C_KV * HEAD)
    return pe_rows, w1big.astype(BF16), w2big.astype(BF16)


def _overlap_const(n_chunk):
    n_slc = n_chunk * C_CMP_STRIDE // C_SLC_BLOCK
    ci = np.arange(n_chunk)[:, None] * C_CMP_STRIDE
    sj = np.arange(n_slc)[None, :] * C_SLC_BLOCK
    ov = ((ci <= sj + C_SLC_BLOCK - 1) & (ci + C_CMP_BLOCK - 1 >= sj)).astype(np.float32)
    ov[n_chunk - 1] = 0.0
    out = np.zeros((C_KV, n_chunk, LANES), np.float32)
    for kv in range(C_KV):
        out[kv, :, kv * HEAD:kv * HEAD + n_slc] = ov
    return jnp.asarray(out, BF16)


def _expand_const(s, tk):
    r = np.arange(LANES)[None, :, None] % HEAD
    c = (np.arange(s // tk)[:, None, None] * tk + np.arange(tk)[None, None, :]) // C_SLC_BLOCK
    return jnp.asarray((r == c).astype(np.float32), BF16)


def _gate_weights(c_gate_w, c_gate_b):
    head_of_lane = np.repeat(np.asarray(C_ORDER), HEAD)
    cols = np.concatenate([head_of_lane * 3 + k for k in range(3)])
    wg = jnp.concatenate([c_gate_w[:, cols], jnp.zeros((LANES - C_GATE_HID, cols.size), F32)], axis=0)
    return wg.astype(BF16), c_gate_b[cols].reshape(1, -1)


def _out_weight(w_out):
    rows = ([(hd * HEAD, (hd + 1) * HEAD) for hd in A_ORDER] + [(256, 512)]
            + [(512 + hd * HEAD, 512 + (hd + 1) * HEAD) for hd in C_ORDER])
    return jnp.concatenate([w_out[a:b] for a, b in rows], axis=0).astype(BF16)


def kernel(x, c, positions, ada_w, ada_b, norm_g, ffn_w_in, ffn_w_out, w_in, w_out, a_sinks, a_qk_g,
           b_q_lat_g, b_kv_lat_g, b_w_uq, b_w_ukv, b_qk_nope_g, b_qk_rope_g, c_qk_g, c_cmp_pe, c_cmp_w1,
           c_cmp_w2, c_gate_w, c_gate_b):
    bsz, s, d = x.shape
    depth = ada_w.shape[0]
    n_chunk = s // C_CMP_STRIDE
    tq = min(256, s)
    assert s % tq == 0 and s % FLASH_TK == 0 and s >= C_WINDOW + tq and s // C_SLC_BLOCK <= HEAD

    mod_all = _ada_call(c, ada_w, ada_b).reshape(depth, bsz, N_MOD, d)
    tab = _rope_call(positions)
    bd64, bdq = _block_diag_consts()
    ov = _overlap_const(n_chunk)
    e = _expand_const(s, FLASH_TK)

    h = x
    for l in range(depth):
        mod = mod_all[l]
        h = _ffn_call(h, mod, norm_g[l, 0:1], ffn_w_in[l, 0].astype(BF16), ffn_w_out[l, 0].astype(BF16), 0)

        wuq, wukv = _mla_weights(b_w_uq[l], b_w_ukv[l])
        gv = _gain_rows(a_qk_g[l], b_q_lat_g[l], b_kv_lat_g[l], b_qk_nope_g[l], b_qk_rope_g[l], c_qk_g[l])
        (qa, ka, va, qb, kb, vb, qc, qcr, kc, vc, ks, vs, kw, vw, gf) = _prep_call(
            h, mod, norm_g[l, 1:2], _proj_weight(w_in[l]), wuq, wukv, gv, bd64, bdq, tab)

        pe_rows, w1big, w2big = _compress_weights(c_cmp_pe[l], c_cmp_w1[l], c_cmp_w2[l])
        kcmp, vcmp = _compress_call(kc.reshape(bsz, n_chunk, C_CMP_STRIDE * LANES),
                                    vc.reshape(bsz, n_chunk, C_CMP_STRIDE * LANES),
                                    pe_rows, w1big, w2big, bd64, jnp.tile(c_qk_g[l, 1], 2).reshape(1, LANES))

        oa = _window_call(qa, ka, va, a_sinks[l], A_WINDOW, BF16, "swa_attention")
        ob = _mla_call(qb, kb, vb)
        ocmp, sel = _cmp_select_call(qc, kcmp, vcmp, ov)
        oslc = _slc_call(qcr, ks, vs, sel, e)
        owin = _window_call(qcr, kw, vw, None, C_WINDOW, F32, "nsa_window")

        wg, bg = _gate_weights(c_gate_w[l], c_gate_b[l])
        h = _merge_call(h, mod, oa, ob, ocmp, oslc, owin, gf, wg, bg, _out_weight(w_out[l]))

        h = _ffn_call(h, mod, norm_g[l, 2:3], ffn_w_in[l, 1].astype(BF16), ffn_w_out[l, 1].astype(BF16), 6)
    return h
```

```python
import functools

import numpy as np
import jax
import jax.numpy as jnp
from jax import lax
from jax.experimental import pallas as pl
from jax.experimental.pallas import tpu as pltpu

F32 = jnp.float32
BF16 = jnp.bfloat16

LANES = 128
HEAD = 64
EPS = 1e-6
ROPE_THETA = 10000.0
NEG_INF = -1e30
FORCE_SCORE = 1e4
REMOVED = -3e38
N_MOD = 9

A_HEADS, A_KV, A_WINDOW = 4, 2, 128
B_HEADS, B_Q_LORA, B_KV_LORA, B_NOPE, B_ROPE, B_V = 4, 256, 128, 64, 32, 64
C_HEADS, C_KV, C_WINDOW = 8, 2, 256
C_CMP_BLOCK, C_CMP_STRIDE, C_SLC_BLOCK, C_N_SEL = 32, 16, 64, 8
C_CMP_HID, C_GATE_HID = 128, 64

VMEM_LIMIT = 56 * 1024 * 1024

O_AQ, O_AK, O_AV, O_BCQ, O_BCKV, O_KR, O_CQ = 0, 256, 384, 512, 768, 896, 1024
O_KC, O_VC, O_KS, O_VS, O_KW, O_VW, O_GF, W_PROJ = 1536, 1664, 1792, 1920, 2048, 2176, 2304, 2432


def _cparams(n_grid):
    return pltpu.CompilerParams(dimension_semantics=("arbitrary",) * n_grid,
                                vmem_limit_bytes=VMEM_LIMIT)


def _dot(a, b):
    return jnp.dot(a, b, preferred_element_type=F32)


def _dot_t(a, b):
    return lax.dot_general(a, b, (((1,), (1,)), ((), ())), preferred_element_type=F32)


def _silu(x):
    return x * jax.nn.sigmoid(x)


def _half_masks(dtype):
    lane = lax.broadcasted_iota(jnp.int32, (1, LANES), 1)
    lo = jnp.where(lane < HEAD, 1.0, 0.0).astype(dtype)
    hi = jnp.where(lane >= HEAD, 1.0, 0.0).astype(dtype)
    return lo, hi


def _ada_kernel(c_ref, w_ref, b_ref, o_ref):
    cond = _silu(c_ref[...]).astype(BF16)
    o_ref[0] = _dot(cond, w_ref[0].astype(BF16)) + b_ref[0]


def _ada_call(c, ada_w, ada_b):
    depth, d, nm = ada_w.shape
    bsz = c.shape[0]
    tn = 1024
    return pl.pallas_call(
        _ada_kernel,
        grid=(depth, nm // tn),
        in_specs=[pl.BlockSpec((bsz, d), lambda l, j: (0, 0)),
                  pl.BlockSpec((1, d, tn), lambda l, j: (l, 0, j)),
                  pl.BlockSpec((1, 1, tn), lambda l, j: (l, 0, j))],
        out_specs=pl.BlockSpec((1, bsz, tn), lambda l, j: (l, 0, j)),
        out_shape=jax.ShapeDtypeStruct((depth, bsz, nm), F32),
        compiler_params=_cparams(2),
        name="adaln",
    )(c, ada_w, ada_b.reshape(depth, 1, nm))


def _rope_kernel(pos_ref, f_ref, o_ref):
    pos = pos_ref[0]
    lane = lax.broadcasted_iota(jnp.int32, (1, LANES), 1)
    ang64 = pos * f_ref[0:1, :]
    sign64 = jnp.where((lane & 63) < 32, -1.0, 1.0)
    o_ref[0, :, 0:128] = jnp.cos(ang64)
    o_ref[0, :, 128:256] = jnp.sin(ang64) * sign64
    ang32 = pos * f_ref[1:2, :]
    sign32 = jnp.where(lane < 80, -1.0, 1.0)
    o_ref[0, :, 256:384] = jnp.cos(ang32)
    o_ref[0, :, 384:512] = jnp.sin(ang32) * sign32


def _rope_call(positions):
    bsz, s = positions.shape
    ts = min(512, s)
    inv64 = 1.0 / (ROPE_THETA ** (jnp.arange(0, HEAD, 2, dtype=F32) / HEAD))
    inv32 = 1.0 / (ROPE_THETA ** (jnp.arange(0, B_ROPE, 2, dtype=F32) / B_ROPE))
    z = jnp.zeros((32,), F32)
    f = jnp.stack([jnp.tile(inv64, 4),
                   jnp.concatenate([z, z, inv32, inv32, z])])
    pos = positions.astype(F32).reshape(bsz, s, 1)
    return pl.pallas_call(
        _rope_kernel,
        grid=(bsz, s // ts),
        in_specs=[pl.BlockSpec((1, ts, 1), lambda b, i: (b, i, 0)),
                  pl.BlockSpec((2, LANES), lambda b, i: (0, 0))],
        out_specs=pl.BlockSpec((1, ts, 512), lambda b, i: (b, i, 0)),
        out_shape=jax.ShapeDtypeStruct((bsz, s, 512), F32),
        compiler_params=_cparams(2),
        name="rope_tables",
    )(pos, f)


def _modulated_norm(h, g, scale, shift):
    y = h * lax.rsqrt(jnp.mean(h * h, axis=-1, keepdims=True) + EPS)
    return (y * g) * (1.0 + scale) + shift


def _ffn_kernel(h_ref, mod_ref, g_ref, wi_ref, wo_ref, o_ref, act_ref, *, mod_row, d_ff, chunk):
    h = h_ref[0]
    u = _modulated_norm(h, g_ref[...], mod_ref[0, mod_row + 1:mod_row + 2, :],
                        mod_ref[0, mod_row:mod_row + 1, :]).astype(BF16)
    for c in range(0, d_ff, chunk):
        gate = _dot(u, wi_ref[:, c:c + chunk])
        up = _dot(u, wi_ref[:, d_ff + c:d_ff + c + chunk])
        act_ref[:, c:c + chunk] = (_silu(gate) * up).astype(BF16)
    y = _dot(act_ref[...], wo_ref[...])
    o_ref[0] = h + (0.5 * mod_ref[0, mod_row + 2:mod_row + 3, :]) * y


def _ffn_call(h, mod, g, w_in, w_out, mod_row):
    bsz, s, d = h.shape
    d_ff = w_out.shape[0]
    ts = min(512, s)
    chunk = 256
    kern = functools.partial(_ffn_kernel, mod_row=mod_row, d_ff=d_ff, chunk=chunk)
    return pl.pallas_call(
        kern,
        grid=(bsz, s // ts),
        in_specs=[pl.BlockSpec((1, ts, d), lambda b, i: (b, i, 0)),
                  pl.BlockSpec((1, N_MOD, d), lambda b, i: (b, 0, 0)),
                  pl.BlockSpec((1, d), lambda b, i: (0, 0)),
                  pl.BlockSpec((d, 2 * d_ff), lambda b, i: (0, 0), pipeline_mode=pl.Buffered(1)),
                  pl.BlockSpec((d_ff, d), lambda b, i: (0, 0), pipeline_mode=pl.Buffered(1))],
        out_specs=pl.BlockSpec((1, ts, d), lambda b, i: (b, i, 0)),
        out_shape=jax.ShapeDtypeStruct(h.shape, F32),
        scratch_shapes=[pltpu.VMEM((ts, d_ff), BF16)],
        compiler_params=_cparams(2),
        name="ffn",
    )(h, mod, g, w_in, w_out)


def _group_mean_sq(x, bd_ref):
    w = x.shape[-1]
    sq = x * x
    hi = sq.astype(BF16)
    lo = (sq - hi.astype(F32)).astype(BF16)
    pieces = []
    step = 256 if w % 256 == 0 else 128
    for c in range(0, w, step):
        bd = bd_ref[0:step, 0:step]
        pieces.append(_dot(hi[:, c:c + step], bd) + _dot(lo[:, c:c + step], bd))
    return pieces[0] if len(pieces) == 1 else jnp.concatenate(pieces, axis=-1)


def _group_norm(x, bd_ref, gain):
    return x * lax.rsqrt(_group_mean_sq(x, bd_ref) + EPS) * gain


def _rope_slab(x, cos, sin, first_half, fwd, bwd):
    rot = jnp.where(first_half, pltpu.roll(x, fwd, axis=1), pltpu.roll(x, bwd, axis=1))
    return x * cos + rot * sin


def _rope64(x, cos, sin):
    lane = lax.broadcasted_iota(jnp.int32, (1, LANES), 1)
    first = (lane & 63) < 32
    out = [_rope_slab(x[:, c:c + LANES], cos, sin, first, 96, 32) for c in range(0, x.shape[-1], LANES)]
    return out[0] if len(out) == 1 else jnp.concatenate(out, axis=-1)


def _rope32(x, cos, sin):
    lane = lax.broadcasted_iota(jnp.int32, (1, LANES), 1)
    first = lane < 80
    out = [_rope_slab(x[:, c:c + LANES], cos, sin, first, 112, 16) for c in range(0, x.shape[-1], LANES)]
    return out[0] if len(out) == 1 else jnp.concatenate(out, axis=-1)


def _with_ones(v):
    lane = lax.broadcasted_iota(jnp.int32, (1, LANES), 1)
    out = []
    for c in range(0, v.shape[-1], LANES):
        slab = v[:, c:c + LANES]
        out += [jnp.where(lane < HEAD, slab, 1.0), jnp.where(lane < HEAD, 1.0, slab)]
    return jnp.concatenate(out, axis=-1).astype(BF16)


def _prep_kernel(h_ref, mod_ref, g_ref, w_ref, wuq_ref, wukv_ref, gv_ref, bd64_ref, bdq_ref, tab_ref,
                 qa_ref, ka_ref, va_ref, qb_ref, kb_ref, vb_ref, qc_ref, qcr_ref,
                 kc_ref, vc_ref, ks_ref, vs_ref, kw_ref, vw_ref, gf_ref):
    u = _modulated_norm(h_ref[0], g_ref[...], mod_ref[0, 4:5, :], mod_ref[0, 3:4, :]).astype(BF16)
    cos64, sin64 = tab_ref[0, :, 0:128], tab_ref[0, :, 128:256]
    cos32, sin32 = tab_ref[0, :, 256:384], tab_ref[0, :, 384:512]

    proj_all = _dot(u, w_ref[...])

    def proj(a, b):
        return proj_all[:, a:b]

    qa = _rope64(_group_norm(proj(O_AQ, O_AK), bd64_ref, gv_ref[0:1, 0:256]), cos64, sin64)
    qa_ref[0] = (qa * 0.125).astype(BF16)
    ka = _rope64(_group_norm(proj(O_AK, O_AV), bd64_ref, gv_ref[1:2, 0:128]), cos64, sin64)
    ka_ref[0] = ka.astype(BF16)
    va_ref[0] = _with_ones(proj(O_AV, O_BCQ))

    cq = proj(O_BCQ, O_BCKV)
    cq = cq * lax.rsqrt(jnp.mean(cq * cq, axis=-1, keepdims=True) + EPS) * gv_ref[8:9, 0:256]
    qb = _group_norm(_dot(cq.astype(BF16), wuq_ref[...]), bdq_ref, gv_ref[5:6, :])
    qb_ref[0] = _rope32(qb, cos32, sin32).astype(BF16)
    ckv = proj(O_BCKV, O_KR)
    ckv = (ckv * lax.rsqrt(jnp.mean(ckv * ckv, axis=-1, keepdims=True) + EPS) * gv_ref[9:10, 0:128]).astype(BF16)
    k_nope = _group_norm(_dot(ckv, wukv_ref[:, 0:512]), bd64_ref, gv_ref[6:7, :])
    vb_ref[0] = _with_ones(_dot(ckv, wukv_ref[:, 512:768]))
    k_pe = _rope32(_group_norm(proj(O_KR, O_CQ), bdq_ref, gv_ref[7:8, 0:128]), cos32, sin32)
    for hh in range(B_HEADS):
        kb_ref[0, :, hh * LANES:(hh + 1) * LANES] = (k_nope[:, hh * LANES:(hh + 1) * LANES] + k_pe).astype(BF16)

    qc = _group_norm(proj(O_CQ, O_KC), bd64_ref, gv_ref[2:3, :]) * 0.125
    qc_ref[0] = qc.astype(BF16)
    qcr_ref[0] = _rope64(qc, cos64, sin64).astype(BF16)
    kc_ref[0] = proj(O_KC, O_VC)
    vc_ref[0] = proj(O_VC, O_KS)
    ks = _rope64(_group_norm(proj(O_KS, O_VS), bd64_ref, gv_ref[3:4, 0:128]), cos64, sin64)
    ks_ref[0] = ks.astype(BF16)
    vs_ref[0] = _with_ones(proj(O_VS, O_KW))
    kw = _rope64(_group_norm(proj(O_KW, O_VW), bd64_ref, gv_ref[4:5, 0:128]), cos64, sin64)
    kw_ref[0] = kw.astype(BF16)
    vw_ref[0] = _with_ones(proj(O_VW, O_GF))
    gf_ref[0] = proj(O_GF, W_PROJ)


def _prep_call(h, mod, g, w, wuq, wukv, gv, bd64, bdq, tab):
    bsz, s, d = h.shape
    ts = min(512, s)
    widths = [(256, BF16), (128, BF16), (256, BF16), (512, BF16), (512, BF16), (512, BF16),
              (512, BF16), (512, BF16), (128, F32), (128, F32), (128, BF16), (256, BF16),
              (128, BF16), (256, BF16), (128, F32)]

    def const(shape):
        return pl.BlockSpec(shape, lambda b, i: (0,) * len(shape))

    return pl.pallas_call(
        _prep_kernel,
        grid=(bsz, s // ts),
        in_specs=[pl.BlockSpec((1, ts, d), lambda b, i: (b, i, 0)),
                  pl.BlockSpec((1, N_MOD, d), lambda b, i: (b, 0, 0)),
                  const((1, d)), const(w.shape), const(wuq.shape), const(wukv.shape),
                  const(gv.shape), const(bd64.shape), const(bdq.shape),
                  pl.BlockSpec((1, ts, 512), lambda b, i: (b, i, 0))],
        out_specs=[pl.BlockSpec((1, ts, wd), lambda b, i: (b, i, 0)) for wd, _ in widths],
        out_shape=[jax.ShapeDtypeStruct((bsz, s, wd), dt) for wd, dt in widths],
        compiler_params=_cparams(2),
        name="mixer_proj",
    )(h, mod, g, w, wuq, wukv, gv, bd64, bdq, tab)


def _compress_kernel(kc_ref, vc_ref, pe_ref, w1_ref, w2_ref, bd64_ref, g_ref, ko_ref, vo_ref):
    n = kc_ref.shape[1]

    def one(z_ref, idx):
        z = z_ref[0]
        top = _dot((z + pe_ref[idx, 0:1, :]).astype(BF16), w1_ref[idx, 0])
        bot = _dot((z + pe_ref[idx, 1:2, :]).astype(BF16), w1_ref[idx, 1])
        pre = top + pltpu.roll(bot, n - 1, axis=0)
        return _dot(_silu(pre).astype(BF16), w2_ref[idx])

    ko_ref[0] = _group_norm(one(kc_ref, 0), bd64_ref, g_ref[...]).astype(BF16)
    vo_ref[0] = one(vc_ref, 1).astype(BF16)


def _compress_call(kc, vc, pe, w1, w2, bd64, g):
    bsz, n, wz = kc.shape

    def const(shape):
        return pl.BlockSpec(shape, lambda b: (0,) * len(shape))

    return pl.pallas_call(
        _compress_kernel,
        grid=(bsz,),
        in_specs=[pl.BlockSpec((1, n, wz), lambda b: (b, 0, 0)),
                  pl.BlockSpec((1, n, wz), lambda b: (b, 0, 0)),
                  const(pe.shape), const(w1.shape), const(w2.shape), const(bd64.shape), const(g.shape)],
        out_specs=[pl.BlockSpec((1, n, LANES), lambda b: (b, 0, 0))] * 2,
        out_shape=[jax.ShapeDtypeStruct((bsz, n, LANES), BF16)] * 2,
        compiler_params=_cparams(1),
        name="nsa_compress",
    )(kc, vc, pe, w1, w2, bd64, g)


def _window_kernel(*refs, window, tq, n_slab, n_group, use_sink):
    if use_sink:
        sink_ref, q_ref, k_ref, v_ref, o_ref = refs
    else:
        q_ref, k_ref, v_ref, o_ref = refs
    span = window + tq
    half_masks = _half_masks(BF16)
    lane = lax.broadcasted_iota(jnp.int32, (1, LANES), 1)
    n_sub = q_ref.shape[1] // tq
    for sub in range(n_sub):
        q0 = pl.program_id(1) * (n_sub * tq) + sub * tq
        qrows = slice(sub * tq, (sub + 1) * tq)
        start = pl.multiple_of(jnp.maximum(q0 - window, 0), LANES)
        kwin = k_ref[0, pl.ds(start, span), :]
        rel = (q0 + lax.broadcasted_iota(jnp.int32, (tq, span), 0)) - \
              (start + lax.broadcasted_iota(jnp.int32, (tq, span), 1))
        visible = (rel >= 0) & (rel < window)
        scs = [[_dot_t(q_ref[0, qrows, s * LANES:(s + 1) * LANES] * hm, kwin) for hm in half_masks]
               for s in range(n_slab)]
        for s in range(n_slab):
            outs = []
            for kv in range(2):
                sc = jnp.where(visible, scs[s][kv], NEG_INF)
                m = jnp.max(sc, axis=-1, keepdims=True)
                if use_sink:
                    sink = sink_ref[kv * n_group + s]
                    m = jnp.maximum(m, sink)
                p = jnp.exp2((sc - m) * LOG2E)
                acc = _dot(p.astype(BF16), v_ref[0, pl.ds(start, span), kv * LANES:(kv + 1) * LANES])
                denom = jnp.sum(jnp.where(lane == (1 - kv) * HEAD, acc, 0.0), axis=-1, keepdims=True)
                if use_sink:
                    denom = denom + jnp.exp2((sink - m) * LOG2E)
                outs.append(acc / denom)
            o_ref[0, qrows, s * LANES:(s + 1) * LANES] = \
                jnp.where(lane < HEAD, outs[0], outs[1]).astype(o_ref.dtype)


def _window_call(q, k, v, sinks, window, out_dtype, name):
    bsz, s, wq = q.shape
    tq = WINDOW_TQ
    tblk = min(WINDOW_BLOCK, s)
    n_slab = wq // LANES
    use_sink = sinks is not None
    kern = functools.partial(_window_kernel, window=window, tq=tq, n_slab=n_slab, n_group=n_slab,
                             use_sink=use_sink)
    in_specs = [pl.BlockSpec((1, tblk, wq), lambda b, i: (b, i, 0)),
                pl.BlockSpec((1, s, LANES), lambda b, i: (b, 0, 0)),
                pl.BlockSpec((1, s, 2 * LANES), lambda b, i: (b, 0, 0))]
    args = [q, k, v]
    if use_sink:
        in_specs = [pl.BlockSpec(memory_space=pltpu.SMEM)] + in_specs
        args = [sinks] + args
    return pl.pallas_call(
        kern,
        grid=(bsz, s // tblk),
        in_specs=in_specs,
        out_specs=pl.BlockSpec((1, tblk, wq), lambda b, i: (b, i, 0)),
        out_shape=jax.ShapeDtypeStruct((bsz, s, wq), out_dtype),
        compiler_params=_cparams(2),
        name=name,
    )(*args)


LOG2E = 1.4426950408889634


FLASH_TQ, FLASH_TK = 256, 512
WINDOW_TQ, WINDOW_BLOCK = 128, 512


def _softmax_state(n_heads, tq):
    return ((jnp.full((tq, 1), NEG_INF, F32), jnp.zeros((tq, LANES), F32)),) * n_heads


def _online_step(state, sc, vt, c):
    m_prev, acc = state
    m_new = jnp.maximum(m_prev, jnp.max(sc, axis=-1, keepdims=True))
    alpha = jnp.exp2((m_prev - m_new) * c)
    p = jnp.exp2((sc - m_new) * c)
    return m_new, alpha * acc + _dot(p.astype(BF16), vt)


def _normalised(acc, value_half):
    lane = lax.broadcasted_iota(jnp.int32, (1, LANES), 1)
    denom_lane = (1 - value_half) * HEAD
    denom = jnp.sum(jnp.where(lane == denom_lane, acc, 0.0), axis=-1, keepdims=True)
    return acc / denom


def _causal_tile(q0, k0, tq, tk):
    return (k0 + lax.broadcasted_iota(jnp.int32, (tq, tk), 1)) <= (q0 + lax.broadcasted_iota(jnp.int32, (tq, tk), 0))


def _mla_kernel(q_ref, k_ref, v_ref, o_ref, *, tq, tk, c):
    q0 = pl.program_id(1) * tq
    n_full = q0 // tk
    lane = lax.broadcasted_iota(jnp.int32, (1, LANES), 1)

    def step(j, states, diagonal):
        rows = pl.ds(pl.multiple_of(j * tk, tk), tk)
        if diagonal:
            causal = _causal_tile(q0, j * tk, tq, tk)
        scs = [_dot_t(q_ref[0, :, hh * LANES:(hh + 1) * LANES], k_ref[0, rows, hh * LANES:(hh + 1) * LANES])
               for hh in range(B_HEADS)]
        out = []
        for hh in range(B_HEADS):
            sc = jnp.where(causal, scs[hh], NEG_INF) if diagonal else scs[hh]
            out.append(_online_step(states[hh], sc, v_ref[0, rows, hh * LANES:(hh + 1) * LANES], c))
        return tuple(out)

    states = lax.fori_loop(0, n_full, lambda j, st: step(j, st, False), _softmax_state(B_HEADS, tq))
    states = step(n_full, states, True)
    for s in range(B_HEADS // 2):
        o_lo = _normalised(states[2 * s][1], 0)
        o_hi = _normalised(states[2 * s + 1][1], 1)
        o_ref[0, :, s * LANES:(s + 1) * LANES] = jnp.where(lane < HEAD, o_lo, o_hi).astype(o_ref.dtype)


def _mla_call(q, k, v):
    bsz, s, _ = q.shape
    tq, tk = FLASH_TQ, FLASH_TK
    kern = functools.partial(_mla_kernel, tq=tq, tk=tk, c=float((B_NOPE + B_ROPE) ** -0.5) * LOG2E)
    return pl.pallas_call(
        kern,
        grid=(bsz, s // tq),
        in_specs=[pl.BlockSpec((1, tq, 512), lambda b, i: (b, i, 0)),
                  pl.BlockSpec((1, s, 512), lambda b, i: (b, 0, 0)),
                  pl.BlockSpec((1, s, 512), lambda b, i: (b, 0, 0))],
        out_specs=pl.BlockSpec((1, tq, 256), lambda b, i: (b, i, 0)),
        out_shape=jax.ShapeDtypeStruct((bsz, s, 256), BF16),
        compiler_params=_cparams(2),
        name="mla_attention",
    )(q, k, v)


def _cmp_select_kernel(q_ref, k_ref, v_ref, ov_ref, o_ref, sel_ref, *, tq):
    n = k_ref.shape[1]
    t = pl.program_id(1) * tq + lax.broadcasted_iota(jnp.int32, (tq, n), 0)
    j = lax.broadcasted_iota(jnp.int32, (tq, n), 1)
    visible = (j * C_CMP_STRIDE + (C_CMP_BLOCK - 1)) <= t
    kc, vc = k_ref[0], v_ref[0]
    m_lo, m_hi = _half_masks(BF16)
    lane = lax.broadcasted_iota(jnp.int32, (1, LANES), 1)
    p_sum = [jnp.zeros((tq, n), F32), jnp.zeros((tq, n), F32)]
    scs = [[_dot_t(q_ref[0, :, s * LANES:(s + 1) * LANES] * hm, kc) for hm in (m_lo, m_hi)]
           for s in range(C_HEADS // C_KV)]
    for s in range(C_HEADS // C_KV):
        outs = []
        for kv in range(C_KV):
            sc = jnp.where(visible, scs[s][kv], NEG_INF)
            m = jnp.max(sc, axis=-1, keepdims=True)
            p = jnp.where(visible, jnp.exp(sc - m), 0.0)
            denom = jnp.sum(p, axis=-1, keepdims=True)
            p = p * jnp.where(denom > 0.0, 1.0 / denom, 0.0)
            p_sum[kv] = p_sum[kv] + p
            outs.append(_dot(p.astype(BF16), vc))
        o_ref[0, :, s * LANES:(s + 1) * LANES] = jnp.where(lane < HEAD, outs[0], outs[1])

    imp = jnp.zeros((tq, LANES), F32)
    for kv in range(C_KV):
        hi = p_sum[kv].astype(BF16)
        lo = (p_sum[kv] - hi.astype(F32)).astype(BF16)
        imp = imp + _dot(hi, ov_ref[kv]) + _dot(lo, ov_ref[kv])
    tl = pl.program_id(1) * tq + lax.broadcasted_iota(jnp.int32, (tq, LANES), 0)
    lane_f = lax.broadcasted_iota(jnp.int32, (tq, LANES), 1)
    blk = lane_f & (HEAD - 1)
    cur = tl >> 6
    forced = (blk == 0) | (blk == cur) | (blk == cur - 1)
    future = blk > cur
    score = jnp.where(future, NEG_INF, jnp.where(forced, FORCE_SCORE, imp))
    n_forced = 3
    sel = jnp.where(forced, 1.0, 0.0)
    score = jnp.where(forced, REMOVED, score)
    lane_id = lane_f.astype(F32)
    for kv in range(C_KV):
        in_group = (lane_f >= kv * HEAD) & (lane_f < (kv + 1) * HEAD)
        sc = jnp.where(in_group, score, REMOVED)
        for _ in range(C_N_SEL - n_forced):
            best = jnp.max(sc, axis=-1, keepdims=True)
            first = jnp.min(jnp.where(sc == best, lane_id, 2.0 * LANES), axis=-1, keepdims=True)
            hit = lane_id == first
            sel = jnp.where(hit, 1.0, sel)
            sc = jnp.where(hit, REMOVED, sc)
    sel_ref[0] = jnp.where(future, 0.0, sel).astype(BF16)


def _cmp_select_call(q, kcmp, vcmp, ov):
    bsz, s, wq = q.shape
    n = kcmp.shape[1]
    tq = min(256, s)
    kern = functools.partial(_cmp_select_kernel, tq=tq)
    return pl.pallas_call(
        kern,
        grid=(bsz, s // tq),
        in_specs=[pl.BlockSpec((1, tq, wq), lambda b, i: (b, i, 0)),
                  pl.BlockSpec((1, n, LANES), lambda b, i: (b, 0, 0)),
                  pl.BlockSpec((1, n, LANES), lambda b, i: (b, 0, 0)),
                  pl.BlockSpec(ov.shape, lambda b, i: (0, 0, 0))],
        out_specs=[pl.BlockSpec((1, tq, wq), lambda b, i: (b, i, 0)),
                   pl.BlockSpec((1, tq, LANES), lambda b, i: (b, i, 0))],
        out_shape=[jax.ShapeDtypeStruct((bsz, s, wq), F32),
                   jax.ShapeDtypeStruct((bsz, s, LANES), BF16)],
        compiler_params=_cparams(2),
        name="nsa_cmp_select",
    )(q, kcmp, vcmp, ov)


def _slc_kernel(q_ref, k_ref, v_ref, sel_ref, e_ref, o_ref, *, tq, tk):
    q0 = pl.program_id(1) * tq
    n_full = q0 // tk
    lane = lax.broadcasted_iota(jnp.int32, (1, LANES), 1)
    m_lo, m_hi = _half_masks(BF16)
    n_slab = C_HEADS // C_KV
    outs = []
    for kv, hm in enumerate((m_lo, m_hi)):
        sel_h = sel_ref[0] * hm
        qm = [q_ref[0, :, s * LANES:(s + 1) * LANES] * hm for s in range(n_slab)]

        def step(j, states, diagonal):
            rows = pl.ds(pl.multiple_of(j * tk, tk), tk)
            kt = k_ref[0, rows, :]
            vt = v_ref[0, rows, kv * LANES:(kv + 1) * LANES]
            chosen = _dot(sel_h, e_ref[j]) > 0.5
            if diagonal:
                chosen = chosen & _causal_tile(q0, j * tk, tq, tk)
            scs = [_dot_t(qm[s], kt) for s in range(n_slab)]
            return tuple(_online_step(states[s], jnp.where(chosen, scs[s], NEG_INF), vt, LOG2E)
                         for s in range(n_slab))

        states = lax.fori_loop(0, n_full, lambda j, st: step(j, st, False), _softmax_state(n_slab, tq))
        states = step(n_full, states, True)
        outs.append([_normalised(states[s][1], kv) for s in range(n_slab)])
    for s in range(n_slab):
        o_ref[0, :, s * LANES:(s + 1) * LANES] = jnp.where(lane < HEAD, outs[0][s], outs[1][s])


def _slc_call(q, k, v, sel, e):
    bsz, s, wq = q.shape
    tq, tk = FLASH_TQ, e.shape[2]
    kern = functools.partial(_slc_kernel, tq=tq, tk=tk)
    return pl.pallas_call(
        kern,
        grid=(bsz, s // tq),
        in_specs=[pl.BlockSpec((1, tq, wq), lambda b, i: (b, i, 0)),
                  pl.BlockSpec((1, s, LANES), lambda b, i: (b, 0, 0)),
                  pl.BlockSpec((1, s, 2 * LANES), lambda b, i: (b, 0, 0)),
                  pl.BlockSpec((1, tq, LANES), lambda b, i: (b, i, 0)),
                  pl.BlockSpec(e.shape, lambda b, i: (0, 0, 0))],
        out_specs=pl.BlockSpec((1, tq, wq), lambda b, i: (b, i, 0)),
        out_shape=jax.ShapeDtypeStruct((bsz, s, wq), F32),
        compiler_params=_cparams(2),
        name="nsa_selected",
    )(q, k, v, sel, e)


def _merge_kernel(h_ref, mod_ref, oa_ref, ob_ref, ocmp_ref, oslc_ref, owin_ref, gf_ref, wg_ref, bg_ref,
                  wo_ref, o_ref, cat_ref):
    wc = ocmp_ref.shape[2]
    ts = h_ref.shape[1]
    halves = [slice(r, r + ts // 2) for r in (0, ts // 2)]
    pre = [_dot(_silu(gf_ref[0, r, :]).astype(BF16), wg_ref[...]) for r in halves]
    for r, g in zip(halves, pre):
        gates = jax.nn.sigmoid(g + bg_ref[...])
        oc = (gates[:, 0:wc] * ocmp_ref[0, r, :] + gates[:, wc:2 * wc] * oslc_ref[0, r, :]
              + gates[:, 2 * wc:3 * wc] * owin_ref[0, r, :])
        cat_ref[r, 0:256] = oa_ref[0, r, :]
        cat_ref[r, 256:512] = ob_ref[0, r, :]
        cat_ref[r, 512:512 + wc] = oc.astype(BF16)
        o_ref[0, r, :] = h_ref[0, r, :] + mod_ref[0, 5:6, :] * _dot(cat_ref[r, :], wo_ref[...])


def _merge_call(h, mod, oa, ob, ocmp, oslc, owin, gf, wg, bg, wo):
    bsz, s, d = h.shape
    ts = min(512, s)

    def tok(wd):
        return pl.BlockSpec((1, ts, wd), lambda b, i: (b, i, 0))

    def const(shape):
        return pl.BlockSpec(shape, lambda b, i: (0,) * len(shape))

    return pl.pallas_call(
        _merge_kernel,
        grid=(bsz, s // ts),
        in_specs=[tok(d), pl.BlockSpec((1, N_MOD, d), lambda b, i: (b, 0, 0)),
                  tok(256), tok(256), tok(512), tok(512), tok(512), tok(LANES),
                  const(wg.shape), const(bg.shape), const(wo.shape)],
        out_specs=tok(d),
        out_shape=jax.ShapeDtypeStruct(h.shape, F32),
        scratch_shapes=[pltpu.VMEM((ts, wo.shape[0]), BF16)],
        compiler_params=_cparams(2),
        name="mixer_merge",
    )(h, mod, oa, ob, ocmp, oslc, owin, gf, wg, bg, wo)


def _cols(w, pieces):
    out = [jnp.zeros((w.shape[0], p), w.dtype) if isinstance(p, int) else w[:, p[0]:p[1]] for p in pieces]
    return jnp.concatenate(out, axis=1)


def _head_cols(base, order):
    return [(base + hd * HEAD, base + (hd + 1) * HEAD) for hd in order]


A_ORDER = (0, 2, 1, 3)
C_ORDER = (0, 4, 1, 5, 2, 6, 3, 7)


def _proj_weight(w):
    pieces = (_head_cols(0, A_ORDER) + [(256, 384), (384, 512), (512, 768), (768, 896)]
              + [64, (896, 928), 32] + _head_cols(928, C_ORDER)
              + [(1440 + 128 * i, 1568 + 128 * i) for i in range(6)] + [(2208, 2272), 64])
    return _cols(w, pieces).astype(BF16)


def _mla_weights(w_uq, w_ukv):
    dq = B_NOPE + B_ROPE
    uq = []
    for hd in range(B_HEADS):
        uq += [(hd * dq, hd * dq + dq), 32]
    dk = B_NOPE + B_V
    uk = []
    for hd in range(B_HEADS):
        uk += [(hd * dk, hd * dk + B_NOPE), 64]
    uv = [(hd * dk + B_NOPE, (hd + 1) * dk) for hd in range(B_HEADS)]
    return _cols(w_uq, uq).astype(BF16), _cols(w_ukv, uk + uv).astype(BF16)


def _tile_to(v, reps, width=512):
    t = jnp.tile(v, reps)
    return jnp.pad(t, (0, width - t.shape[0]))


def _gain_rows(a_qk_g, b_q_lat_g, b_kv_lat_g, b_qk_nope_g, b_qk_rope_g, c_qk_g):
    z32, z64 = jnp.zeros((32,), F32), jnp.zeros((64,), F32)
    rows = [_tile_to(a_qk_g[0], 4), _tile_to(a_qk_g[1], 2), _tile_to(c_qk_g[0], 8),
            _tile_to(c_qk_g[2], 2), _tile_to(c_qk_g[3], 2),
            jnp.tile(jnp.concatenate([b_qk_nope_g[0], b_qk_rope_g[0], z32]), 4),
            jnp.tile(jnp.concatenate([b_qk_nope_g[1], z64]), 4),
            _tile_to(jnp.concatenate([z64, b_qk_rope_g[1], z32]), 1),
            _tile_to(b_q_lat_g, 1), _tile_to(b_kv_lat_g, 1)]
    rows += [jnp.zeros((512,), F32)] * (16 - len(rows))
    return jnp.stack(rows)


def _block_diag_consts():
    i = np.arange(256)
    bd64 = np.where((i[:, None] // 64) == (i[None, :] // 64), 1.0 / 64, 0.0)
    j = i % 128
    grp = np.where(j < 64, 0, np.where(j < 96, 1, 2)) + 3 * (i // 128)
    size = np.where(j < 64, 64.0, 32.0)
    bdq = np.where(grp[:, None] == grp[None, :], 1.0 / size[None, :], 0.0)
    return jnp.asarray(bd64, BF16), jnp.asarray(bdq, BF16)


def _compress_weights(pe, w1, w2):
    n_tok = C_CMP_STRIDE
    pe_rows = jnp.broadcast_to(pe.reshape(2, 2, n_tok, 1, HEAD), (2, 2, n_tok, C_KV, HEAD))
    pe_rows = pe_rows.reshape(2, 2, n_tok * C_KV * HEAD)
    w1r = w1.reshape(2, 2, n_tok, HEAD, C_CMP_HID)
    eye = jnp.eye(C_KV, dtype=F32)
    w1big = jnp.einsum('xytdj,kq->xytkdqj', w1r, eye).reshape(2, 2, n_tok * C_KV * HEAD, C_KV * C_CMP_HID)
    w2big = jnp.einsum('xjd,kq->xkjqd', w2, eye).reshape(2, C_KV * C_CMP_HID, C_KV * HEAD)
    return pe_rows, w1big.astype(BF16), w2big.astype(BF16)


def _overlap_const(n_chunk):
    n_slc = n_chunk * C_CMP_STRIDE // C_SLC_BLOCK
    ci = np.arange(n_chunk)[:, None] * C_CMP_STRIDE
    sj = np.arange(n_slc)[None, :] * C_SLC_BLOCK
    ov = ((ci <= sj + C_SLC_BLOCK - 1) & (ci + C_CMP_BLOCK - 1 >= sj)).astype(np.float32)
    ov[n_chunk - 1] = 0.0
    out = np.zeros((C_KV, n_chunk, LANES), np.float32)
    for kv in range(C_KV):
        out[kv, :, kv * HEAD:kv * HEAD + n_slc] = ov
    return jnp.asarray(out, BF16)


def _expand_const(s, tk):
    r = np.arange(LANES)[None, :, None] % HEAD
    c = (np.arange(s // tk)[:, None, None] * tk + np.arange(tk)[None, None, :]) // C_SLC_BLOCK
    return jnp.asarray((r == c).astype(np.float32), BF16)


def _gate_weights(c_gate_w, c_gate_b):
    head_of_lane = np.repeat(np.asarray(C_ORDER), HEAD)
    cols = np.concatenate([head_of_lane * 3 + k for k in range(3)])
    wg = jnp.concatenate([c_gate_w[:, cols], jnp.zeros((LANES - C_GATE_HID, cols.size), F32)], axis=0)
    return wg.astype(BF16), c_gate_b[cols].reshape(1, -1)


def _out_weight(w_out):
    rows = ([(hd * HEAD, (hd + 1) * HEAD) for hd in A_ORDER] + [(256, 512)]
            + [(512 + hd * HEAD, 512 + (hd + 1) * HEAD) for hd in C_ORDER])
    return jnp.concatenate([w_out[a:b] for a, b in rows], axis=0).astype(BF16)


def kernel(x, c, positions, ada_w, ada_b, norm_g, ffn_w_in, ffn_w_out, w_in, w_out, a_sinks, a_qk_g,
           b_q_lat_g, b_kv_lat_g, b_w_uq, b_w_ukv, b_qk_nope_g, b_qk_rope_g, c_qk_g, c_cmp_pe, c_cmp_w1,
           c_cmp_w2, c_gate_w, c_gate_b):
    bsz, s, d = x.shape
    depth = ada_w.shape[0]
    n_chunk = s // C_CMP_STRIDE
    tq = min(256, s)
    assert s % tq == 0 and s % FLASH_TK == 0 and s >= C_WINDOW + tq and s // C_SLC_BLOCK <= HEAD

    mod_all = _ada_call(c, ada_w, ada_b).reshape(depth, bsz, N_MOD, d)
    tab = _rope_call(positions)
    bd64, bdq = _block_diag_consts()
    ov = _overlap_const(n_chunk)
    e = _expand_const(s, FLASH_TK)

    h = x
    for l in range(depth):
        mod = mod_all[l]
        h = _ffn_call(h, mod, norm_g[l, 0:1], ffn_w_in[l, 0].astype(BF16), ffn_w_out[l, 0].astype(BF16), 0)

        wuq, wukv = _mla_weights(b_w_uq[l], b_w_ukv[l])
        gv = _gain_rows(a_qk_g[l], b_q_lat_g[l], b_kv_lat_g[l], b_qk_nope_g[l], b_qk_rope_g[l], c_qk_g[l])
        (qa, ka, va, qb, kb, vb, qc, qcr, kc, vc, ks, vs, kw, vw, gf) = _prep_call(
            h, mod, norm_g[l, 1:2], _proj_weight(w_in[l]), wuq, wukv, gv, bd64, bdq, tab)

        pe_rows, w1big, w2big = _compress_weights(c_cmp_pe[l], c_cmp_w1[l], c_cmp_w2[l])
        kcmp, vcmp = _compress_call(kc.reshape(bsz, n_chunk, C_CMP_STRIDE * LANES),
                                    vc.reshape(bsz, n_chunk, C_CMP_STRIDE * LANES),
                                    pe_rows, w1big, w2big, bd64, jnp.tile(c_qk_g[l, 1], 2).reshape(1, LANES))

        oa = _window_call(qa, ka, va, a_sinks[l], A_WINDOW, BF16, "swa_attention")
        ob = _mla_call(qb, kb, vb)
        ocmp, sel = _cmp_select_call(qc, kcmp, vcmp, ov)
        oslc = _slc_call(qcr, ks, vs, sel, e)
        owin = _window_call(qcr, kw, vw, None, C_WINDOW, F32, "nsa_window")

        wg, bg = _gate_weights(c_gate_w[l], c_gate_b[l])
        h = _merge_call(h, mod, oa, ob, ocmp, oslc, owin, gf, wg, bg, _out_weight(w_out[l]))

        h = _ffn_call(h, mod, norm_g[l, 2:3], ffn_w_in[l, 1].astype(BF16), ffn_w_out[l, 1].astype(BF16), 6)
    return h
```

```python
import functools

import numpy as np
import jax
import jax.numpy as jnp
from jax import lax
from jax.experimental import pallas as pl
from jax.experimental.pallas import tpu as pltpu

F32 = jnp.float32
BF16 = jnp.bfloat16

LANES = 128
HEAD = 64
EPS = 1e-6
ROPE_THETA = 10000.0
NEG_INF = -1e30
FORCE_SCORE = 1e4
REMOVED = -3e38
N_MOD = 9

A_HEADS, A_KV, A_WINDOW = 4, 2, 128
B_HEADS, B_Q_LORA, B_KV_LORA, B_NOPE, B_ROPE, B_V = 4, 256, 128, 64, 32, 64
C_HEADS, C_KV, C_WINDOW = 8, 2, 256
C_CMP_BLOCK, C_CMP_STRIDE, C_SLC_BLOCK, C_N_SEL = 32, 16, 64, 8
C_CMP_HID, C_GATE_HID = 128, 64

VMEM_LIMIT = 56 * 1024 * 1024

O_AQ, O_AK, O_AV, O_BCQ, O_BCKV, O_KR, O_CQ = 0, 256, 384, 512, 768, 896, 1024
O_KC, O_VC, O_KS, O_VS, O_KW, O_VW, O_GF, W_PROJ = 1536, 1664, 1792, 1920, 2048, 2176, 2304, 2432


def _cparams(n_grid):
    return pltpu.CompilerParams(dimension_semantics=("arbitrary",) * n_grid,
                                vmem_limit_bytes=VMEM_LIMIT)


def _dot(a, b):
    return jnp.dot(a, b, preferred_element_type=F32)


def _dot_t(a, b):
    return lax.dot_general(a, b, (((1,), (1,)), ((), ())), preferred_element_type=F32)


def _silu(x):
    return x * jax.nn.sigmoid(x)


def _half_masks(dtype):
    lane = lax.broadcasted_iota(jnp.int32, (1, LANES), 1)
    lo = jnp.where(lane < HEAD, 1.0, 0.0).astype(dtype)
    hi = jnp.where(lane >= HEAD, 1.0, 0.0).astype(dtype)
    return lo, hi


def _ada_kernel(c_ref, w_ref, b_ref, o_ref):
    cond = _silu(c_ref[...]).astype(BF16)
    o_ref[0] = _dot(cond, w_ref[0].astype(BF16)) + b_ref[0]


def _ada_call(c, ada_w, ada_b):
    depth, d, nm = ada_w.shape
    bsz = c.shape[0]
    tn = 1024
    return pl.pallas_call(
        _ada_kernel,
        grid=(depth, nm // tn),
        in_specs=[pl.BlockSpec((bsz, d), lambda l, j: (0, 0)),
                  pl.BlockSpec((1, d, tn), lambda l, j: (l, 0, j)),
                  pl.BlockSpec((1, 1, tn), lambda l, j: (l, 0, j))],
        out_specs=pl.BlockSpec((1, bsz, tn), lambda l, j: (l, 0, j)),
        out_shape=jax.ShapeDtypeStruct((depth, bsz, nm), F32),
        compiler_params=_cparams(2),
        name="adaln",
    )(c, ada_w, ada_b.reshape(depth, 1, nm))


def _rope_kernel(pos_ref, f_ref, o_ref):
    pos = pos_ref[0]
    lane = lax.broadcasted_iota(jnp.int32, (1, LANES), 1)
    ang64 = pos * f_ref[0:1, :]
    sign64 = jnp.where((lane & 63) < 32, -1.0, 1.0)
    o_ref[0, :, 0:128] = jnp.cos(ang64)
    o_ref[0, :, 128:256] = jnp.sin(ang64) * sign64
    ang32 = pos * f_ref[1:2, :]
    sign32 = jnp.where(lane < 80, -1.0, 1.0)
    o_ref[0, :, 256:384] = jnp.cos(ang32)
    o_ref[0, :, 384:512] = jnp.sin(ang32) * sign32


def _rope_call(positions):
    bsz, s = positions.shape
    ts = min(512, s)
    inv64 = 1.0 / (ROPE_THETA ** (jnp.arange(0, HEAD, 2, dtype=F32) / HEAD))
    inv32 = 1.0 / (ROPE_THETA ** (jnp.arange(0, B_ROPE, 2, dtype=F32) / B_ROPE))
    z = jnp.zeros((32,), F32)
    f = jnp.stack([jnp.tile(inv64, 4),
                   jnp.concatenate([z, z, inv32, inv32, z])])
    pos = positions.astype(F32).reshape(bsz, s, 1)
    return pl.pallas_call(
        _rope_kernel,
        grid=(bsz, s // ts),
        in_specs=[pl.BlockSpec((1, ts, 1), lambda b, i: (b, i, 0)),
                  pl.BlockSpec((2, LANES), lambda b, i: (0, 0))],
        out_specs=pl.BlockSpec((1, ts, 512), lambda b, i: (b, i, 0)),
        out_shape=jax.ShapeDtypeStruct((bsz, s, 512), F32),
        compiler_params=_cparams(2),
        name="rope_tables",
    )(pos, f)


def _modulated_norm(h, g, scale, shift):
    y = h * lax.rsqrt(jnp.mean(h * h, axis=-1, keepdims=True) + EPS)
    return (y * g) * (1.0 + scale) + shift


def _ffn_kernel(h_ref, mod_ref, g_ref, wi_ref, wo_ref, o_ref, act_ref, *, mod_row, d_ff, chunk):
    h = h_ref[0]
    u = _modulated_norm(h, g_ref[...], mod_ref[0, mod_row + 1:mod_row + 2, :],
                        mod_ref[0, mod_row:mod_row + 1, :]).astype(BF16)
    for c in range(0, d_ff, chunk):
        gate = _dot(u, wi_ref[:, c:c + chunk])
        up = _dot(u, wi_ref[:, d_ff + c:d_ff + c + chunk])
        act_ref[:, c:c + chunk] = (_silu(gate) * up).astype(BF16)
    y = _dot(act_ref[...], wo_ref[...])
    o_ref[0] = h + (0.5 * mod_ref[0, mod_row + 2:mod_row + 3, :]) * y


def _ffn_call(h, mod, g, w_in, w_out, mod_row):
    bsz, s, d = h.shape
    d_ff = w_out.shape[0]
    ts = min(512, s)
    chunk = 256
    kern = functools.partial(_ffn_kernel, mod_row=mod_row, d_ff=d_ff, chunk=chunk)
    return pl.pallas_call(
        kern,
        grid=(bsz, s // ts),
        in_specs=[pl.BlockSpec((1, ts, d), lambda b, i: (b, i, 0)),
                  pl.BlockSpec((1, N_MOD, d), lambda b, i: (b, 0, 0)),
                  pl.BlockSpec((1, d), lambda b, i: (0, 0)),
                  pl.BlockSpec((d, 2 * d_ff), lambda b, i: (0, 0), pipeline_mode=pl.Buffered(1)),
                  pl.BlockSpec((d_ff, d), lambda b, i: (0, 0), pipeline_mode=pl.Buffered(1))],
        out_specs=pl.BlockSpec((1, ts, d), lambda b, i: (b, i, 0)),
        out_shape=jax.ShapeDtypeStruct(h.shape, F32),
        scratch_shapes=[pltpu.VMEM((ts, d_ff), BF16)],
        compiler_params=_cparams(2),
        name="ffn",
    )(h, mod, g, w_in, w_out)


def _group_mean_sq(x, bd_ref):
    w = x.shape[-1]
    sq = x * x
    hi = sq.astype(BF16)
    lo = (sq - hi.astype(F32)).astype(BF16)
    pieces = []
    step = 256 if w % 256 == 0 else 128
    for c in range(0, w, step):
        bd = bd_ref[0:step, 0:step]
        pieces.append(_dot(hi[:, c:c + step], bd) + _dot(lo[:, c:c + step], bd))
    return pieces[0] if len(pieces) == 1 else jnp.concatenate(pieces, axis=-1)


def _group_norm(x, bd_ref, gain):
    return x * lax.rsqrt(_group_mean_sq(x, bd_ref) + EPS) * gain


def _rope_slab(x, cos, sin, first_half, fwd, bwd):
    rot = jnp.where(first_half, pltpu.roll(x, fwd, axis=1), pltpu.roll(x, bwd, axis=1))
    return x * cos + rot * sin


def _rope64(x, cos, sin):
    lane = lax.broadcasted_iota(jnp.int32, (1, LANES), 1)
    first = (lane & 63) < 32
    out = [_rope_slab(x[:, c:c + LANES], cos, sin, first, 96, 32) for c in range(0, x.shape[-1], LANES)]
    return out[0] if len(out) == 1 else jnp.concatenate(out, axis=-1)


def _rope32(x, cos, sin):
    lane = lax.broadcasted_iota(jnp.int32, (1, LANES), 1)
    first = lane < 80
    out = [_rope_slab(x[:, c:c + LANES], cos, sin, first, 112, 16) for c in range(0, x.shape[-1], LANES)]
    return out[0] if len(out) == 1 else jnp.concatenate(out, axis=-1)


def _with_ones(v):
    lane = lax.broadcasted_iota(jnp.int32, (1, LANES), 1)
    out = []
    for c in range(0, v.shape[-1], LANES):
        slab = v[:, c:c + LANES]
        out += [jnp.where(lane < HEAD, slab, 1.0), jnp.where(lane < HEAD, 1.0, slab)]
    return jnp.concatenate(out, axis=-1).astype(BF16)


def _prep_kernel(h_ref, mod_ref, g_ref, w_ref, wuq_ref, wukv_ref, gv_ref, bd64_ref, bdq_ref, tab_ref,
                 qa_ref, ka_ref, va_ref, qb_ref, kb_ref, vb_ref, qc_ref, qcr_ref,
                 kc_ref, vc_ref, ks_ref, vs_ref, kw_ref, vw_ref, gf_ref):
    u = _modulated_norm(h_ref[0], g_ref[...], mod_ref[0, 4:5, :], mod_ref[0, 3:4, :]).astype(BF16)
    cos64, sin64 = tab_ref[0, :, 0:128], tab_ref[0, :, 128:256]
    cos32, sin32 = tab_ref[0, :, 256:384], tab_ref[0, :, 384:512]

    proj_all = _dot(u, w_ref[...])

    def proj(a, b):
        return proj_all[:, a:b]

    qa = _rope64(_group_norm(proj(O_AQ, O_AK), bd64_ref, gv_ref[0:1, 0:256]), cos64, sin64)
    qa_ref[0] = (qa * 0.125).astype(BF16)
    ka = _rope64(_group_norm(proj(O_AK, O_AV), bd64_ref, gv_ref[1:2, 0:128]), cos64, sin64)
    ka_ref[0] = ka.astype(BF16)
    va_ref[0] = _with_ones(proj(O_AV, O_BCQ))

    cq = proj(O_BCQ, O_BCKV)
    cq = cq * lax.rsqrt(jnp.mean(cq * cq, axis=-1, keepdims=True) + EPS) * gv_ref[8:9, 0:256]
    qb = _group_norm(_dot(cq.astype(BF16), wuq_ref[...]), bdq_ref, gv_ref[5:6, :])
    qb_ref[0] = _rope32(qb, cos32, sin32).astype(BF16)
    ckv = proj(O_BCKV, O_KR)
    ckv = (ckv * lax.rsqrt(jnp.mean(ckv * ckv, axis=-1, keepdims=True) + EPS) * gv_ref[9:10, 0:128]).astype(BF16)
    k_nope = _group_norm(_dot(ckv, wukv_ref[:, 0:512]), bd64_ref, gv_ref[6:7, :])
    vb_ref[0] = _with_ones(_dot(ckv, wukv_ref[:, 512:768]))
    k_pe = _rope32(_group_norm(proj(O_KR, O_CQ), bdq_ref, gv_ref[7:8, 0:128]), cos32, sin32)
    for hh in range(B_HEADS):
        kb_ref[0, :, hh * LANES:(hh + 1) * LANES] = (k_nope[:, hh * LANES:(hh + 1) * LANES] + k_pe).astype(BF16)

    qc = _group_norm(proj(O_CQ, O_KC), bd64_ref, gv_ref[2:3, :]) * 0.125
    qc_ref[0] = qc.astype(BF16)
    qcr_ref[0] = _rope64(qc, cos64, sin64).astype(BF16)
    kc_ref[0] = proj(O_KC, O_VC)
    vc_ref[0] = proj(O_VC, O_KS)
    ks = _rope64(_group_norm(proj(O_KS, O_VS), bd64_ref, gv_ref[3:4, 0:128]), cos64, sin64)
    ks_ref[0] = ks.astype(BF16)
    vs_ref[0] = _with_ones(proj(O_VS, O_KW))
    kw = _rope64(_group_norm(proj(O_KW, O_VW), bd64_ref, gv_ref[4:5, 0:128]), cos64, sin64)
    kw_ref[0] = kw.astype(BF16)
    vw_ref[0] = _with_ones(proj(O_VW, O_GF))
    gf_ref[0] = proj(O_GF, W_PROJ)


def _prep_call(h, mod, g, w, wuq, wukv, gv, bd64, bdq, tab):
    bsz, s, d = h.shape
    ts = min(512, s)
    widths = [(256, BF16), (128, BF16), (256, BF16), (512, BF16), (512, BF16), (512, BF16),
              (512, BF16), (512, BF16), (128, F32), (128, F32), (128, BF16), (256, BF16),
              (128, BF16), (256, BF16), (128, F32)]

    def const(shape):
        return pl.BlockSpec(shape, lambda b, i: (0,) * len(shape))

    return pl.pallas_call(
        _prep_kernel,
        grid=(bsz, s // ts),
        in_specs=[pl.BlockSpec((1, ts, d), lambda b, i: (b, i, 0)),
                  pl.BlockSpec((1, N_MOD, d), lambda b, i: (b, 0, 0)),
                  const((1, d)), const(w.shape), const(wuq.shape), const(wukv.shape),
                  const(gv.shape), const(bd64.shape), const(bdq.shape),
                  pl.BlockSpec((1, ts, 512), lambda b, i: (b, i, 0))],
        out_specs=[pl.BlockSpec((1, ts, wd), lambda b, i: (b, i, 0)) for wd, _ in widths],
        out_shape=[jax.ShapeDtypeStruct((bsz, s, wd), dt) for wd, dt in widths],
        compiler_params=_cparams(2),
        name="mixer_proj",
    )(h, mod, g, w, wuq, wukv, gv, bd64, bdq, tab)


def _compress_kernel(kc_ref, vc_ref, pe_ref, w1_ref, w2_ref, bd64_ref, g_ref, ko_ref, vo_ref):
    n = kc_ref.shape[1] // C_CMP_STRIDE

    def one(z_ref, idx):
        top = jnp.zeros((n, 2 * C_CMP_HID), F32)
        bot = jnp.zeros((n, 2 * C_CMP_HID), F32)
        for t in range(C_CMP_STRIDE):
            z = z_ref[0, pl.ds(t, n, stride=C_CMP_STRIDE), :]
            cols = slice(t * LANES, (t + 1) * LANES)
            top = top + _dot((z + pe_ref[idx, 0:1, cols]).astype(BF16), w1_ref[idx, 0, cols, :])
            bot = bot + _dot((z + pe_ref[idx, 1:2, cols]).astype(BF16), w1_ref[idx, 1, cols, :])
        pre = top + pltpu.roll(bot, n - 1, axis=0)
        return _dot(_silu(pre).astype(BF16), w2_ref[idx])

    ko_ref[0] = _group_norm(one(kc_ref, 0), bd64_ref, g_ref[...]).astype(BF16)
    vo_ref[0] = one(vc_ref, 1).astype(BF16)


def _compress_call(kc, vc, pe, w1, w2, bd64, g):
    bsz, s, wz = kc.shape
    n = s // C_CMP_STRIDE

    def const(shape):
        return pl.BlockSpec(shape, lambda b: (0,) * len(shape))

    return pl.pallas_call(
        _compress_kernel,
        grid=(bsz,),
        in_specs=[pl.BlockSpec((1, s, wz), lambda b: (b, 0, 0)),
                  pl.BlockSpec((1, s, wz), lambda b: (b, 0, 0)),
                  const(pe.shape), const(w1.shape), const(w2.shape), const(bd64.shape), const(g.shape)],
        out_specs=[pl.BlockSpec((1, n, LANES), lambda b: (b, 0, 0))] * 2,
        out_shape=[jax.ShapeDtypeStruct((bsz, n, LANES), BF16)] * 2,
        compiler_params=_cparams(1),
        name="nsa_compress",
    )(kc, vc, pe, w1, w2, bd64, g)


def _window_kernel(*refs, window, tq, n_slab, n_group, use_sink):
    if use_sink:
        sink_ref, q_ref, k_ref, v_ref, o_ref = refs
    else:
        q_ref, k_ref, v_ref, o_ref = refs
    span = window + tq
    half_masks = _half_masks(BF16)
    lane = lax.broadcasted_iota(jnp.int32, (1, LANES), 1)
    n_sub = q_ref.shape[1] // tq
    for sub in range(n_sub):
        q0 = pl.program_id(1) * (n_sub * tq) + sub * tq
        qrows = slice(sub * tq, (sub + 1) * tq)
        start = pl.multiple_of(jnp.maximum(q0 - window, 0), LANES)
        kwin = k_ref[0, pl.ds(start, span), :]
        rel = (q0 + lax.broadcasted_iota(jnp.int32, (tq, span), 0)) - \
              (start + lax.broadcasted_iota(jnp.int32, (tq, span), 1))
        visible = (rel >= 0) & (rel < window)
        scs = [[_dot_t(q_ref[0, qrows, s * LANES:(s + 1) * LANES] * hm, kwin) for hm in half_masks]
               for s in range(n_slab)]
        for s in range(n_slab):
            outs = []
            for kv in range(2):
                sc = jnp.where(visible, scs[s][kv], NEG_INF)
                m = jnp.max(sc, axis=-1, keepdims=True)
                if use_sink:
                    sink = sink_ref[kv * n_group + s]
                    m = jnp.maximum(m, sink)
                p = jnp.exp2((sc - m) * LOG2E)
                acc = _dot(p.astype(BF16), v_ref[0, pl.ds(start, span), kv * LANES:(kv + 1) * LANES])
                denom = jnp.sum(jnp.where(lane == (1 - kv) * HEAD, acc, 0.0), axis=-1, keepdims=True)
                if use_sink:
                    denom = denom + jnp.exp2((sink - m) * LOG2E)
                outs.append(acc / denom)
            o_ref[0, qrows, s * LANES:(s + 1) * LANES] = \
                jnp.where(lane < HEAD, outs[0], outs[1]).astype(o_ref.dtype)


def _window_call(q, k, v, sinks, window, out_dtype, name):
    bsz, s, wq = q.shape
    tq = WINDOW_TQ
    tblk = min(WINDOW_BLOCK, s)
    n_slab = wq // LANES
    use_sink = sinks is not None
    kern = functools.partial(_window_kernel, window=window, tq=tq, n_slab=n_slab, n_group=n_slab,
                             use_sink=use_sink)
    in_specs = [pl.BlockSpec((1, tblk, wq), lambda b, i: (b, i, 0)),
                pl.BlockSpec((1, s, LANES), lambda b, i: (b, 0, 0)),
                pl.BlockSpec((1, s, 2 * LANES), lambda b, i: (b, 0, 0))]
    args = [q, k, v]
    if use_sink:
        in_specs = [pl.BlockSpec(memory_space=pltpu.SMEM)] + in_specs
        args = [sinks] + args
    return pl.pallas_call(
        kern,
        grid=(bsz, s // tblk),
        in_specs=in_specs,
        out_specs=pl.BlockSpec((1, tblk, wq), lambda b, i: (b, i, 0)),
        out_shape=jax.ShapeDtypeStruct((bsz, s, wq), out_dtype),
        compiler_params=_cparams(2),
        name=name,
    )(*args)


LOG2E = 1.4426950408889634


FLASH_TQ, FLASH_TK = 512, 512
WINDOW_TQ, WINDOW_BLOCK = 128, 512


def _softmax_state(n_heads, tq):
    return ((jnp.full((tq, 1), NEG_INF, F32), jnp.zeros((tq, LANES), F32)),) * n_heads


def _online_step(state, sc, vt, c):
    m_prev, acc = state
    m_new = jnp.maximum(m_prev, jnp.max(sc, axis=-1, keepdims=True))
    alpha = jnp.exp2((m_prev - m_new) * c)
    p = jnp.exp2((sc - m_new) * c)
    return m_new, alpha * acc + _dot(p.astype(BF16), vt)


def _normalised(acc, value_half):
    lane = lax.broadcasted_iota(jnp.int32, (1, LANES), 1)
    denom_lane = (1 - value_half) * HEAD
    denom = jnp.sum(jnp.where(lane == denom_lane, acc, 0.0), axis=-1, keepdims=True)
    return acc / denom


def _causal_tile(q0, k0, tq, tk):
    return (k0 + lax.broadcasted_iota(jnp.int32, (tq, tk), 1)) <= (q0 + lax.broadcasted_iota(jnp.int32, (tq, tk), 0))


def _mla_kernel(q_ref, k_ref, v_ref, o_ref, *, tq, tk, c):
    q0 = pl.program_id(1) * tq
    n_full = q0 // tk
    lane = lax.broadcasted_iota(jnp.int32, (1, LANES), 1)

    def step(j, states, diagonal):
        rows = pl.ds(pl.multiple_of(j * tk, tk), tk)
        if diagonal:
            causal = _causal_tile(q0, j * tk, tq, tk)
        scs = [_dot_t(q_ref[0, :, hh * LANES:(hh + 1) * LANES], k_ref[0, rows, hh * LANES:(hh + 1) * LANES])
               for hh in range(B_HEADS)]
        out = []
        for hh in range(B_HEADS):
            sc = jnp.where(causal, scs[hh], NEG_INF) if diagonal else scs[hh]
            out.append(_online_step(states[hh], sc, v_ref[0, rows, hh * LANES:(hh + 1) * LANES], c))
        return tuple(out)

    states = lax.fori_loop(0, n_full, lambda j, st: step(j, st, False), _softmax_state(B_HEADS, tq))
    states = step(n_full, states, True)
    for s in range(B_HEADS // 2):
        o_lo = _normalised(states[2 * s][1], 0)
        o_hi = _normalised(states[2 * s + 1][1], 1)
        o_ref[0, :, s * LANES:(s + 1) * LANES] = jnp.where(lane < HEAD, o_lo, o_hi).astype(o_ref.dtype)


def _mla_call(q, k, v):
    bsz, s, _ = q.shape
    tq, tk = FLASH_TQ, FLASH_TK
    kern = functools.partial(_mla_kernel, tq=tq, tk=tk, c=float((B_NOPE + B_ROPE) ** -0.5) * LOG2E)
    return pl.pallas_call(
        kern,
        grid=(bsz, s // tq),
        in_specs=[pl.BlockSpec((1, tq, 512), lambda b, i: (b, i, 0)),
                  pl.BlockSpec((1, s, 512), lambda b, i: (b, 0, 0)),
                  pl.BlockSpec((1, s, 512), lambda b, i: (b, 0, 0))],
        out_specs=pl.BlockSpec((1, tq, 256), lambda b, i: (b, i, 0)),
        out_shape=jax.ShapeDtypeStruct((bsz, s, 256), BF16),
        compiler_params=_cparams(2),
        name="mla_attention",
    )(q, k, v)


def _cmp_select_kernel(q_ref, k_ref, v_ref, ov_ref, o_ref, sel_ref, *, tq):
    n = k_ref.shape[1]
    t = pl.program_id(1) * tq + lax.broadcasted_iota(jnp.int32, (tq, n), 0)
    j = lax.broadcasted_iota(jnp.int32, (tq, n), 1)
    visible = (j * C_CMP_STRIDE + (C_CMP_BLOCK - 1)) <= t
    kc, vc = k_ref[0], v_ref[0]
    m_lo, m_hi = _half_masks(BF16)
    lane = lax.broadcasted_iota(jnp.int32, (1, LANES), 1)
    p_sum = [jnp.zeros((tq, n), F32), jnp.zeros((tq, n), F32)]
    scs = [[_dot_t(q_ref[0, :, s * LANES:(s + 1) * LANES] * hm, kc) for hm in (m_lo, m_hi)]
           for s in range(C_HEADS // C_KV)]
    for s in range(C_HEADS // C_KV):
        outs = []
        for kv in range(C_KV):
            sc = jnp.where(visible, scs[s][kv], NEG_INF)
            m = jnp.max(sc, axis=-1, keepdims=True)
            p = jnp.where(visible, jnp.exp(sc - m), 0.0)
            denom = jnp.sum(p, axis=-1, keepdims=True)
            p = p * jnp.where(denom > 0.0, 1.0 / denom, 0.0)
            p_sum[kv] = p_sum[kv] + p
            outs.append(_dot(p.astype(BF16), vc))
        o_ref[0, :, s * LANES:(s + 1) * LANES] = jnp.where(lane < HEAD, outs[0], outs[1])

    imp = jnp.zeros((tq, LANES), F32)
    for kv in range(C_KV):
        hi = p_sum[kv].astype(BF16)
        lo = (p_sum[kv] - hi.astype(F32)).astype(BF16)
        imp = imp + _dot(hi, ov_ref[kv]) + _dot(lo, ov_ref[kv])
    tl = pl.program_id(1) * tq + lax.broadcasted_iota(jnp.int32, (tq, LANES), 0)
    lane_f = lax.broadcasted_iota(jnp.int32, (tq, LANES), 1)
    blk = lane_f & (HEAD - 1)
    cur = tl >> 6
    forced = (blk == 0) | (blk == cur) | (blk == cur - 1)
    future = blk > cur
    score = jnp.where(future, NEG_INF, jnp.where(forced, FORCE_SCORE, imp))
    n_forced = 3
    sel = jnp.where(forced, 1.0, 0.0)
    score = jnp.where(forced, REMOVED, score)
    lane_id = lane_f.astype(F32)
    for kv in range(C_KV):
        in_group = (lane_f >= kv * HEAD) & (lane_f < (kv + 1) * HEAD)
        sc = jnp.where(in_group, score, REMOVED)
        for _ in range(C_N_SEL - n_forced):
            best = jnp.max(sc, axis=-1, keepdims=True)
            first = jnp.min(jnp.where(sc == best, lane_id, 2.0 * LANES), axis=-1, keepdims=True)
            hit = lane_id == first
            sel = jnp.where(hit, 1.0, sel)
            sc = jnp.where(hit, REMOVED, sc)
    sel_ref[0] = jnp.where(future, 0.0, sel).astype(BF16)


def _cmp_select_call(q, kcmp, vcmp, ov):
    bsz, s, wq = q.shape
    n = kcmp.shape[1]
    tq = min(256, s)
    kern = functools.partial(_cmp_select_kernel, tq=tq)
    return pl.pallas_call(
        kern,
        grid=(bsz, s // tq),
        in_specs=[pl.BlockSpec((1, tq, wq), lambda b, i: (b, i, 0)),
                  pl.BlockSpec((1, n, LANES), lambda b, i: (b, 0, 0)),
                  pl.BlockSpec((1, n, LANES), lambda b, i: (b, 0, 0)),
                  pl.BlockSpec(ov.shape, lambda b, i: (0, 0, 0))],
        out_specs=[pl.BlockSpec((1, tq, wq), lambda b, i: (b, i, 0)),
                   pl.BlockSpec((1, tq, LANES), lambda b, i: (b, i, 0))],
        out_shape=[jax.ShapeDtypeStruct((bsz, s, wq), F32),
                   jax.ShapeDtypeStruct((bsz, s, LANES), BF16)],
        compiler_params=_cparams(2),
        name="nsa_cmp_select",
    )(q, kcmp, vcmp, ov)


def _slc_kernel(q_ref, k_ref, v_ref, sel_ref, e_ref, o_ref, *, tq, tk):
    q0 = pl.program_id(1) * tq
    n_full = q0 // tk
    lane = lax.broadcasted_iota(jnp.int32, (1, LANES), 1)
    m_lo, m_hi = _half_masks(BF16)
    n_slab = C_HEADS // C_KV
    outs = []
    for kv, hm in enumerate((m_lo, m_hi)):
        sel_h = sel_ref[0] * hm
        qm = [q_ref[0, :, s * LANES:(s + 1) * LANES] * hm for s in range(n_slab)]

        def step(j, states, diagonal):
            rows = pl.ds(pl.multiple_of(j * tk, tk), tk)
            kt = k_ref[0, rows, :]
            vt = v_ref[0, rows, kv * LANES:(kv + 1) * LANES]
            chosen = _dot(sel_h, e_ref[j]) > 0.5
            if diagonal:
                chosen = chosen & _causal_tile(q0, j * tk, tq, tk)
            scs = [_dot_t(qm[s], kt) for s in range(n_slab)]
            return tuple(_online_step(states[s], jnp.where(chosen, scs[s], NEG_INF), vt, LOG2E)
                         for s in range(n_slab))

        states = lax.fori_loop(0, n_full, lambda j, st: step(j, st, False), _softmax_state(n_slab, tq))
        states = step(n_full, states, True)
        outs.append([_normalised(states[s][1], kv) for s in range(n_slab)])
    for s in range(n_slab):
        o_ref[0, :, s * LANES:(s + 1) * LANES] = jnp.where(lane < HEAD, outs[0][s], outs[1][s])


def _slc_call(q, k, v, sel, e):
    bsz, s, wq = q.shape
    tq, tk = FLASH_TQ, e.shape[2]
    kern = functools.partial(_slc_kernel, tq=tq, tk=tk)
    return pl.pallas_call(
        kern,
        grid=(bsz, s // tq),
        in_specs=[pl.BlockSpec((1, tq, wq), lambda b, i: (b, i, 0)),
                  pl.BlockSpec((1, s, LANES), lambda b, i: (b, 0, 0)),
                  pl.BlockSpec((1, s, 2 * LANES), lambda b, i: (b, 0, 0)),
                  pl.BlockSpec((1, tq, LANES), lambda b, i: (b, i, 0)),
                  pl.BlockSpec(e.shape, lambda b, i: (0, 0, 0))],
        out_specs=pl.BlockSpec((1, tq, wq), lambda b, i: (b, i, 0)),
        out_shape=jax.ShapeDtypeStruct((bsz, s, wq), F32),
        compiler_params=_cparams(2),
        name="nsa_selected",
    )(q, k, v, sel, e)


def _merge_kernel(h_ref, mod_ref, oa_ref, ob_ref, ocmp_ref, oslc_ref, owin_ref, gf_ref, wg_ref, bg_ref,
                  wo_ref, o_ref, cat_ref):
    wc = ocmp_ref.shape[2]
    ts = h_ref.shape[1]
    halves = [slice(r, r + ts // 2) for r in (0, ts // 2)]
    pre = [_dot(_silu(gf_ref[0, r, :]).astype(BF16), wg_ref[...]) for r in halves]
    for r, g in zip(halves, pre):
        gates = jax.nn.sigmoid(g + bg_ref[...])
        oc = (gates[:, 0:wc] * ocmp_ref[0, r, :] + gates[:, wc:2 * wc] * oslc_ref[0, r, :]
              + gates[:, 2 * wc:3 * wc] * owin_ref[0, r, :])
        cat_ref[r, 0:256] = oa_ref[0, r, :]
        cat_ref[r, 256:512] = ob_ref[0, r, :]
        cat_ref[r, 512:512 + wc] = oc.astype(BF16)
        o_ref[0, r, :] = h_ref[0, r, :] + mod_ref[0, 5:6, :] * _dot(cat_ref[r, :], wo_ref[...])


def _merge_call(h, mod, oa, ob, ocmp, oslc, owin, gf, wg, bg, wo):
    bsz, s, d = h.shape
    ts = min(512, s)

    def tok(wd):
        return pl.BlockSpec((1, ts, wd), lambda b, i: (b, i, 0))

    def const(shape):
        return pl.BlockSpec(shape, lambda b, i: (0,) * len(shape))

    return pl.pallas_call(
        _merge_kernel,
        grid=(bsz, s // ts),
        in_specs=[tok(d), pl.BlockSpec((1, N_MOD, d), lambda b, i: (b, 0, 0)),
                  tok(256), tok(256), tok(512), tok(512), tok(512), tok(LANES),
                  const(wg.shape), const(bg.shape), const(wo.shape)],
        out_specs=tok(d),
        out_shape=jax.ShapeDtypeStruct(h.shape, F32),
        scratch_shapes=[pltpu.VMEM((ts, wo.shape[0]), BF16)],
        compiler_params=_cparams(2),
        name="mixer_merge",
    )(h, mod, oa, ob, ocmp, oslc, owin, gf, wg, bg, wo)


def _cols(w, pieces):
    out = [jnp.zeros((w.shape[0], p), w.dtype) if isinstance(p, int) else w[:, p[0]:p[1]] for p in pieces]
    return jnp.concatenate(out, axis=1)


def _head_cols(base, order):
    return [(base + hd * HEAD, base + (hd + 1) * HEAD) for hd in order]


A_ORDER = (0, 2, 1, 3)
C_ORDER = (0, 4, 1, 5, 2, 6, 3, 7)


def _proj_weight(w):
    pieces = (_head_cols(0, A_ORDER) + [(256, 384), (384, 512), (512, 768), (768, 896)]
              + [64, (896, 928), 32] + _head_cols(928, C_ORDER)
              + [(1440 + 128 * i, 1568 + 128 * i) for i in range(6)] + [(2208, 2272), 64])
    return _cols(w, pieces).astype(BF16)


def _mla_weights(w_uq, w_ukv):
    dq = B_NOPE + B_ROPE
    uq = []
    for hd in range(B_HEADS):
        uq += [(hd * dq, hd * dq + dq), 32]
    dk = B_NOPE + B_V
    uk = []
    for hd in range(B_HEADS):
        uk += [(hd * dk, hd * dk + B_NOPE), 64]
    uv = [(hd * dk + B_NOPE, (hd + 1) * dk) for hd in range(B_HEADS)]
    return _cols(w_uq, uq).astype(BF16), _cols(w_ukv, uk + uv).astype(BF16)


def _tile_to(v, reps, width=512):
    t = jnp.tile(v, reps)
    return jnp.pad(t, (0, width - t.shape[0]))


def _gain_rows(a_qk_g, b_q_lat_g, b_kv_lat_g, b_qk_nope_g, b_qk_rope_g, c_qk_g):
    z32, z64 = jnp.zeros((32,), F32), jnp.zeros((64,), F32)
    rows = [_tile_to(a_qk_g[0], 4), _tile_to(a_qk_g[1], 2), _tile_to(c_qk_g[0], 8),
            _tile_to(c_qk_g[2], 2), _tile_to(c_qk_g[3], 2),
            jnp.tile(jnp.concatenate([b_qk_nope_g[0], b_qk_rope_g[0], z32]), 4),
            jnp.tile(jnp.concatenate([b_qk_nope_g[1], z64]), 4),
            _tile_to(jnp.concatenate([z64, b_qk_rope_g[1], z32]), 1),
            _tile_to(b_q_lat_g, 1), _tile_to(b_kv_lat_g, 1)]
    rows += [jnp.zeros((512,), F32)] * (16 - len(rows))
    return jnp.stack(rows)


def _block_diag_consts():
    i = np.arange(256)
    bd64 = np.where((i[:, None] // 64) == (i[None, :] // 64), 1.0 / 64, 0.0)
    j = i % 128
    grp = np.where(j < 64, 0, np.where(j < 96, 1, 2)) + 3 * (i // 128)
    size = np.where(j < 64, 64.0, 32.0)
    bdq = np.where(grp[:, None] == grp[None, :], 1.0 / size[None, :], 0.0)
    return jnp.asarray(bd64, BF16), jnp.asarray(bdq, BF16)


def _compress_weights(pe, w1, w2):
    n_tok = C_CMP_STRIDE
    pe_rows = jnp.broadcast_to(pe.reshape(2, 2, n_tok, 1, HEAD), (2, 2, n_tok, C_KV, HEAD))
    pe_rows = pe_rows.reshape(2, 2, n_tok * C_KV * HEAD)
    w1r = w1.reshape(2, 2, n_tok, HEAD, C_CMP_HID)
    eye = jnp.eye(C_KV, dtype=F32)
    w1big = jnp.einsum('xytdj,kq->xytkdqj', w1r, eye).reshape(2, 2, n_tok * C_KV * HEAD, C_KV * C_CMP_HID)
    w2big = jnp.einsum('xjd,kq->xkjqd', w2, eye).reshape(2, C_KV * C_CMP_HID, C_KV * HEAD)
    return pe_rows, w1big.astype(BF16), w2big.astype(BF16)


def _overlap_const(n_chunk):
    n_slc = n_chunk * C_CMP_STRIDE // C_SLC_BLOCK
    ci = np.arange(n_chunk)[:, None] * C_CMP_STRIDE
    sj = np.arange(n_slc)[None, :] * C_SLC_BLOCK
    ov = ((ci <= sj + C_SLC_BLOCK - 1) & (ci + C_CMP_BLOCK - 1 >= sj)).astype(np.float32)
    ov[n_chunk - 1] = 0.0
    out = np.zeros((C_KV, n_chunk, LANES), np.float32)
    for kv in range(C_KV):
        out[kv, :, kv * HEAD:kv * HEAD + n_slc] = ov
    return jnp.asarray(out, BF16)


def _expand_const(s, tk):
    r = np.arange(LANES)[None, :, None] % HEAD
    c = (np.arange(s // tk)[:, None, None] * tk + np.arange(tk)[None, None, :]) // C_SLC_BLOCK
    return jnp.asarray((r == c).astype(np.float32), BF16)


def _gate_weights(c_gate_w, c_gate_b):
    head_of_lane = np.repeat(np.asarray(C_ORDER), HEAD)
    cols = np.concatenate([head_of_lane * 3 + k for k in range(3)])
    wg = jnp.concatenate([c_gate_w[:, cols], jnp.zeros((LANES - C_GATE_HID, cols.size), F32)], axis=0)
    return wg.astype(BF16), c_gate_b[cols].reshape(1, -1)


def _out_weight(w_out):
    rows = ([(hd * HEAD, (hd + 1) * HEAD) for hd in A_ORDER] + [(256, 512)]
            + [(512 + hd * HEAD, 512 + (hd + 1) * HEAD) for hd in C_ORDER])
    return jnp.concatenate([w_out[a:b] for a, b in rows], axis=0).astype(BF16)


def kernel(x, c, positions, ada_w, ada_b, norm_g, ffn_w_in, ffn_w_out, w_in, w_out, a_sinks, a_qk_g,
           b_q_lat_g, b_kv_lat_g, b_w_uq, b_w_ukv, b_qk_nope_g, b_qk_rope_g, c_qk_g, c_cmp_pe, c_cmp_w1,
           c_cmp_w2, c_gate_w, c_gate_b):
    bsz, s, d = x.shape
    depth = ada_w.shape[0]
    n_chunk = s // C_CMP_STRIDE
    tq = min(256, s)
    assert s % tq == 0 and s % FLASH_TK == 0 and s >= C_WINDOW + tq and s // C_SLC_BLOCK <= HEAD

    mod_all = _ada_call(c, ada_w, ada_b).reshape(depth, bsz, N_MOD, d)
    tab = _rope_call(positions)
    bd64, bdq = _block_diag_consts()
    ov = _overlap_const(n_chunk)
    e = _expand_const(s, FLASH_TK)

    h = x
    for l in range(depth):
        mod = mod_all[l]
        h = _ffn_call(h, mod, norm_g[l, 0:1], ffn_w_in[l, 0].astype(BF16), ffn_w_out[l, 0].astype(BF16), 0)

        wuq, wukv = _mla_weights(b_w_uq[l], b_w_ukv[l])
        gv = _gain_rows(a_qk_g[l], b_q_lat_g[l], b_kv_lat_g[l], b_qk_nope_g[l], b_qk_rope_g[l], c_qk_g[l])
        (qa, ka, va, qb, kb, vb, qc, qcr, kc, vc, ks, vs, kw, vw, gf) = _prep_call(
            h, mod, norm_g[l, 1:2], _proj_weight(w_in[l]), wuq, wukv, gv, bd64, bdq, tab)

        pe_rows, w1big, w2big = _compress_weights(c_cmp_pe[l], c_cmp_w1[l], c_cmp_w2[l])
        kcmp, vcmp = _compress_call(kc, vc, pe_rows, w1big, w2big, bd64,
                                    jnp.tile(c_qk_g[l, 1], 2).reshape(1, LANES))

        oa = _window_call(qa, ka, va, a_sinks[l], A_WINDOW, BF16, "swa_attention")
        ob = _mla_call(qb, kb, vb)
        ocmp, sel = _cmp_select_call(qc, kcmp, vcmp, ov)
        oslc = _slc_call(qcr, ks, vs, sel, e)
        owin = _window_call(qcr, kw, vw, None, C_WINDOW, F32, "nsa_window")

        wg, bg = _gate_weights(c_gate_w[l], c_gate_b[l])
        h = _merge_call(h, mod, oa, ob, ocmp, oslc, owin, gf, wg, bg, _out_weight(w_out[l]))

        h = _ffn_call(h, mod, norm_g[l, 2:3], ffn_w_in[l, 1].astype(BF16), ffn_w_out[l, 1].astype(BF16), 6)
    return h
```

```python
import functools

import numpy as np
import jax
import jax.numpy as jnp
from jax import lax
from jax.experimental import pallas as pl
from jax.experimental.pallas import tpu as pltpu

F32 = jnp.float32
BF16 = jnp.bfloat16

LANES = 128
HEAD = 64
EPS = 1e-6
ROPE_THETA = 10000.0
NEG_INF = -1e30
FORCE_SCORE = 1e4
REMOVED = -3e38
N_MOD = 9
LOG2E = 1.4426950408889634

A_HEADS, A_KV, A_WINDOW = 4, 2, 128
B_HEADS, B_Q_LORA, B_KV_LORA, B_NOPE, B_ROPE, B_V = 4, 256, 128, 64, 32, 64
C_HEADS, C_KV, C_WINDOW = 8, 2, 256
C_CMP_BLOCK, C_CMP_STRIDE, C_SLC_BLOCK, C_N_SEL = 32, 16, 64, 8
C_CMP_HID, C_GATE_HID = 128, 64

VMEM_LIMIT = 56 * 1024 * 1024

O_AQ, O_AK, O_AV, O_BCQ, O_BCKV, O_KR, O_CQ = 0, 256, 384, 512, 768, 896, 1024
O_KC, O_VC, O_KS, O_VS, O_KW, O_VW, O_GF, W_PROJ = 1536, 1664, 1792, 1920, 2048, 2176, 2304, 2432


def _cparams(n_grid):
    return pltpu.CompilerParams(dimension_semantics=("arbitrary",) * n_grid,
                                vmem_limit_bytes=VMEM_LIMIT)


def _dot(a, b):
    return jnp.dot(a, b, preferred_element_type=F32)


def _dot_t(a, b):
    return lax.dot_general(a, b, (((1,), (1,)), ((), ())), preferred_element_type=F32)


def _silu(x):
    return x * jax.nn.sigmoid(x)


def _half_masks(dtype):
    lane = lax.broadcasted_iota(jnp.int32, (1, LANES), 1)
    lo = jnp.where(lane < HEAD, 1.0, 0.0).astype(dtype)
    hi = jnp.where(lane >= HEAD, 1.0, 0.0).astype(dtype)
    return lo, hi


def _ada_kernel(c_ref, w_ref, b_ref, o_ref):
    cond = _silu(c_ref[...]).astype(BF16)
    o_ref[0] = _dot(cond, w_ref[0].astype(BF16)) + b_ref[0]


def _ada_call(c, ada_w, ada_b):
    depth, d, nm = ada_w.shape
    bsz = c.shape[0]
    tn = 1024
    return pl.pallas_call(
        _ada_kernel,
        grid=(depth, nm // tn),
        in_specs=[pl.BlockSpec((bsz, d), lambda l, j: (0, 0)),
                  pl.BlockSpec((1, d, tn), lambda l, j: (l, 0, j)),
                  pl.BlockSpec((1, 1, tn), lambda l, j: (l, 0, j))],
        out_specs=pl.BlockSpec((1, bsz, tn), lambda l, j: (l, 0, j)),
        out_shape=jax.ShapeDtypeStruct((depth, bsz, nm), F32),
        compiler_params=_cparams(2),
        name="adaln",
    )(c, ada_w, ada_b.reshape(depth, 1, nm))


def _rope_kernel(pos_ref, f_ref, o_ref):
    pos = pos_ref[0]
    lane = lax.broadcasted_iota(jnp.int32, (1, LANES), 1)
    ang64 = pos * f_ref[0:1, :]
    sign64 = jnp.where((lane & 63) < 32, -1.0, 1.0)
    o_ref[0, :, 0:128] = jnp.cos(ang64)
    o_ref[0, :, 128:256] = jnp.sin(ang64) * sign64
    ang32 = pos * f_ref[1:2, :]
    sign32 = jnp.where(lane < 80, -1.0, 1.0)
    o_ref[0, :, 256:384] = jnp.cos(ang32)
    o_ref[0, :, 384:512] = jnp.sin(ang32) * sign32


def _rope_call(positions):
    bsz, s = positions.shape
    ts = min(512, s)
    inv64 = 1.0 / (ROPE_THETA ** (jnp.arange(0, HEAD, 2, dtype=F32) / HEAD))
    inv32 = 1.0 / (ROPE_THETA ** (jnp.arange(0, B_ROPE, 2, dtype=F32) / B_ROPE))
    z = jnp.zeros((32,), F32)
    f = jnp.stack([jnp.tile(inv64, 4),
                   jnp.concatenate([z, z, inv32, inv32, z])])
    pos = positions.astype(F32).reshape(bsz, s, 1)
    return pl.pallas_call(
        _rope_kernel,
        grid=(bsz, s // ts),
        in_specs=[pl.BlockSpec((1, ts, 1), lambda b, i: (b, i, 0)),
                  pl.BlockSpec((2, LANES), lambda b, i: (0, 0))],
        out_specs=pl.BlockSpec((1, ts, 512), lambda b, i: (b, i, 0)),
        out_shape=jax.ShapeDtypeStruct((bsz, s, 512), F32),
        compiler_params=_cparams(2),
        name="rope_tables",
    )(pos, f)


def _modulated_norm(h, g, scale, shift):
    y = h * lax.rsqrt(jnp.mean(h * h, axis=-1, keepdims=True) + EPS)
    return (y * g) * (1.0 + scale) + shift


def _ffn_kernel(h_ref, mod_ref, g_ref, wi_ref, wo_ref, o_ref, act_ref, *, mod_row, d_ff, chunk):
    h = h_ref[0]
    u = _modulated_norm(h, g_ref[...], mod_ref[0, mod_row + 1:mod_row + 2, :],
                        mod_ref[0, mod_row:mod_row + 1, :]).astype(BF16)
    for c in range(0, d_ff, chunk):
        gate = _dot(u, wi_ref[:, c:c + chunk])
        up = _dot(u, wi_ref[:, d_ff + c:d_ff + c + chunk])
        act_ref[:, c:c + chunk] = (_silu(gate) * up).astype(BF16)
    y = _dot(act_ref[...], wo_ref[...])
    o_ref[0] = h + (0.5 * mod_ref[0, mod_row + 2:mod_row + 3, :]) * y


def _ffn_call(h, mod, g, w_in, w_out, mod_row):
    bsz, s, d = h.shape
    d_ff = w_out.shape[0]
    ts = min(512, s)
    chunk = 256
    kern = functools.partial(_ffn_kernel, mod_row=mod_row, d_ff=d_ff, chunk=chunk)
    return pl.pallas_call(
        kern,
        grid=(bsz, s // ts),
        in_specs=[pl.BlockSpec((1, ts, d), lambda b, i: (b, i, 0)),
                  pl.BlockSpec((1, N_MOD, d), lambda b, i: (b, 0, 0)),
                  pl.BlockSpec((1, d), lambda b, i: (0, 0)),
                  pl.BlockSpec((d, 2 * d_ff), lambda b, i: (0, 0), pipeline_mode=pl.Buffered(1)),
                  pl.BlockSpec((d_ff, d), lambda b, i: (0, 0), pipeline_mode=pl.Buffered(1))],
        out_specs=pl.BlockSpec((1, ts, d), lambda b, i: (b, i, 0)),
        out_shape=jax.ShapeDtypeStruct(h.shape, F32),
        scratch_shapes=[pltpu.VMEM((ts, d_ff), BF16)],
        compiler_params=_cparams(2),
        name="ffn",
    )(h, mod, g, w_in, w_out)


def _group_mean_sq(x, bd_ref):
    w = x.shape[-1]
    sq = x * x
    hi = sq.astype(BF16)
    lo = (sq - hi.astype(F32)).astype(BF16)
    pieces = []
    step = 256 if w % 256 == 0 else 128
    for c in range(0, w, step):
        bd = bd_ref[0:step, 0:step]
        pieces.append(_dot(hi[:, c:c + step], bd) + _dot(lo[:, c:c + step], bd))
    return pieces[0] if len(pieces) == 1 else jnp.concatenate(pieces, axis=-1)


def _group_norm(x, bd_ref, gain):
    return x * lax.rsqrt(_group_mean_sq(x, bd_ref) + EPS) * gain


def _rope_slab(x, cos, sin, first_half, fwd, bwd):
    rot = jnp.where(first_half, pltpu.roll(x, fwd, axis=1), pltpu.roll(x, bwd, axis=1))
    return x * cos + rot * sin


def _rope64(x, cos, sin):
    lane = lax.broadcasted_iota(jnp.int32, (1, LANES), 1)
    first = (lane & 63) < 32
    out = [_rope_slab(x[:, c:c + LANES], cos, sin, first, 96, 32) for c in range(0, x.shape[-1], LANES)]
    return out[0] if len(out) == 1 else jnp.concatenate(out, axis=-1)


def _rope32(x, cos, sin):
    lane = lax.broadcasted_iota(jnp.int32, (1, LANES), 1)
    first = lane < 80
    out = [_rope_slab(x[:, c:c + LANES], cos, sin, first, 112, 16) for c in range(0, x.shape[-1], LANES)]
    return out[0] if len(out) == 1 else jnp.concatenate(out, axis=-1)


def _with_ones(v):
    lane = lax.broadcasted_iota(jnp.int32, (1, LANES), 1)
    out = []
    for c in range(0, v.shape[-1], LANES):
        slab = v[:, c:c + LANES]
        out += [jnp.where(lane < HEAD, slab, 1.0), jnp.where(lane < HEAD, 1.0, slab)]
    return jnp.concatenate(out, axis=-1).astype(BF16)


def _prep_kernel(h_ref, mod_ref, g_ref, w_ref, wuq_ref, wukv_ref, gv_ref, bd64_ref, bdq_ref, tab_ref,
                 qa_ref, ka_ref, va_ref, qb_ref, kb_ref, vb_ref, qc_ref, qcr_ref,
                 kc_ref, vc_ref, ks_ref, vs_ref, kw_ref, vw_ref, gf_ref):
    u = _modulated_norm(h_ref[0], g_ref[...], mod_ref[0, 4:5, :], mod_ref[0, 3:4, :]).astype(BF16)
    cos64, sin64 = tab_ref[0, :, 0:128], tab_ref[0, :, 128:256]
    cos32, sin32 = tab_ref[0, :, 256:384], tab_ref[0, :, 384:512]

    proj_all = _dot(u, w_ref[...])

    def proj(a, b):
        return proj_all[:, a:b]

    qa = _rope64(_group_norm(proj(O_AQ, O_AK), bd64_ref, gv_ref[0:1, 0:256]), cos64, sin64)
    qa_ref[0] = (qa * (HEAD ** -0.5 * LOG2E)).astype(BF16)
    ka = _rope64(_group_norm(proj(O_AK, O_AV), bd64_ref, gv_ref[1:2, 0:128]), cos64, sin64)
    ka_ref[0] = ka.astype(BF16)
    va_ref[0] = _with_ones(proj(O_AV, O_BCQ))

    cq = proj(O_BCQ, O_BCKV)
    cq = cq * lax.rsqrt(jnp.mean(cq * cq, axis=-1, keepdims=True) + EPS) * gv_ref[8:9, 0:256]
    qb = _group_norm(_dot(cq.astype(BF16), wuq_ref[...]), bdq_ref, gv_ref[5:6, :])
    qb_ref[0] = (_rope32(qb, cos32, sin32) * ((B_NOPE + B_ROPE) ** -0.5 * LOG2E)).astype(BF16)
    ckv = proj(O_BCKV, O_KR)
    ckv = (ckv * lax.rsqrt(jnp.mean(ckv * ckv, axis=-1, keepdims=True) + EPS) * gv_ref[9:10, 0:128]).astype(BF16)
    k_nope = _group_norm(_dot(ckv, wukv_ref[:, 0:512]), bd64_ref, gv_ref[6:7, :])
    vb_ref[0] = _with_ones(_dot(ckv, wukv_ref[:, 512:768]))
    k_pe = _rope32(_group_norm(proj(O_KR, O_CQ), bdq_ref, gv_ref[7:8, 0:128]), cos32, sin32)
    for hh in range(B_HEADS):
        kb_ref[0, :, hh * LANES:(hh + 1) * LANES] = (k_nope[:, hh * LANES:(hh + 1) * LANES] + k_pe).astype(BF16)

    qc = _group_norm(proj(O_CQ, O_KC), bd64_ref, gv_ref[2:3, :])
    qc_ref[0] = (qc * HEAD ** -0.5).astype(BF16)
    qcr_ref[0] = (_rope64(qc, cos64, sin64) * (HEAD ** -0.5 * LOG2E)).astype(BF16)
    kc_ref[0] = proj(O_KC, O_VC)
    vc_ref[0] = proj(O_VC, O_KS)
    ks = _rope64(_group_norm(proj(O_KS, O_VS), bd64_ref, gv_ref[3:4, 0:128]), cos64, sin64)
    ks_ref[0] = ks.astype(BF16)
    vs_ref[0] = _with_ones(proj(O_VS, O_KW))
    kw = _rope64(_group_norm(proj(O_KW, O_VW), bd64_ref, gv_ref[4:5, 0:128]), cos64, sin64)
    kw_ref[0] = kw.astype(BF16)
    vw_ref[0] = _with_ones(proj(O_VW, O_GF))
    gf_ref[0] = proj(O_GF, W_PROJ)


def _prep_call(h, mod, g, w, wuq, wukv, gv, bd64, bdq, tab):
    bsz, s, d = h.shape
    ts = min(512, s)
    widths = [(256, BF16), (128, BF16), (256, BF16), (512, BF16), (512, BF16), (512, BF16),
              (512, BF16), (512, BF16), (128, F32), (128, F32), (128, BF16), (256, BF16),
              (128, BF16), (256, BF16), (128, F32)]

    def const(shape):
        return pl.BlockSpec(shape, lambda b, i: (0,) * len(shape))

    return pl.pallas_call(
        _prep_kernel,
        grid=(bsz, s // ts),
        in_specs=[pl.BlockSpec((1, ts, d), lambda b, i: (b, i, 0)),
                  pl.BlockSpec((1, N_MOD, d), lambda b, i: (b, 0, 0)),
                  const((1, d)), const(w.shape), const(wuq.shape), const(wukv.shape),
                  const(gv.shape), const(bd64.shape), const(bdq.shape),
                  pl.BlockSpec((1, ts, 512), lambda b, i: (b, i, 0))],
        out_specs=[pl.BlockSpec((1, ts, wd), lambda b, i: (b, i, 0)) for wd, _ in widths],
        out_shape=[jax.ShapeDtypeStruct((bsz, s, wd), dt) for wd, dt in widths],
        compiler_params=_cparams(2),
        name="mixer_proj",
    )(h, mod, g, w, wuq, wukv, gv, bd64, bdq, tab)


def _compress_kernel(kc_ref, vc_ref, pe_ref, w1_ref, w2_ref, bd64_ref, g_ref, ko_ref, vo_ref):
    n = kc_ref.shape[1] // C_CMP_STRIDE

    def one(z_ref, idx):
        top = jnp.zeros((n, 2 * C_CMP_HID), F32)
        bot = jnp.zeros((n, 2 * C_CMP_HID), F32)
        for t in range(C_CMP_STRIDE):
            z = z_ref[0, pl.ds(t, n, stride=C_CMP_STRIDE), :]
            cols = slice(t * LANES, (t + 1) * LANES)
            top = top + _dot((z + pe_ref[idx, 0:1, cols]).astype(BF16), w1_ref[idx, 0, cols, :])
            bot = bot + _dot((z + pe_ref[idx, 1:2, cols]).astype(BF16), w1_ref[idx, 1, cols, :])
        pre = top + pltpu.roll(bot, n - 1, axis=0)
        return _dot(_silu(pre).astype(BF16), w2_ref[idx])

    ko_ref[0] = _group_norm(one(kc_ref, 0), bd64_ref, g_ref[...]).astype(BF16)
    vo_ref[0] = one(vc_ref, 1).astype(BF16)


def _compress_call(kc, vc, pe, w1, w2, bd64, g):
    bsz, s, wz = kc.shape
    n = s // C_CMP_STRIDE

    def const(shape):
        return pl.BlockSpec(shape, lambda b: (0,) * len(shape))

    return pl.pallas_call(
        _compress_kernel,
        grid=(bsz,),
        in_specs=[pl.BlockSpec((1, s, wz), lambda b: (b, 0, 0)),
                  pl.BlockSpec((1, s, wz), lambda b: (b, 0, 0)),
                  const(pe.shape), const(w1.shape), const(w2.shape), const(bd64.shape), const(g.shape)],
        out_specs=[pl.BlockSpec((1, n, LANES), lambda b: (b, 0, 0))] * 2,
        out_shape=[jax.ShapeDtypeStruct((bsz, n, LANES), BF16)] * 2,
        compiler_params=_cparams(1),
        name="nsa_compress",
    )(kc, vc, pe, w1, w2, bd64, g)


def _window_kernel(*refs, window, tq, n_slab, n_group, use_sink):
    if use_sink:
        sink_ref, q_ref, k_ref, v_ref, o_ref = refs
    else:
        q_ref, k_ref, v_ref, o_ref = refs
    span = window + tq
    half_masks = _half_masks(BF16)
    lane = lax.broadcasted_iota(jnp.int32, (1, LANES), 1)
    n_sub = q_ref.shape[1] // tq
    for sub in range(n_sub):
        q0 = pl.program_id(1) * (n_sub * tq) + sub * tq
        qrows = slice(sub * tq, (sub + 1) * tq)
        start = pl.multiple_of(jnp.maximum(q0 - window, 0), LANES)
        kwin = k_ref[0, pl.ds(start, span), :]
        rel = (q0 + lax.broadcasted_iota(jnp.int32, (tq, span), 0)) - \
              (start + lax.broadcasted_iota(jnp.int32, (tq, span), 1))
        visible = (rel >= 0) & (rel < window)
        scs = [[_dot_t(q_ref[0, qrows, s * LANES:(s + 1) * LANES] * hm, kwin) for hm in half_masks]
               for s in range(n_slab)]
        for s in range(n_slab):
            outs = []
            for kv in range(2):
                sc = jnp.where(visible, scs[s][kv], NEG_INF)
                m = jnp.max(sc, axis=-1, keepdims=True)
                if use_sink:
                    sink = sink_ref[kv * n_group + s] * LOG2E
                    m = jnp.maximum(m, sink)
                p = jnp.exp2(sc - m)
                acc = _dot(p.astype(BF16), v_ref[0, pl.ds(start, span), kv * LANES:(kv + 1) * LANES])
                denom = jnp.sum(jnp.where(lane == (1 - kv) * HEAD, acc, 0.0), axis=-1, keepdims=True)
                if use_sink:
                    denom = denom + jnp.exp2(sink - m)
                outs.append(acc / denom)
            o_ref[0, qrows, s * LANES:(s + 1) * LANES] = \
                jnp.where(lane < HEAD, outs[0], outs[1]).astype(o_ref.dtype)


def _window_call(q, k, v, sinks, window, out_dtype, name):
    bsz, s, wq = q.shape
    tq = WINDOW_TQ
    tblk = min(WINDOW_BLOCK, s)
    n_slab = wq // LANES
    use_sink = sinks is not None
    kern = functools.partial(_window_kernel, window=window, tq=tq, n_slab=n_slab, n_group=n_slab,
                             use_sink=use_sink)
    in_specs = [pl.BlockSpec((1, tblk, wq), lambda b, i: (b, i, 0)),
                pl.BlockSpec((1, s, LANES), lambda b, i: (b, 0, 0)),
                pl.BlockSpec((1, s, 2 * LANES), lambda b, i: (b, 0, 0))]
    args = [q, k, v]
    if use_sink:
        in_specs = [pl.BlockSpec(memory_space=pltpu.SMEM)] + in_specs
        args = [sinks] + args
    return pl.pallas_call(
        kern,
        grid=(bsz, s // tblk),
        in_specs=in_specs,
        out_specs=pl.BlockSpec((1, tblk, wq), lambda b, i: (b, i, 0)),
        out_shape=jax.ShapeDtypeStruct((bsz, s, wq), out_dtype),
        compiler_params=_cparams(2),
        name=name,
    )(*args)


FLASH_TQ, FLASH_TK = 512, 512
WINDOW_TQ, WINDOW_BLOCK = 128, 512


def _softmax_state(n_heads, tq):
    return ((jnp.full((tq, 1), NEG_INF, F32), jnp.zeros((tq, LANES), F32)),) * n_heads


def _online_step(state, sc, vt):
    m_prev, acc = state
    m_new = jnp.maximum(m_prev, jnp.max(sc, axis=-1, keepdims=True))
    alpha = jnp.exp2(m_prev - m_new)
    p = jnp.exp2(sc - m_new)
    return m_new, alpha * acc + _dot(p.astype(BF16), vt)


def _normalised(acc, value_half):
    lane = lax.broadcasted_iota(jnp.int32, (1, LANES), 1)
    denom_lane = (1 - value_half) * HEAD
    denom = jnp.sum(jnp.where(lane == denom_lane, acc, 0.0), axis=-1, keepdims=True)
    return acc / denom


def _causal_tile(q0, k0, tq, tk):
    return (k0 + lax.broadcasted_iota(jnp.int32, (tq, tk), 1)) <= (q0 + lax.broadcasted_iota(jnp.int32, (tq, tk), 0))


def _mla_kernel(q_ref, k_ref, v_ref, o_ref, *, tq, tk):
    q0 = pl.program_id(1) * tq
    n_full = q0 // tk
    lane = lax.broadcasted_iota(jnp.int32, (1, LANES), 1)

    def step(j, states, diagonal):
        rows = pl.ds(pl.multiple_of(j * tk, tk), tk)
        if diagonal:
            causal = _causal_tile(q0, j * tk, tq, tk)
        scs = [_dot_t(q_ref[0, :, hh * LANES:(hh + 1) * LANES], k_ref[0, rows, hh * LANES:(hh + 1) * LANES])
               for hh in range(B_HEADS)]
        out = []
        for hh in range(B_HEADS):
            sc = jnp.where(causal, scs[hh], NEG_INF) if diagonal else scs[hh]
            out.append(_online_step(states[hh], sc, v_ref[0, rows, hh * LANES:(hh + 1) * LANES]))
        return tuple(out)

    states = lax.fori_loop(0, n_full, lambda j, st: step(j, st, False), _softmax_state(B_HEADS, tq))
    states = step(n_full, states, True)
    for s in range(B_HEADS // 2):
        o_lo = _normalised(states[2 * s][1], 0)
        o_hi = _normalised(states[2 * s + 1][1], 1)
        o_ref[0, :, s * LANES:(s + 1) * LANES] = jnp.where(lane < HEAD, o_lo, o_hi).astype(o_ref.dtype)


def _mla_call(q, k, v):
    bsz, s, _ = q.shape
    tq, tk = FLASH_TQ, FLASH_TK
    kern = functools.partial(_mla_kernel, tq=tq, tk=tk)
    return pl.pallas_call(
        kern,
        grid=(bsz, s // tq),
        in_specs=[pl.BlockSpec((1, tq, 512), lambda b, i: (b, i, 0)),
                  pl.BlockSpec((1, s, 512), lambda b, i: (b, 0, 0)),
                  pl.BlockSpec((1, s, 512), lambda b, i: (b, 0, 0))],
        out_specs=pl.BlockSpec((1, tq, 256), lambda b, i: (b, i, 0)),
        out_shape=jax.ShapeDtypeStruct((bsz, s, 256), BF16),
        compiler_params=_cparams(2),
        name="mla_attention",
    )(q, k, v)


def _cmp_select_kernel(q_ref, k_ref, v_ref, ov_ref, o_ref, sel_ref, *, tq):
    n = k_ref.shape[1]
    t = pl.program_id(1) * tq + lax.broadcasted_iota(jnp.int32, (tq, n), 0)
    j = lax.broadcasted_iota(jnp.int32, (tq, n), 1)
    visible = (j * C_CMP_STRIDE + (C_CMP_BLOCK - 1)) <= t
    kc, vc = k_ref[0], v_ref[0]
    m_lo, m_hi = _half_masks(BF16)
    lane = lax.broadcasted_iota(jnp.int32, (1, LANES), 1)
    p_sum = [jnp.zeros((tq, n), F32), jnp.zeros((tq, n), F32)]
    scs = [[_dot_t(q_ref[0, :, s * LANES:(s + 1) * LANES] * hm, kc) for hm in (m_lo, m_hi)]
           for s in range(C_HEADS // C_KV)]
    for s in range(C_HEADS // C_KV):
        outs = []
        for kv in range(C_KV):
            sc = jnp.where(visible, scs[s][kv], NEG_INF)
            m = jnp.max(sc, axis=-1, keepdims=True)
            p = jnp.where(visible, jnp.exp(sc - m), 0.0)
            denom = jnp.sum(p, axis=-1, keepdims=True)
            p = p * jnp.where(denom > 0.0, 1.0 / denom, 0.0)
            p_sum[kv] = p_sum[kv] + p
            outs.append(_dot(p.astype(BF16), vc))
        o_ref[0, :, s * LANES:(s + 1) * LANES] = jnp.where(lane < HEAD, outs[0], outs[1])

    imp = jnp.zeros((tq, LANES), F32)
    for kv in range(C_KV):
        hi = p_sum[kv].astype(BF16)
        lo = (p_sum[kv] - hi.astype(F32)).astype(BF16)
        imp = imp + _dot(hi, ov_ref[kv]) + _dot(lo, ov_ref[kv])
    tl = pl.program_id(1) * tq + lax.broadcasted_iota(jnp.int32, (tq, LANES), 0)
    lane_f = lax.broadcasted_iota(jnp.int32, (tq, LANES), 1)
    blk = lane_f & (HEAD - 1)
    cur = tl >> 6
    forced = (blk == 0) | (blk == cur) | (blk == cur - 1)
    future = blk > cur
    score = jnp.where(future, NEG_INF, jnp.where(forced, FORCE_SCORE, imp))
    n_forced = 3
    sel = jnp.where(forced, 1.0, 0.0)
    score = jnp.where(forced, REMOVED, score)
    lane_id = lane_f.astype(F32)
    for kv in range(C_KV):
        in_group = (lane_f >= kv * HEAD) & (lane_f < (kv + 1) * HEAD)
        sc = jnp.where(in_group, score, REMOVED)
        for _ in range(C_N_SEL - n_forced):
            best = jnp.max(sc, axis=-1, keepdims=True)
            first = jnp.min(jnp.where(sc == best, lane_id, 2.0 * LANES), axis=-1, keepdims=True)
            hit = lane_id == first
            sel = jnp.where(hit, 1.0, sel)
            sc = jnp.where(hit, REMOVED, sc)
    sel_ref[0] = jnp.where(future, 0.0, sel).astype(BF16)


def _cmp_select_call(q, kcmp, vcmp, ov):
    bsz, s, wq = q.shape
    n = kcmp.shape[1]
    tq = min(256, s)
    kern = functools.partial(_cmp_select_kernel, tq=tq)
    return pl.pallas_call(
        kern,
        grid=(bsz, s // tq),
        in_specs=[pl.BlockSpec((1, tq, wq), lambda b, i: (b, i, 0)),
                  pl.BlockSpec((1, n, LANES), lambda b, i: (b, 0, 0)),
                  pl.BlockSpec((1, n, LANES), lambda b, i: (b, 0, 0)),
                  pl.BlockSpec(ov.shape, lambda b, i: (0, 0, 0))],
        out_specs=[pl.BlockSpec((1, tq, wq), lambda b, i: (b, i, 0)),
                   pl.BlockSpec((1, tq, LANES), lambda b, i: (b, i, 0))],
        out_shape=[jax.ShapeDtypeStruct((bsz, s, wq), F32),
                   jax.ShapeDtypeStruct((bsz, s, LANES), BF16)],
        compiler_params=_cparams(2),
        name="nsa_cmp_select",
    )(q, kcmp, vcmp, ov)


def _slc_kernel(q_ref, k_ref, v_ref, sel_ref, e_ref, o_ref, *, tq, tk):
    q0 = pl.program_id(1) * tq
    n_full = q0 // tk
    lane = lax.broadcasted_iota(jnp.int32, (1, LANES), 1)
    m_lo, m_hi = _half_masks(BF16)
    n_slab = C_HEADS // C_KV
    outs = []
    for kv, hm in enumerate((m_lo, m_hi)):
        sel_h = sel_ref[0] * hm
        qm = [q_ref[0, :, s * LANES:(s + 1) * LANES] * hm for s in range(n_slab)]

        def step(j, states, diagonal):
            rows = pl.ds(pl.multiple_of(j * tk, tk), tk)
            kt = k_ref[0, rows, :]
            vt = v_ref[0, rows, kv * LANES:(kv + 1) * LANES]
            chosen = _dot(sel_h, e_ref[j]) > 0.5
            if diagonal:
                chosen = chosen & _causal_tile(q0, j * tk, tq, tk)
            scs = [_dot_t(qm[s], kt) for s in range(n_slab)]
            return tuple(_online_step(states[s], jnp.where(chosen, scs[s], NEG_INF), vt)
                         for s in range(n_slab))

        states = lax.fori_loop(0, n_full, lambda j, st: step(j, st, False), _softmax_state(n_slab, tq))
        states = step(n_full, states, True)
        outs.append([_normalised(states[s][1], kv) for s in range(n_slab)])
    for s in range(n_slab):
        o_ref[0, :, s * LANES:(s + 1) * LANES] = jnp.where(lane < HEAD, outs[0][s], outs[1][s])


def _slc_call(q, k, v, sel, e):
    bsz, s, wq = q.shape
    tq, tk = FLASH_TQ, e.shape[2]
    kern = functools.partial(_slc_kernel, tq=tq, tk=tk)
    return pl.pallas_call(
        kern,
        grid=(bsz, s // tq),
        in_specs=[pl.BlockSpec((1, tq, wq), lambda b, i: (b, i, 0)),
                  pl.BlockSpec((1, s, LANES), lambda b, i: (b, 0, 0)),
                  pl.BlockSpec((1, s, 2 * LANES), lambda b, i: (b, 0, 0)),
                  pl.BlockSpec((1, tq, LANES), lambda b, i: (b, i, 0)),
                  pl.BlockSpec(e.shape, lambda b, i: (0, 0, 0))],
        out_specs=pl.BlockSpec((1, tq, wq), lambda b, i: (b, i, 0)),
        out_shape=jax.ShapeDtypeStruct((bsz, s, wq), F32),
        compiler_params=_cparams(2),
        name="nsa_selected",
    )(q, k, v, sel, e)


def _merge_kernel(h_ref, mod_ref, oa_ref, ob_ref, ocmp_ref, oslc_ref, owin_ref, gf_ref, wg_ref, bg_ref,
                  wo_ref, o_ref, cat_ref):
    wc = ocmp_ref.shape[2]
    ts = h_ref.shape[1]
    halves = [slice(r, r + ts // 2) for r in (0, ts // 2)]
    pre = [_dot(_silu(gf_ref[0, r, :]).astype(BF16), wg_ref[...]) for r in halves]
    for r, g in zip(halves, pre):
        gates = jax.nn.sigmoid(g + bg_ref[...])
        oc = (gates[:, 0:wc] * ocmp_ref[0, r, :] + gates[:, wc:2 * wc] * oslc_ref[0, r, :]
              + gates[:, 2 * wc:3 * wc] * owin_ref[0, r, :])
        cat_ref[r, 0:256] = oa_ref[0, r, :]
        cat_ref[r, 256:512] = ob_ref[0, r, :]
        cat_ref[r, 512:512 + wc] = oc.astype(BF16)
        o_ref[0, r, :] = h_ref[0, r, :] + mod_ref[0, 5:6, :] * _dot(cat_ref[r, :], wo_ref[...])


def _merge_call(h, mod, oa, ob, ocmp, oslc, owin, gf, wg, bg, wo):
    bsz, s, d = h.shape
    ts = min(512, s)

    def tok(wd):
        return pl.BlockSpec((1, ts, wd), lambda b, i: (b, i, 0))

    def const(shape):
        return pl.BlockSpec(shape, lambda b, i: (0,) * len(shape))

    return pl.pallas_call(
        _merge_kernel,
        grid=(bsz, s // ts),
        in_specs=[tok(d), pl.BlockSpec((1, N_MOD, d), lambda b, i: (b, 0, 0)),
                  tok(256), tok(256), tok(512), tok(512), tok(512), tok(LANES),
                  const(wg.shape), const(bg.shape), const(wo.shape)],
        out_specs=tok(d),
        out_shape=jax.ShapeDtypeStruct(h.shape, F32),
        scratch_shapes=[pltpu.VMEM((ts, wo.shape[0]), BF16)],
        compiler_params=_cparams(2),
        name="mixer_merge",
    )(h, mod, oa, ob, ocmp, oslc, owin, gf, wg, bg, wo)


def _cols(w, pieces):
    out = [jnp.zeros((w.shape[0], p), w.dtype) if isinstance(p, int) else w[:, p[0]:p[1]] for p in pieces]
    return jnp.concatenate(out, axis=1)


def _head_cols(base, order):
    return [(base + hd * HEAD, base + (hd + 1) * HEAD) for hd in order]


A_ORDER = (0, 2, 1, 3)
C_ORDER = (0, 4, 1, 5, 2, 6, 3, 7)


def _proj_weight(w):
    pieces = (_head_cols(0, A_ORDER) + [(256, 384), (384, 512), (512, 768), (768, 896)]
              + [64, (896, 928), 32] + _head_cols(928, C_ORDER)
              + [(1440 + 128 * i, 1568 + 128 * i) for i in range(6)] + [(2208, 2272), 64])
    return _cols(w, pieces).astype(BF16)


def _mla_weights(w_uq, w_ukv):
    dq = B_NOPE + B_ROPE
    uq = []
    for hd in range(B_HEADS):
        uq += [(hd * dq, hd * dq + dq), 32]
    dk = B_NOPE + B_V
    uk = []
    for hd in range(B_HEADS):
        uk += [(hd * dk, hd * dk + B_NOPE), 64]
    uv = [(hd * dk + B_NOPE, (hd + 1) * dk) for hd in range(B_HEADS)]
    return _cols(w_uq, uq).astype(BF16), _cols(w_ukv, uk + uv).astype(BF16)


def _tile_to(v, reps, width=512):
    t = jnp.tile(v, reps)
    return jnp.pad(t, (0, width - t.shape[0]))


def _gain_rows(a_qk_g, b_q_lat_g, b_kv_lat_g, b_qk_nope_g, b_qk_rope_g, c_qk_g):
    z32, z64 = jnp.zeros((32,), F32), jnp.zeros((64,), F32)
    rows = [_tile_to(a_qk_g[0], 4), _tile_to(a_qk_g[1], 2), _tile_to(c_qk_g[0], 8),
            _tile_to(c_qk_g[2], 2), _tile_to(c_qk_g[3], 2),
            jnp.tile(jnp.concatenate([b_qk_nope_g[0], b_qk_rope_g[0], z32]), 4),
            jnp.tile(jnp.concatenate([b_qk_nope_g[1], z64]), 4),
            _tile_to(jnp.concatenate([z64, b_qk_rope_g[1], z32]), 1),
            _tile_to(b_q_lat_g, 1), _tile_to(b_kv_lat_g, 1)]
    rows += [jnp.zeros((512,), F32)] * (16 - len(rows))
    return jnp.stack(rows)


def _block_diag_consts():
    i = np.arange(256)
    bd64 = np.where((i[:, None] // 64) == (i[None, :] // 64), 1.0 / 64, 0.0)
    j = i % 128
    grp = np.where(j < 64, 0, np.where(j < 96, 1, 2)) + 3 * (i // 128)
    size = np.where(j < 64, 64.0, 32.0)
    bdq = np.where(grp[:, None] == grp[None, :], 1.0 / size[None, :], 0.0)
    return jnp.asarray(bd64, BF16), jnp.asarray(bdq, BF16)


def _compress_weights(pe, w1, w2):
    n_tok = C_CMP_STRIDE
    pe_rows = jnp.broadcast_to(pe.reshape(2, 2, n_tok, 1, HEAD), (2, 2, n_tok, C_KV, HEAD))
    pe_rows = pe_rows.reshape(2, 2, n_tok * C_KV * HEAD)
    w1r = w1.reshape(2, 2, n_tok, HEAD, C_CMP_HID)
    eye = jnp.eye(C_KV, dtype=F32)
    w1big = jnp.einsum('xytdj,kq->xytkdqj', w1r, eye).reshape(2, 2, n_tok * C_KV * HEAD, C_KV * C_CMP_HID)
    w2big = jnp.einsum('xjd,kq->xkjqd', w2, eye).reshape(2, C_KV * C_CMP_HID, C_KV * HEAD)
    return pe_rows, w1big.astype(BF16), w2big.astype(BF16)


def _overlap_const(n_chunk):
    n_slc = n_chunk * C_CMP_STRIDE // C_SLC_BLOCK
    ci = np.arange(n_chunk)[:, None] * C_CMP_STRIDE
    sj = np.arange(n_slc)[None, :] * C_SLC_BLOCK
    ov = ((ci <= sj + C_SLC_BLOCK - 1) & (ci + C_CMP_BLOCK - 1 >= sj)).astype(np.float32)
    ov[n_chunk - 1] = 0.0
    out = np.zeros((C_KV, n_chunk, LANES), np.float32)
    for kv in range(C_KV):
        out[kv, :, kv * HEAD:kv * HEAD + n_slc] = ov
    return jnp.asarray(out, BF16)


def _expand_const(s, tk):
    r = np.arange(LANES)[None, :, None] % HEAD
    c = (np.arange(s // tk)[:, None, None] * tk + np.arange(tk)[None, None, :]) // C_SLC_BLOCK
    return jnp.asarray((r == c).astype(np.float32), BF16)


def _gate_weights(c_gate_w, c_gate_b):
    head_of_lane = np.repeat(np.asarray(C_ORDER), HEAD)
    cols = np.concatenate([head_of_lane * 3 + k for k in range(3)])
    wg = jnp.concatenate([c_gate_w[:, cols], jnp.zeros((LANES - C_GATE_HID, cols.size), F32)], axis=0)
    return wg.astype(BF16), c_gate_b[cols].reshape(1, -1)


def _out_weight(w_out):
    rows = ([(hd * HEAD, (hd + 1) * HEAD) for hd in A_ORDER] + [(256, 512)]
            + [(512 + hd * HEAD, 512 + (hd + 1) * HEAD) for hd in C_ORDER])
    return jnp.concatenate([w_out[a:b] for a, b in rows], axis=0).astype(BF16)


def kernel(x, c, positions, ada_w, ada_b, norm_g, ffn_w_in, ffn_w_out, w_in, w_out, a_sinks, a_qk_g,
           b_q_lat_g, b_kv_lat_g, b_w_uq, b_w_ukv, b_qk_nope_g, b_qk_rope_g, c_qk_g, c_cmp_pe, c_cmp_w1,
           c_cmp_w2, c_gate_w, c_gate_b):
    bsz, s, d = x.shape
    depth = ada_w.shape[0]
    n_chunk = s // C_CMP_STRIDE
    tq = min(256, s)
    assert s % tq == 0 and s % FLASH_TK == 0 and s >= C_WINDOW + tq and s // C_SLC_BLOCK <= HEAD

    mod_all = _ada_call(c, ada_w, ada_b).reshape(depth, bsz, N_MOD, d)
    tab = _rope_call(positions)
    bd64, bdq = _block_diag_consts()
    ov = _overlap_const(n_chunk)
    e = _expand_const(s, FLASH_TK)

    h = x
    for l in range(depth):
        mod = mod_all[l]
        h = _ffn_call(h, mod, norm_g[l, 0:1], ffn_w_in[l, 0].astype(BF16), ffn_w_out[l, 0].astype(BF16), 0)

        wuq, wukv = _mla_weights(b_w_uq[l], b_w_ukv[l])
        gv = _gain_rows(a_qk_g[l], b_q_lat_g[l], b_kv_lat_g[l], b_qk_nope_g[l], b_qk_rope_g[l], c_qk_g[l])
        (qa, ka, va, qb, kb, vb, qc, qcr, kc, vc, ks, vs, kw, vw, gf) = _prep_call(
            h, mod, norm_g[l, 1:2], _proj_weight(w_in[l]), wuq, wukv, gv, bd64, bdq, tab)

        pe_rows, w1big, w2big = _compress_weights(c_cmp_pe[l], c_cmp_w1[l], c_cmp_w2[l])
        kcmp, vcmp = _compress_call(kc, vc, pe_rows, w1big, w2big, bd64,
                                    jnp.tile(c_qk_g[l, 1], 2).reshape(1, LANES))

        oa = _window_call(qa, ka, va, a_sinks[l], A_WINDOW, BF16, "swa_attention")
        ob = _mla_call(qb, kb, vb)
        ocmp, sel = _cmp_select_call(qc, kcmp, vcmp, ov)
        oslc = _slc_call(qcr, ks, vs, sel, e)
        owin = _window_call(qcr, kw, vw, None, C_WINDOW, F32, "nsa_window")

        wg, bg = _gate_weights(c_gate_w[l], c_gate_b[l])
        h = _merge_call(h, mod, oa, ob, ocmp, oslc, owin, gf, wg, bg, _out_weight(w_out[l]))

        h = _ffn_call(h, mod, norm_g[l, 2:3], ffn_w_in[l, 1].astype(BF16), ffn_w_out[l, 1].astype(BF16), 6)
    return h
```

```python
import functools

import numpy as np
import jax
import jax.numpy as jnp
from jax import lax
from jax.experimental import pallas as pl
from jax.experimental.pallas import tpu as pltpu

F32 = jnp.float32
BF16 = jnp.bfloat16

LANES = 128
HEAD = 64
EPS = 1e-6
ROPE_THETA = 10000.0
NEG_INF = -1e30
FORCE_SCORE = 1e4
REMOVED = -3e38
N_MOD = 9
LOG2E = 1.4426950408889634

A_HEADS, A_KV, A_WINDOW = 4, 2, 128
B_HEADS, B_Q_LORA, B_KV_LORA, B_NOPE, B_ROPE, B_V = 4, 256, 128, 64, 32, 64
C_HEADS, C_KV, C_WINDOW = 8, 2, 256
C_CMP_BLOCK, C_CMP_STRIDE, C_SLC_BLOCK, C_N_SEL = 32, 16, 64, 8
C_CMP_HID, C_GATE_HID = 128, 64

VMEM_LIMIT = 56 * 1024 * 1024

O_AQ, O_AK, O_AV, O_BCQ, O_BCKV, O_KR, O_CQ = 0, 256, 384, 512, 768, 896, 1024
O_KC, O_VC, O_KS, O_VS, O_KW, O_VW, O_GF, W_PROJ = 1536, 1664, 1792, 1920, 2048, 2176, 2304, 2432


def _cparams(n_grid):
    return pltpu.CompilerParams(dimension_semantics=("arbitrary",) * n_grid,
                                vmem_limit_bytes=VMEM_LIMIT)


def _dot(a, b):
    return jnp.dot(a, b, preferred_element_type=F32)


def _dot_t(a, b):
    return lax.dot_general(a, b, (((1,), (1,)), ((), ())), preferred_element_type=F32)


def _silu(x):
    return x * jax.nn.sigmoid(x)


def _half_masks(dtype):
    lane = lax.broadcasted_iota(jnp.int32, (1, LANES), 1)
    lo = jnp.where(lane < HEAD, 1.0, 0.0).astype(dtype)
    hi = jnp.where(lane >= HEAD, 1.0, 0.0).astype(dtype)
    return lo, hi


def _ada_kernel(c_ref, w_ref, b_ref, o_ref):
    cond = _silu(c_ref[...]).astype(BF16)
    o_ref[0] = _dot(cond, w_ref[0].astype(BF16)) + b_ref[0]


def _ada_call(c, ada_w, ada_b):
    depth, d, nm = ada_w.shape
    bsz = c.shape[0]
    tn = 1024
    return pl.pallas_call(
        _ada_kernel,
        grid=(depth, nm // tn),
        in_specs=[pl.BlockSpec((bsz, d), lambda l, j: (0, 0)),
                  pl.BlockSpec((1, d, tn), lambda l, j: (l, 0, j)),
                  pl.BlockSpec((1, 1, tn), lambda l, j: (l, 0, j))],
        out_specs=pl.BlockSpec((1, bsz, tn), lambda l, j: (l, 0, j)),
        out_shape=jax.ShapeDtypeStruct((depth, bsz, nm), F32),
        compiler_params=_cparams(2),
        name="adaln",
    )(c, ada_w, ada_b.reshape(depth, 1, nm))


def _rope_kernel(pos_ref, f_ref, o_ref):
    pos = pos_ref[0]
    lane = lax.broadcasted_iota(jnp.int32, (1, LANES), 1)
    ang64 = pos * f_ref[0:1, :]
    sign64 = jnp.where((lane & 63) < 32, -1.0, 1.0)
    o_ref[0, :, 0:128] = jnp.cos(ang64)
    o_ref[0, :, 128:256] = jnp.sin(ang64) * sign64
    ang32 = pos * f_ref[1:2, :]
    sign32 = jnp.where(lane < 80, -1.0, 1.0)
    o_ref[0, :, 256:384] = jnp.cos(ang32)
    o_ref[0, :, 384:512] = jnp.sin(ang32) * sign32


def _rope_call(positions):
    bsz, s = positions.shape
    ts = min(512, s)
    inv64 = 1.0 / (ROPE_THETA ** (jnp.arange(0, HEAD, 2, dtype=F32) / HEAD))
    inv32 = 1.0 / (ROPE_THETA ** (jnp.arange(0, B_ROPE, 2, dtype=F32) / B_ROPE))
    z = jnp.zeros((32,), F32)
    f = jnp.stack([jnp.tile(inv64, 4),
                   jnp.concatenate([z, z, inv32, inv32, z])])
    pos = positions.astype(F32).reshape(bsz, s, 1)
    return pl.pallas_call(
        _rope_kernel,
        grid=(bsz, s // ts),
        in_specs=[pl.BlockSpec((1, ts, 1), lambda b, i: (b, i, 0)),
                  pl.BlockSpec((2, LANES), lambda b, i: (0, 0))],
        out_specs=pl.BlockSpec((1, ts, 512), lambda b, i: (b, i, 0)),
        out_shape=jax.ShapeDtypeStruct((bsz, s, 512), F32),
        compiler_params=_cparams(2),
        name="rope_tables",
    )(pos, f)


def _modulated_norm(h, g, scale, shift):
    y = h * lax.rsqrt(jnp.mean(h * h, axis=-1, keepdims=True) + EPS)
    return (y * g) * (1.0 + scale) + shift


def _ffn_kernel(h_ref, mod_ref, g_ref, wi_ref, wo_ref, o_ref, act_ref, *, mod_row, d_ff, chunk):
    h = h_ref[0]
    u = _modulated_norm(h, g_ref[...], mod_ref[0, mod_row + 1:mod_row + 2, :],
                        mod_ref[0, mod_row:mod_row + 1, :]).astype(BF16)
    for c in range(0, d_ff, chunk):
        gate = _dot(u, wi_ref[:, c:c + chunk])
        up = _dot(u, wi_ref[:, d_ff + c:d_ff + c + chunk])
        act_ref[:, c:c + chunk] = (_silu(gate) * up).astype(BF16)
    y = _dot(act_ref[...], wo_ref[...])
    o_ref[0] = h + (0.5 * mod_ref[0, mod_row + 2:mod_row + 3, :]) * y


def _ffn_call(h, mod, g, w_in, w_out, mod_row):
    bsz, s, d = h.shape
    d_ff = w_out.shape[0]
    ts = min(512, s)
    chunk = 256
    kern = functools.partial(_ffn_kernel, mod_row=mod_row, d_ff=d_ff, chunk=chunk)
    return pl.pallas_call(
        kern,
        grid=(bsz, s // ts),
        in_specs=[pl.BlockSpec((1, ts, d), lambda b, i: (b, i, 0)),
                  pl.BlockSpec((1, N_MOD, d), lambda b, i: (b, 0, 0)),
                  pl.BlockSpec((1, d), lambda b, i: (0, 0)),
                  pl.BlockSpec((d, 2 * d_ff), lambda b, i: (0, 0), pipeline_mode=pl.Buffered(1)),
                  pl.BlockSpec((d_ff, d), lambda b, i: (0, 0), pipeline_mode=pl.Buffered(1))],
        out_specs=pl.BlockSpec((1, ts, d), lambda b, i: (b, i, 0)),
        out_shape=jax.ShapeDtypeStruct(h.shape, F32),
        scratch_shapes=[pltpu.VMEM((ts, d_ff), BF16)],
        compiler_params=_cparams(2),
        name="ffn",
    )(h, mod, g, w_in, w_out)


def _group_mean_sq(x, bd_ref):
    w = x.shape[-1]
    sq = x * x
    hi = sq.astype(BF16)
    lo = (sq - hi.astype(F32)).astype(BF16)
    pieces = []
    step = 256 if w % 256 == 0 else 128
    for c in range(0, w, step):
        bd = bd_ref[0:step, 0:step]
        pieces.append(_dot(hi[:, c:c + step], bd) + _dot(lo[:, c:c + step], bd))
    return pieces[0] if len(pieces) == 1 else jnp.concatenate(pieces, axis=-1)


def _group_norm(x, bd_ref, gain):
    return x * lax.rsqrt(_group_mean_sq(x, bd_ref) + EPS) * gain


def _rope_slab(x, cos, sin, first_half, fwd, bwd):
    rot = jnp.where(first_half, pltpu.roll(x, fwd, axis=1), pltpu.roll(x, bwd, axis=1))
    return x * cos + rot * sin


def _rope64(x, cos, sin):
    lane = lax.broadcasted_iota(jnp.int32, (1, LANES), 1)
    first = (lane & 63) < 32
    out = [_rope_slab(x[:, c:c + LANES], cos, sin, first, 96, 32) for c in range(0, x.shape[-1], LANES)]
    return out[0] if len(out) == 1 else jnp.concatenate(out, axis=-1)


def _rope32(x, cos, sin):
    lane = lax.broadcasted_iota(jnp.int32, (1, LANES), 1)
    first = lane < 80
    out = [_rope_slab(x[:, c:c + LANES], cos, sin, first, 112, 16) for c in range(0, x.shape[-1], LANES)]
    return out[0] if len(out) == 1 else jnp.concatenate(out, axis=-1)


def _with_ones(v):
    lane = lax.broadcasted_iota(jnp.int32, (1, LANES), 1)
    out = []
    for c in range(0, v.shape[-1], LANES):
        slab = v[:, c:c + LANES]
        out += [jnp.where(lane < HEAD, slab, 1.0), jnp.where(lane < HEAD, 1.0, slab)]
    return jnp.concatenate(out, axis=-1).astype(BF16)


def _prep_kernel(h_ref, mod_ref, g_ref, w_ref, wuq_ref, wukv_ref, gv_ref, bd64_ref, bdq_ref, tab_ref,
                 qa_ref, ka_ref, va_ref, qb_ref, kb_ref, vb_ref, qc_ref, qcr_ref,
                 kc_ref, vc_ref, ks_ref, vs_ref, kw_ref, vw_ref, gf_ref):
    u = _modulated_norm(h_ref[0], g_ref[...], mod_ref[0, 4:5, :], mod_ref[0, 3:4, :]).astype(BF16)
    cos64, sin64 = tab_ref[0, :, 0:128], tab_ref[0, :, 128:256]
    cos32, sin32 = tab_ref[0, :, 256:384], tab_ref[0, :, 384:512]

    proj_all = _dot(u, w_ref[...])

    def proj(a, b):
        return proj_all[:, a:b]

    qa = _rope64(_group_norm(proj(O_AQ, O_AK), bd64_ref, gv_ref[0:1, 0:256]), cos64, sin64)
    qa_ref[0] = (qa * (HEAD ** -0.5 * LOG2E)).astype(BF16)
    ka = _rope64(_group_norm(proj(O_AK, O_AV), bd64_ref, gv_ref[1:2, 0:128]), cos64, sin64)
    ka_ref[0] = ka.astype(BF16)
    va_ref[0] = _with_ones(proj(O_AV, O_BCQ))

    cq = proj(O_BCQ, O_BCKV)
    cq = cq * lax.rsqrt(jnp.mean(cq * cq, axis=-1, keepdims=True) + EPS) * gv_ref[8:9, 0:256]
    qb = _group_norm(_dot(cq.astype(BF16), wuq_ref[...]), bdq_ref, gv_ref[5:6, :])
    qb_ref[0] = (_rope32(qb, cos32, sin32) * ((B_NOPE + B_ROPE) ** -0.5 * LOG2E)).astype(BF16)
    ckv = proj(O_BCKV, O_KR)
    ckv = (ckv * lax.rsqrt(jnp.mean(ckv * ckv, axis=-1, keepdims=True) + EPS) * gv_ref[9:10, 0:128]).astype(BF16)
    k_nope = _group_norm(_dot(ckv, wukv_ref[:, 0:512]), bd64_ref, gv_ref[6:7, :])
    vb_ref[0] = _with_ones(_dot(ckv, wukv_ref[:, 512:768]))
    k_pe = _rope32(_group_norm(proj(O_KR, O_CQ), bdq_ref, gv_ref[7:8, 0:128]), cos32, sin32)
    for hh in range(B_HEADS):
        kb_ref[0, :, hh * LANES:(hh + 1) * LANES] = (k_nope[:, hh * LANES:(hh + 1) * LANES] + k_pe).astype(BF16)

    qc = _group_norm(proj(O_CQ, O_KC), bd64_ref, gv_ref[2:3, :])
    qc_ref[0] = (qc * HEAD ** -0.5).astype(BF16)
    qcr_ref[0] = (_rope64(qc, cos64, sin64) * (HEAD ** -0.5 * LOG2E)).astype(BF16)
    kc_ref[0] = proj(O_KC, O_VC)
    vc_ref[0] = proj(O_VC, O_KS)
    ks = _rope64(_group_norm(proj(O_KS, O_VS), bd64_ref, gv_ref[3:4, 0:128]), cos64, sin64)
    ks_ref[0] = ks.astype(BF16)
    vs_ref[0] = _with_ones(proj(O_VS, O_KW))
    kw = _rope64(_group_norm(proj(O_KW, O_VW), bd64_ref, gv_ref[4:5, 0:128]), cos64, sin64)
    kw_ref[0] = kw.astype(BF16)
    vw_ref[0] = _with_ones(proj(O_VW, O_GF))
    gf_ref[0] = proj(O_GF, W_PROJ)


def _prep_call(h, mod, g, w, wuq, wukv, gv, bd64, bdq, tab):
    bsz, s, d = h.shape
    ts = min(512, s)
    widths = [(256, BF16), (128, BF16), (256, BF16), (512, BF16), (512, BF16), (512, BF16),
              (512, BF16), (512, BF16), (128, F32), (128, F32), (128, BF16), (256, BF16),
              (128, BF16), (256, BF16), (128, F32)]

    def const(shape):
        return pl.BlockSpec(shape, lambda b, i: (0,) * len(shape))

    return pl.pallas_call(
        _prep_kernel,
        grid=(bsz, s // ts),
        in_specs=[pl.BlockSpec((1, ts, d), lambda b, i: (b, i, 0)),
                  pl.BlockSpec((1, N_MOD, d), lambda b, i: (b, 0, 0)),
                  const((1, d)), const(w.shape), const(wuq.shape), const(wukv.shape),
                  const(gv.shape), const(bd64.shape), const(bdq.shape),
                  pl.BlockSpec((1, ts, 512), lambda b, i: (b, i, 0))],
        out_specs=[pl.BlockSpec((1, ts, wd), lambda b, i: (b, i, 0)) for wd, _ in widths],
        out_shape=[jax.ShapeDtypeStruct((bsz, s, wd), dt) for wd, dt in widths],
        compiler_params=_cparams(2),
        name="mixer_proj",
    )(h, mod, g, w, wuq, wukv, gv, bd64, bdq, tab)


def _compress_kernel(kc_ref, vc_ref, pe_ref, w1_ref, w2_ref, bd64_ref, g_ref, ko_ref, vo_ref):
    n = kc_ref.shape[1] // C_CMP_STRIDE

    def one(z_ref, idx):
        top = jnp.zeros((n, 2 * C_CMP_HID), F32)
        bot = jnp.zeros((n, 2 * C_CMP_HID), F32)
        for t in range(C_CMP_STRIDE):
            z = z_ref[0, pl.ds(t, n, stride=C_CMP_STRIDE), :]
            cols = slice(t * LANES, (t + 1) * LANES)
            top = top + _dot((z + pe_ref[idx, 0:1, cols]).astype(BF16), w1_ref[idx, 0, cols, :])
            bot = bot + _dot((z + pe_ref[idx, 1:2, cols]).astype(BF16), w1_ref[idx, 1, cols, :])
        pre = top + pltpu.roll(bot, n - 1, axis=0)
        return _dot(_silu(pre).astype(BF16), w2_ref[idx])

    ko_ref[0] = _group_norm(one(kc_ref, 0), bd64_ref, g_ref[...]).astype(BF16)
    vo_ref[0] = one(vc_ref, 1).astype(BF16)


def _compress_call(kc, vc, pe, w1, w2, bd64, g):
    bsz, s, wz = kc.shape
    n = s // C_CMP_STRIDE

    def const(shape):
        return pl.BlockSpec(shape, lambda b: (0,) * len(shape))

    return pl.pallas_call(
        _compress_kernel,
        grid=(bsz,),
        in_specs=[pl.BlockSpec((1, s, wz), lambda b: (b, 0, 0)),
                  pl.BlockSpec((1, s, wz), lambda b: (b, 0, 0)),
                  const(pe.shape), const(w1.shape), const(w2.shape), const(bd64.shape), const(g.shape)],
        out_specs=[pl.BlockSpec((1, n, LANES), lambda b: (b, 0, 0))] * 2,
        out_shape=[jax.ShapeDtypeStruct((bsz, n, LANES), BF16)] * 2,
        compiler_params=_cparams(1),
        name="nsa_compress",
    )(kc, vc, pe, w1, w2, bd64, g)


def _window_kernel(*refs, window, tq, n_slab, n_group, use_sink):
    if use_sink:
        sink_ref, q_ref, k_ref, v_ref, o_ref = refs
    else:
        q_ref, k_ref, v_ref, o_ref = refs
    span = window + tq
    half_masks = _half_masks(BF16)
    lane = lax.broadcasted_iota(jnp.int32, (1, LANES), 1)
    n_sub = q_ref.shape[1] // tq
    for sub in range(n_sub):
        q0 = pl.program_id(1) * (n_sub * tq) + sub * tq
        qrows = slice(sub * tq, (sub + 1) * tq)
        start = pl.multiple_of(jnp.maximum(q0 - window, 0), LANES)
        kwin = k_ref[0, pl.ds(start, span), :]
        rel = (q0 + lax.broadcasted_iota(jnp.int32, (tq, span), 0)) - \
              (start + lax.broadcasted_iota(jnp.int32, (tq, span), 1))
        visible = (rel >= 0) & (rel < window)
        scs = [[_dot_t(q_ref[0, qrows, s * LANES:(s + 1) * LANES] * hm, kwin) for hm in half_masks]
               for s in range(n_slab)]
        for s in range(n_slab):
            outs = []
            for kv in range(2):
                sc = jnp.where(visible, scs[s][kv], NEG_INF)
                m = jnp.max(sc, axis=-1, keepdims=True)
                if use_sink:
                    sink = sink_ref[kv * n_group + s] * LOG2E
                    m = jnp.maximum(m, sink)
                p = jnp.exp2(sc - m)
                acc = _dot(p.astype(BF16), v_ref[0, pl.ds(start, span), kv * LANES:(kv + 1) * LANES])
                denom = jnp.sum(jnp.where(lane == (1 - kv) * HEAD, acc, 0.0), axis=-1, keepdims=True)
                if use_sink:
                    denom = denom + jnp.exp2(sink - m)
                outs.append(acc / denom)
            o_ref[0, qrows, s * LANES:(s + 1) * LANES] = \
                jnp.where(lane < HEAD, outs[0], outs[1]).astype(o_ref.dtype)


def _window_call(q, k, v, sinks, window, out_dtype, name):
    bsz, s, wq = q.shape
    tq = WINDOW_TQ
    tblk = min(WINDOW_BLOCK, s)
    n_slab = wq // LANES
    use_sink = sinks is not None
    kern = functools.partial(_window_kernel, window=window, tq=tq, n_slab=n_slab, n_group=n_slab,
                             use_sink=use_sink)
    in_specs = [pl.BlockSpec((1, tblk, wq), lambda b, i: (b, i, 0)),
                pl.BlockSpec((1, s, LANES), lambda b, i: (b, 0, 0)),
                pl.BlockSpec((1, s, 2 * LANES), lambda b, i: (b, 0, 0))]
    args = [q, k, v]
    if use_sink:
        in_specs = [pl.BlockSpec(memory_space=pltpu.SMEM)] + in_specs
        args = [sinks] + args
    return pl.pallas_call(
        kern,
        grid=(bsz, s // tblk),
        in_specs=in_specs,
        out_specs=pl.BlockSpec((1, tblk, wq), lambda b, i: (b, i, 0)),
        out_shape=jax.ShapeDtypeStruct((bsz, s, wq), out_dtype),
        compiler_params=_cparams(2),
        name=name,
    )(*args)


FLASH_TQ, FLASH_TK = 512, 512
WINDOW_TQ, WINDOW_BLOCK = 128, 512


def _softmax_state(n_heads, tq):
    return ((jnp.full((tq, 1), NEG_INF, F32), jnp.zeros((tq, LANES), F32)),) * n_heads


def _online_step(state, sc, vt):
    m_prev, acc = state
    m_new = jnp.maximum(m_prev, jnp.max(sc, axis=-1, keepdims=True))
    alpha = jnp.exp2(m_prev - m_new)
    p = jnp.exp2(sc - m_new)
    return m_new, alpha * acc + _dot(p.astype(BF16), vt)


def _normalised(acc, value_half):
    lane = lax.broadcasted_iota(jnp.int32, (1, LANES), 1)
    denom_lane = (1 - value_half) * HEAD
    denom = jnp.sum(jnp.where(lane == denom_lane, acc, 0.0), axis=-1, keepdims=True)
    return acc / denom


def _causal_tile(q0, k0, tq, tk):
    return (k0 + lax.broadcasted_iota(jnp.int32, (tq, tk), 1)) <= (q0 + lax.broadcasted_iota(jnp.int32, (tq, tk), 0))


def _mla_kernel(q_ref, k_ref, v_ref, o_ref, *, tq, tk):
    q0 = pl.program_id(1) * tq
    n_full = q0 // tk
    lane = lax.broadcasted_iota(jnp.int32, (1, LANES), 1)

    def step(j, states, diagonal):
        rows = pl.ds(pl.multiple_of(j * tk, tk), tk)
        if diagonal:
            causal = _causal_tile(q0, j * tk, tq, tk)
        scs = [_dot_t(q_ref[0, :, hh * LANES:(hh + 1) * LANES], k_ref[0, rows, hh * LANES:(hh + 1) * LANES])
               for hh in range(B_HEADS)]
        out = []
        for hh in range(B_HEADS):
            sc = jnp.where(causal, scs[hh], NEG_INF) if diagonal else scs[hh]
            out.append(_online_step(states[hh], sc, v_ref[0, rows, hh * LANES:(hh + 1) * LANES]))
        return tuple(out)

    states = lax.fori_loop(0, n_full, lambda j, st: step(j, st, False), _softmax_state(B_HEADS, tq))
    states = step(n_full, states, True)
    for s in range(B_HEADS // 2):
        o_lo = _normalised(states[2 * s][1], 0)
        o_hi = _normalised(states[2 * s + 1][1], 1)
        o_ref[0, :, s * LANES:(s + 1) * LANES] = jnp.where(lane < HEAD, o_lo, o_hi).astype(o_ref.dtype)


def _mla_call(q, k, v):
    bsz, s, _ = q.shape
    tq, tk = FLASH_TQ, FLASH_TK
    kern = functools.partial(_mla_kernel, tq=tq, tk=tk)
    return pl.pallas_call(
        kern,
        grid=(bsz, s // tq),
        in_specs=[pl.BlockSpec((1, tq, 512), lambda b, i: (b, i, 0)),
                  pl.BlockSpec((1, s, 512), lambda b, i: (b, 0, 0)),
                  pl.BlockSpec((1, s, 512), lambda b, i: (b, 0, 0))],
        out_specs=pl.BlockSpec((1, tq, 256), lambda b, i: (b, i, 0)),
        out_shape=jax.ShapeDtypeStruct((bsz, s, 256), BF16),
        compiler_params=_cparams(2),
        name="mla_attention",
    )(q, k, v)


def _cmp_select_kernel(q_ref, k_ref, v_ref, ov_ref, o_ref, sel_ref, *, tq):
    n = k_ref.shape[1]
    t = pl.program_id(1) * tq + lax.broadcasted_iota(jnp.int32, (tq, n), 0)
    j = lax.broadcasted_iota(jnp.int32, (tq, n), 1)
    visible = (j * C_CMP_STRIDE + (C_CMP_BLOCK - 1)) <= t
    kc, vc = k_ref[0], v_ref[0]
    m_lo, m_hi = _half_masks(BF16)
    lane = lax.broadcasted_iota(jnp.int32, (1, LANES), 1)
    p_sum = [jnp.zeros((tq, n), F32), jnp.zeros((tq, n), F32)]
    scs = [[_dot_t(q_ref[0, :, s * LANES:(s + 1) * LANES] * hm, kc) for hm in (m_lo, m_hi)]
           for s in range(C_HEADS // C_KV)]
    for s in range(C_HEADS // C_KV):
        outs = []
        for kv in range(C_KV):
            sc = jnp.where(visible, scs[s][kv], NEG_INF)
            m = jnp.max(sc, axis=-1, keepdims=True)
            p = jnp.where(visible, jnp.exp(sc - m), 0.0)
            denom = jnp.sum(p, axis=-1, keepdims=True)
            p = p * jnp.where(denom > 0.0, 1.0 / denom, 0.0)
            p_sum[kv] = p_sum[kv] + p
            outs.append(_dot(p.astype(BF16), vc))
        o_ref[0, :, s * LANES:(s + 1) * LANES] = jnp.where(lane < HEAD, outs[0], outs[1]).astype(o_ref.dtype)

    imp = jnp.zeros((tq, LANES), F32)
    for kv in range(C_KV):
        hi = p_sum[kv].astype(BF16)
        lo = (p_sum[kv] - hi.astype(F32)).astype(BF16)
        imp = imp + _dot(hi, ov_ref[kv]) + _dot(lo, ov_ref[kv])
    tl = pl.program_id(1) * tq + lax.broadcasted_iota(jnp.int32, (tq, LANES), 0)
    lane_f = lax.broadcasted_iota(jnp.int32, (tq, LANES), 1)
    blk = lane_f & (HEAD - 1)
    cur = tl >> 6
    forced = (blk == 0) | (blk == cur) | (blk == cur - 1)
    future = blk > cur
    score = jnp.where(future, NEG_INF, jnp.where(forced, FORCE_SCORE, imp))
    n_forced = 3
    sel = jnp.where(forced, 1.0, 0.0)
    score = jnp.where(forced, REMOVED, score)
    lane_id = lane_f.astype(F32)
    for kv in range(C_KV):
        in_group = (lane_f >= kv * HEAD) & (lane_f < (kv + 1) * HEAD)
        sc = jnp.where(in_group, score, REMOVED)
        for _ in range(C_N_SEL - n_forced):
            best = jnp.max(sc, axis=-1, keepdims=True)
            first = jnp.min(jnp.where(sc == best, lane_id, 2.0 * LANES), axis=-1, keepdims=True)
            hit = lane_id == first
            sel = jnp.where(hit, 1.0, sel)
            sc = jnp.where(hit, REMOVED, sc)
    sel_ref[0] = jnp.where(future, 0.0, sel).astype(BF16)


def _cmp_select_call(q, kcmp, vcmp, ov):
    bsz, s, wq = q.shape
    n = kcmp.shape[1]
    tq = min(256, s)
    kern = functools.partial(_cmp_select_kernel, tq=tq)
    return pl.pallas_call(
        kern,
        grid=(bsz, s // tq),
        in_specs=[pl.BlockSpec((1, tq, wq), lambda b, i: (b, i, 0)),
                  pl.BlockSpec((1, n, LANES), lambda b, i: (b, 0, 0)),
                  pl.BlockSpec((1, n, LANES), lambda b, i: (b, 0, 0)),
                  pl.BlockSpec(ov.shape, lambda b, i: (0, 0, 0))],
        out_specs=[pl.BlockSpec((1, tq, wq), lambda b, i: (b, i, 0)),
                   pl.BlockSpec((1, tq, LANES), lambda b, i: (b, i, 0))],
        out_shape=[jax.ShapeDtypeStruct((bsz, s, wq), BF16),
                   jax.ShapeDtypeStruct((bsz, s, LANES), BF16)],
        compiler_params=_cparams(2),
        name="nsa_cmp_select",
    )(q, kcmp, vcmp, ov)


def _slc_kernel(q_ref, k_ref, v_ref, sel_ref, e_ref, o_ref, *, tq, tk):
    q0 = pl.program_id(1) * tq
    n_full = q0 // tk
    lane = lax.broadcasted_iota(jnp.int32, (1, LANES), 1)
    m_lo, m_hi = _half_masks(BF16)
    n_slab = C_HEADS // C_KV
    outs = []
    for kv, hm in enumerate((m_lo, m_hi)):
        sel_h = sel_ref[0] * hm
        qm = [q_ref[0, :, s * LANES:(s + 1) * LANES] * hm for s in range(n_slab)]

        def step(j, states, diagonal):
            rows = pl.ds(pl.multiple_of(j * tk, tk), tk)
            kt = k_ref[0, rows, :]
            vt = v_ref[0, rows, kv * LANES:(kv + 1) * LANES]
            chosen = _dot(sel_h, e_ref[j]) > 0.5
            if diagonal:
                chosen = chosen & _causal_tile(q0, j * tk, tq, tk)
            scs = [_dot_t(qm[s], kt) for s in range(n_slab)]
            return tuple(_online_step(states[s], jnp.where(chosen, scs[s], NEG_INF), vt)
                         for s in range(n_slab))

        states = lax.fori_loop(0, n_full, lambda j, st: step(j, st, False), _softmax_state(n_slab, tq))
        states = step(n_full, states, True)
        outs.append([_normalised(states[s][1], kv) for s in range(n_slab)])
    for s in range(n_slab):
        o_ref[0, :, s * LANES:(s + 1) * LANES] = jnp.where(lane < HEAD, outs[0][s], outs[1][s]).astype(o_ref.dtype)


def _slc_call(q, k, v, sel, e):
    bsz, s, wq = q.shape
    tq, tk = FLASH_TQ, e.shape[2]
    kern = functools.partial(_slc_kernel, tq=tq, tk=tk)
    return pl.pallas_call(
        kern,
        grid=(bsz, s // tq),
        in_specs=[pl.BlockSpec((1, tq, wq), lambda b, i: (b, i, 0)),
                  pl.BlockSpec((1, s, LANES), lambda b, i: (b, 0, 0)),
                  pl.BlockSpec((1, s, 2 * LANES), lambda b, i: (b, 0, 0)),
                  pl.BlockSpec((1, tq, LANES), lambda b, i: (b, i, 0)),
                  pl.BlockSpec(e.shape, lambda b, i: (0, 0, 0))],
        out_specs=pl.BlockSpec((1, tq, wq), lambda b, i: (b, i, 0)),
        out_shape=jax.ShapeDtypeStruct((bsz, s, wq), BF16),
        compiler_params=_cparams(2),
        name="nsa_selected",
    )(q, k, v, sel, e)


def _merge_kernel(h_ref, mod_ref, oa_ref, ob_ref, ocmp_ref, oslc_ref, owin_ref, gf_ref, wg_ref, bg_ref,
                  wo_ref, o_ref, cat_ref):
    wc = ocmp_ref.shape[2]
    ts = h_ref.shape[1]
    halves = [slice(r, r + ts // 2) for r in (0, ts // 2)]
    pre = [_dot(_silu(gf_ref[0, r, :]).astype(BF16), wg_ref[...]) for r in halves]
    for r, g in zip(halves, pre):
        gates = jax.nn.sigmoid(g + bg_ref[...])
        oc = (gates[:, 0:wc] * ocmp_ref[0, r, :] + gates[:, wc:2 * wc] * oslc_ref[0, r, :]
              + gates[:, 2 * wc:3 * wc] * owin_ref[0, r, :])
        cat_ref[r, 0:256] = oa_ref[0, r, :]
        cat_ref[r, 256:512] = ob_ref[0, r, :]
        cat_ref[r, 512:512 + wc] = oc.astype(BF16)
        o_ref[0, r, :] = h_ref[0, r, :] + mod_ref[0, 5:6, :] * _dot(cat_ref[r, :], wo_ref[...])


def _merge_call(h, mod, oa, ob, ocmp, oslc, owin, gf, wg, bg, wo):
    bsz, s, d = h.shape
    ts = min(512, s)

    def tok(wd):
        return pl.BlockSpec((1, ts, wd), lambda b, i: (b, i, 0))

    def const(shape):
        return pl.BlockSpec(shape, lambda b, i: (0,) * len(shape))

    return pl.pallas_call(
        _merge_kernel,
        grid=(bsz, s // ts),
        in_specs=[tok(d), pl.BlockSpec((1, N_MOD, d), lambda b, i: (b, 0, 0)),
                  tok(256), tok(256), tok(512), tok(512), tok(512), tok(LANES),
                  const(wg.shape), const(bg.shape), const(wo.shape)],
        out_specs=tok(d),
        out_shape=jax.ShapeDtypeStruct(h.shape, F32),
        scratch_shapes=[pltpu.VMEM((ts, wo.shape[0]), BF16)],
        compiler_params=_cparams(2),
        name="mixer_merge",
    )(h, mod, oa, ob, ocmp, oslc, owin, gf, wg, bg, wo)


def _cols(w, pieces):
    out = [jnp.zeros((w.shape[0], p), w.dtype) if isinstance(p, int) else w[:, p[0]:p[1]] for p in pieces]
    return jnp.concatenate(out, axis=1)


def _head_cols(base, order):
    return [(base + hd * HEAD, base + (hd + 1) * HEAD) for hd in order]


A_ORDER = (0, 2, 1, 3)
C_ORDER = (0, 4, 1, 5, 2, 6, 3, 7)


def _proj_weight(w):
    pieces = (_head_cols(0, A_ORDER) + [(256, 384), (384, 512), (512, 768), (768, 896)]
              + [64, (896, 928), 32] + _head_cols(928, C_ORDER)
              + [(1440 + 128 * i, 1568 + 128 * i) for i in range(6)] + [(2208, 2272), 64])
    return _cols(w, pieces).astype(BF16)


def _mla_weights(w_uq, w_ukv):
    dq = B_NOPE + B_ROPE
    uq = []
    for hd in range(B_HEADS):
        uq += [(hd * dq, hd * dq + dq), 32]
    dk = B_NOPE + B_V
    uk = []
    for hd in range(B_HEADS):
        uk += [(hd * dk, hd * dk + B_NOPE), 64]
    uv = [(hd * dk + B_NOPE, (hd + 1) * dk) for hd in range(B_HEADS)]
    return _cols(w_uq, uq).astype(BF16), _cols(w_ukv, uk + uv).astype(BF16)


def _tile_to(v, reps, width=512):
    t = jnp.tile(v, reps)
    return jnp.pad(t, (0, width - t.shape[0]))


def _gain_rows(a_qk_g, b_q_lat_g, b_kv_lat_g, b_qk_nope_g, b_qk_rope_g, c_qk_g):
    z32, z64 = jnp.zeros((32,), F32), jnp.zeros((64,), F32)
    rows = [_tile_to(a_qk_g[0], 4), _tile_to(a_qk_g[1], 2), _tile_to(c_qk_g[0], 8),
            _tile_to(c_qk_g[2], 2), _tile_to(c_qk_g[3], 2),
            jnp.tile(jnp.concatenate([b_qk_nope_g[0], b_qk_rope_g[0], z32]), 4),
            jnp.tile(jnp.concatenate([b_qk_nope_g[1], z64]), 4),
            _tile_to(jnp.concatenate([z64, b_qk_rope_g[1], z32]), 1),
            _tile_to(b_q_lat_g, 1), _tile_to(b_kv_lat_g, 1)]
    rows += [jnp.zeros((512,), F32)] * (16 - len(rows))
    return jnp.stack(rows)


def _block_diag_consts():
    i = np.arange(256)
    bd64 = np.where((i[:, None] // 64) == (i[None, :] // 64), 1.0 / 64, 0.0)
    j = i % 128
    grp = np.where(j < 64, 0, np.where(j < 96, 1, 2)) + 3 * (i // 128)
    size = np.where(j < 64, 64.0, 32.0)
    bdq = np.where(grp[:, None] == grp[None, :], 1.0 / size[None, :], 0.0)
    return jnp.asarray(bd64, BF16), jnp.asarray(bdq, BF16)


def _compress_weights(pe, w1, w2):
    n_tok = C_CMP_STRIDE
    pe_rows = jnp.broadcast_to(pe.reshape(2, 2, n_tok, 1, HEAD), (2, 2, n_tok, C_KV, HEAD))
    pe_rows = pe_rows.reshape(2, 2, n_tok * C_KV * HEAD)
    w1r = w1.reshape(2, 2, n_tok, HEAD, C_CMP_HID)
    eye = jnp.eye(C_KV, dtype=F32)
    w1big = jnp.einsum('xytdj,kq->xytkdqj', w1r, eye).reshape(2, 2, n_tok * C_KV * HEAD, C_KV * C_CMP_HID)
    w2big = jnp.einsum('xjd,kq->xkjqd', w2, eye).reshape(2, C_KV * C_CMP_HID, C_KV * HEAD)
    return pe_rows, w1big.astype(BF16), w2big.astype(BF16)


def _overlap_const(n_chunk):
    n_slc = n_chunk * C_CMP_STRIDE // C_SLC_BLOCK
    ci = np.arange(n_chunk)[:, None] * C_CMP_STRIDE
    sj = np.arange(n_slc)[None, :] * C_SLC_BLOCK
    ov = ((ci <= sj + C_SLC_BLOCK - 1) & (ci + C_CMP_BLOCK - 1 >= sj)).astype(np.float32)
    ov[n_chunk - 1] = 0.0
    out = np.zeros((C_KV, n_chunk, LANES), np.float32)
    for kv in range(C_KV):
        out[kv, :, kv * HEAD:kv * HEAD + n_slc] = ov
    return jnp.asarray(out, BF16)


def _expand_const(s, tk):
    r = np.arange(LANES)[None, :, None] % HEAD
    c = (np.arange(s // tk)[:, None, None] * tk + np.arange(tk)[None, None, :]) // C_SLC_BLOCK
    return jnp.asarray((r == c).astype(np.float32), BF16)


def _gate_weights(c_gate_w, c_gate_b):
    head_of_lane = np.repeat(np.asarray(C_ORDER), HEAD)
    cols = np.concatenate([head_of_lane * 3 + k for k in range(3)])
    wg = jnp.concatenate([c_gate_w[:, cols], jnp.zeros((LANES - C_GATE_HID, cols.size), F32)], axis=0)
    return wg.astype(BF16), c_gate_b[cols].reshape(1, -1)


def _out_weight(w_out):
    rows = ([(hd * HEAD, (hd + 1) * HEAD) for hd in A_ORDER] + [(256, 512)]
            + [(512 + hd * HEAD, 512 + (hd + 1) * HEAD) for hd in C_ORDER])
    return jnp.concatenate([w_out[a:b] for a, b in rows], axis=0).astype(BF16)


def kernel(x, c, positions, ada_w, ada_b, norm_g, ffn_w_in, ffn_w_out, w_in, w_out, a_sinks, a_qk_g,
           b_q_lat_g, b_kv_lat_g, b_w_uq, b_w_ukv, b_qk_nope_g, b_qk_rope_g, c_qk_g, c_cmp_pe, c_cmp_w1,
           c_cmp_w2, c_gate_w, c_gate_b):
    bsz, s, d = x.shape
    depth = ada_w.shape[0]
    n_chunk = s // C_CMP_STRIDE
    tq = min(256, s)
    assert s % tq == 0 and s % FLASH_TK == 0 and s >= C_WINDOW + tq and s // C_SLC_BLOCK <= HEAD

    mod_all = _ada_call(c, ada_w, ada_b).reshape(depth, bsz, N_MOD, d)
    tab = _rope_call(positions)
    bd64, bdq = _block_diag_consts()
    ov = _overlap_const(n_chunk)
    e = _expand_const(s, FLASH_TK)

    h = x
    for l in range(depth):
        mod = mod_all[l]
        h = _ffn_call(h, mod, norm_g[l, 0:1], ffn_w_in[l, 0].astype(BF16), ffn_w_out[l, 0].astype(BF16), 0)

        wuq, wukv = _mla_weights(b_w_uq[l], b_w_ukv[l])
        gv = _gain_rows(a_qk_g[l], b_q_lat_g[l], b_kv_lat_g[l], b_qk_nope_g[l], b_qk_rope_g[l], c_qk_g[l])
        (qa, ka, va, qb, kb, vb, qc, qcr, kc, vc, ks, vs, kw, vw, gf) = _prep_call(
            h, mod, norm_g[l, 1:2], _proj_weight(w_in[l]), wuq, wukv, gv, bd64, bdq, tab)

        pe_rows, w1big, w2big = _compress_weights(c_cmp_pe[l], c_cmp_w1[l], c_cmp_w2[l])
        kcmp, vcmp = _compress_call(kc, vc, pe_rows, w1big, w2big, bd64,
                                    jnp.tile(c_qk_g[l, 1], 2).reshape(1, LANES))

        oa = _window_call(qa, ka, va, a_sinks[l], A_WINDOW, BF16, "swa_attention")
        ob = _mla_call(qb, kb, vb)
        ocmp, sel = _cmp_select_call(qc, kcmp, vcmp, ov)
        oslc = _slc_call(qcr, ks, vs, sel, e)
        owin = _window_call(qcr, kw, vw, None, C_WINDOW, BF16, "nsa_window")

        wg, bg = _gate_weights(c_gate_w[l], c_gate_b[l])
        h = _merge_call(h, mod, oa, ob, ocmp, oslc, owin, gf, wg, bg, _out_weight(w_out[l]))

        h = _ffn_call(h, mod, norm_g[l, 2:3], ffn_w_in[l, 1].astype(BF16), ffn_w_out[l, 1].astype(BF16), 6)
    return h
```

```python
import functools

import numpy as np
import jax
import jax.numpy as jnp
from jax import lax
from jax.experimental import pallas as pl
from jax.experimental.pallas import tpu as pltpu

F32 = jnp.float32
BF16 = jnp.bfloat16

LANES = 128
HEAD = 64
EPS = 1e-6
ROPE_THETA = 10000.0
NEG_INF = -1e30
FORCE_SCORE = 1e4
REMOVED = -3e38
N_MOD = 9
LOG2E = 1.4426950408889634

A_HEADS, A_KV, A_WINDOW = 4, 2, 128
B_HEADS, B_Q_LORA, B_KV_LORA, B_NOPE, B_ROPE, B_V = 4, 256, 128, 64, 32, 64
C_HEADS, C_KV, C_WINDOW = 8, 2, 256
C_CMP_BLOCK, C_CMP_STRIDE, C_SLC_BLOCK, C_N_SEL = 32, 16, 64, 8
C_CMP_HID, C_GATE_HID = 128, 64

VMEM_LIMIT = 56 * 1024 * 1024

O_AQ, O_AK, O_AV, O_BCQ, O_BCKV, O_KR, O_CQ = 0, 256, 384, 512, 768, 896, 1024
O_KC, O_VC, O_KS, O_VS, O_KW, O_VW, O_GF, W_PROJ = 1536, 1664, 1792, 1920, 2048, 2176, 2304, 2432


def _cparams(n_grid):
    return pltpu.CompilerParams(dimension_semantics=("arbitrary",) * n_grid,
                                vmem_limit_bytes=VMEM_LIMIT)


def _dot(a, b):
    return jnp.dot(a, b, preferred_element_type=F32)


def _dot_t(a, b):
    return lax.dot_general(a, b, (((1,), (1,)), ((), ())), preferred_element_type=F32)


def _silu(x):
    return x * jax.nn.sigmoid(x)


def _half_masks(dtype):
    lane = lax.broadcasted_iota(jnp.int32, (1, LANES), 1)
    lo = jnp.where(lane < HEAD, 1.0, 0.0).astype(dtype)
    hi = jnp.where(lane >= HEAD, 1.0, 0.0).astype(dtype)
    return lo, hi


def _ada_kernel(c_ref, w_ref, b_ref, o_ref):
    cond = _silu(c_ref[...]).astype(BF16)
    o_ref[0] = _dot(cond, w_ref[0].astype(BF16)) + b_ref[0]


def _ada_call(c, ada_w, ada_b):
    depth, d, nm = ada_w.shape
    bsz = c.shape[0]
    tn = 1024
    return pl.pallas_call(
        _ada_kernel,
        grid=(depth, nm // tn),
        in_specs=[pl.BlockSpec((bsz, d), lambda l, j: (0, 0)),
                  pl.BlockSpec((1, d, tn), lambda l, j: (l, 0, j)),
                  pl.BlockSpec((1, 1, tn), lambda l, j: (l, 0, j))],
        out_specs=pl.BlockSpec((1, bsz, tn), lambda l, j: (l, 0, j)),
        out_shape=jax.ShapeDtypeStruct((depth, bsz, nm), F32),
        compiler_params=_cparams(2),
        name="adaln",
    )(c, ada_w, ada_b.reshape(depth, 1, nm))


def _rope_kernel(pos_ref, f_ref, o_ref):
    pos = pos_ref[0]
    lane = lax.broadcasted_iota(jnp.int32, (1, LANES), 1)
    ang64 = pos * f_ref[0:1, :]
    sign64 = jnp.where((lane & 63) < 32, -1.0, 1.0)
    o_ref[0, :, 0:128] = jnp.cos(ang64)
    o_ref[0, :, 128:256] = jnp.sin(ang64) * sign64
    ang32 = pos * f_ref[1:2, :]
    sign32 = jnp.where(lane < 80, -1.0, 1.0)
    o_ref[0, :, 256:384] = jnp.cos(ang32)
    o_ref[0, :, 384:512] = jnp.sin(ang32) * sign32


def _rope_call(positions):
    bsz, s = positions.shape
    ts = min(512, s)
    inv64 = 1.0 / (ROPE_THETA ** (jnp.arange(0, HEAD, 2, dtype=F32) / HEAD))
    inv32 = 1.0 / (ROPE_THETA ** (jnp.arange(0, B_ROPE, 2, dtype=F32) / B_ROPE))
    z = jnp.zeros((32,), F32)
    f = jnp.stack([jnp.tile(inv64, 4),
                   jnp.concatenate([z, z, inv32, inv32, z])])
    pos = positions.astype(F32).reshape(bsz, s, 1)
    return pl.pallas_call(
        _rope_kernel,
        grid=(bsz, s // ts),
        in_specs=[pl.BlockSpec((1, ts, 1), lambda b, i: (b, i, 0)),
                  pl.BlockSpec((2, LANES), lambda b, i: (0, 0))],
        out_specs=pl.BlockSpec((1, ts, 512), lambda b, i: (b, i, 0)),
        out_shape=jax.ShapeDtypeStruct((bsz, s, 512), F32),
        compiler_params=_cparams(2),
        name="rope_tables",
    )(pos, f)


def _modulated_norm(h, g, scale, shift):
    y = h * lax.rsqrt(jnp.mean(h * h, axis=-1, keepdims=True) + EPS)
    return (y * g) * (1.0 + scale) + shift


def _ffn_kernel(h_ref, mod_ref, g_ref, wi_ref, wo_ref, o_ref, act_ref, *, mod_row, d_ff, chunk):
    h = h_ref[0]
    u = _modulated_norm(h, g_ref[...], mod_ref[0, mod_row + 1:mod_row + 2, :],
                        mod_ref[0, mod_row:mod_row + 1, :]).astype(BF16)
    for c in range(0, d_ff, chunk):
        gate = _dot(u, wi_ref[:, c:c + chunk])
        up = _dot(u, wi_ref[:, d_ff + c:d_ff + c + chunk])
        act_ref[:, c:c + chunk] = (_silu(gate) * up).astype(BF16)
    y = _dot(act_ref[...], wo_ref[...])
    o_ref[0] = h + (0.5 * mod_ref[0, mod_row + 2:mod_row + 3, :]) * y


def _ffn_call(h, mod, g, w_in, w_out, mod_row):
    bsz, s, d = h.shape
    d_ff = w_out.shape[0]
    ts = min(1024, s)
    chunk = 256
    kern = functools.partial(_ffn_kernel, mod_row=mod_row, d_ff=d_ff, chunk=chunk)
    return pl.pallas_call(
        kern,
        grid=(bsz, s // ts),
        in_specs=[pl.BlockSpec((1, ts, d), lambda b, i: (b, i, 0)),
                  pl.BlockSpec((1, N_MOD, d), lambda b, i: (b, 0, 0)),
                  pl.BlockSpec((1, d), lambda b, i: (0, 0)),
                  pl.BlockSpec((d, 2 * d_ff), lambda b, i: (0, 0), pipeline_mode=pl.Buffered(1)),
                  pl.BlockSpec((d_ff, d), lambda b, i: (0, 0), pipeline_mode=pl.Buffered(1))],
        out_specs=pl.BlockSpec((1, ts, d), lambda b, i: (b, i, 0)),
        out_shape=jax.ShapeDtypeStruct(h.shape, F32),
        scratch_shapes=[pltpu.VMEM((ts, d_ff), BF16)],
        compiler_params=_cparams(2),
        name="ffn",
    )(h, mod, g, w_in, w_out)


def _group_mean_sq(x, bd_ref):
    w = x.shape[-1]
    sq = x * x
    hi = sq.astype(BF16)
    lo = (sq - hi.astype(F32)).astype(BF16)
    pieces = []
    step = 256 if w % 256 == 0 else 128
    for c in range(0, w, step):
        bd = bd_ref[0:step, 0:step]
        pieces.append(_dot(hi[:, c:c + step], bd) + _dot(lo[:, c:c + step], bd))
    return pieces[0] if len(pieces) == 1 else jnp.concatenate(pieces, axis=-1)


def _group_norm(x, bd_ref, gain):
    return x * lax.rsqrt(_group_mean_sq(x, bd_ref) + EPS) * gain


def _rope_slab(x, cos, sin, first_half, fwd, bwd):
    rot = jnp.where(first_half, pltpu.roll(x, fwd, axis=1), pltpu.roll(x, bwd, axis=1))
    return x * cos + rot * sin


def _rope64(x, cos, sin):
    lane = lax.broadcasted_iota(jnp.int32, (1, LANES), 1)
    first = (lane & 63) < 32
    out = [_rope_slab(x[:, c:c + LANES], cos, sin, first, 96, 32) for c in range(0, x.shape[-1], LANES)]
    return out[0] if len(out) == 1 else jnp.concatenate(out, axis=-1)


def _rope32(x, cos, sin):
    lane = lax.broadcasted_iota(jnp.int32, (1, LANES), 1)
    first = lane < 80
    out = [_rope_slab(x[:, c:c + LANES], cos, sin, first, 112, 16) for c in range(0, x.shape[-1], LANES)]
    return out[0] if len(out) == 1 else jnp.concatenate(out, axis=-1)


def _with_ones(v):
    lane = lax.broadcasted_iota(jnp.int32, (1, LANES), 1)
    out = []
    for c in range(0, v.shape[-1], LANES):
        slab = v[:, c:c + LANES]
        out += [jnp.where(lane < HEAD, slab, 1.0), jnp.where(lane < HEAD, 1.0, slab)]
    return jnp.concatenate(out, axis=-1).astype(BF16)


def _prep_kernel(h_ref, mod_ref, g_ref, w_ref, wuq_ref, wukv_ref, gv_ref, bd64_ref, bdq_ref, tab_ref,
                 qa_ref, ka_ref, va_ref, qb_ref, kb_ref, vb_ref, qc_ref, qcr_ref,
                 kc_ref, vc_ref, ks_ref, vs_ref, kw_ref, vw_ref, gf_ref):
    u = _modulated_norm(h_ref[0], g_ref[...], mod_ref[0, 4:5, :], mod_ref[0, 3:4, :]).astype(BF16)
    cos64, sin64 = tab_ref[0, :, 0:128], tab_ref[0, :, 128:256]
    cos32, sin32 = tab_ref[0, :, 256:384], tab_ref[0, :, 384:512]

    proj_all = _dot(u, w_ref[...])

    def proj(a, b):
        return proj_all[:, a:b]

    qa = _rope64(_group_norm(proj(O_AQ, O_AK), bd64_ref, gv_ref[0:1, 0:256]), cos64, sin64)
    qa_ref[0] = (qa * (HEAD ** -0.5 * LOG2E)).astype(BF16)
    ka = _rope64(_group_norm(proj(O_AK, O_AV), bd64_ref, gv_ref[1:2, 0:128]), cos64, sin64)
    ka_ref[0] = ka.astype(BF16)
    va_ref[0] = _with_ones(proj(O_AV, O_BCQ))

    cq = proj(O_BCQ, O_BCKV)
    cq = cq * lax.rsqrt(jnp.mean(cq * cq, axis=-1, keepdims=True) + EPS) * gv_ref[8:9, 0:256]
    qb = _group_norm(_dot(cq.astype(BF16), wuq_ref[...]), bdq_ref, gv_ref[5:6, :])
    qb_ref[0] = (_rope32(qb, cos32, sin32) * ((B_NOPE + B_ROPE) ** -0.5 * LOG2E)).astype(BF16)
    ckv = proj(O_BCKV, O_KR)
    ckv = (ckv * lax.rsqrt(jnp.mean(ckv * ckv, axis=-1, keepdims=True) + EPS) * gv_ref[9:10, 0:128]).astype(BF16)
    k_nope = _group_norm(_dot(ckv, wukv_ref[:, 0:512]), bd64_ref, gv_ref[6:7, :])
    vb_ref[0] = _with_ones(_dot(ckv, wukv_ref[:, 512:768]))
    k_pe = _rope32(_group_norm(proj(O_KR, O_CQ), bdq_ref, gv_ref[7:8, 0:128]), cos32, sin32)
    for hh in range(B_HEADS):
        kb_ref[0, :, hh * LANES:(hh + 1) * LANES] = (k_nope[:, hh * LANES:(hh + 1) * LANES] + k_pe).astype(BF16)

    qc = _group_norm(proj(O_CQ, O_KC), bd64_ref, gv_ref[2:3, :])
    qc_ref[0] = (qc * HEAD ** -0.5).astype(BF16)
    qcr_ref[0] = (_rope64(qc, cos64, sin64) * (HEAD ** -0.5 * LOG2E)).astype(BF16)
    kc_ref[0] = proj(O_KC, O_VC)
    vc_ref[0] = proj(O_VC, O_KS)
    ks = _rope64(_group_norm(proj(O_KS, O_VS), bd64_ref, gv_ref[3:4, 0:128]), cos64, sin64)
    ks_ref[0] = ks.astype(BF16)
    vs_ref[0] = _with_ones(proj(O_VS, O_KW))
    kw = _rope64(_group_norm(proj(O_KW, O_VW), bd64_ref, gv_ref[4:5, 0:128]), cos64, sin64)
    kw_ref[0] = kw.astype(BF16)
    vw_ref[0] = _with_ones(proj(O_VW, O_GF))
    gf_ref[0] = proj(O_GF, W_PROJ)


def _prep_call(h, mod, g, w, wuq, wukv, gv, bd64, bdq, tab):
    bsz, s, d = h.shape
    ts = min(512, s)
    widths = [(256, BF16), (128, BF16), (256, BF16), (512, BF16), (512, BF16), (512, BF16),
              (512, BF16), (512, BF16), (128, F32), (128, F32), (128, BF16), (256, BF16),
              (128, BF16), (256, BF16), (128, F32)]

    def const(shape):
        return pl.BlockSpec(shape, lambda b, i: (0,) * len(shape))

    return pl.pallas_call(
        _prep_kernel,
        grid=(bsz, s // ts),
        in_specs=[pl.BlockSpec((1, ts, d), lambda b, i: (b, i, 0)),
                  pl.BlockSpec((1, N_MOD, d), lambda b, i: (b, 0, 0)),
                  const((1, d)), const(w.shape), const(wuq.shape), const(wukv.shape),
                  const(gv.shape), const(bd64.shape), const(bdq.shape),
                  pl.BlockSpec((1, ts, 512), lambda b, i: (b, i, 0))],
        out_specs=[pl.BlockSpec((1, ts, wd), lambda b, i: (b, i, 0)) for wd, _ in widths],
        out_shape=[jax.ShapeDtypeStruct((bsz, s, wd), dt) for wd, dt in widths],
        compiler_params=_cparams(2),
        name="mixer_proj",
    )(h, mod, g, w, wuq, wukv, gv, bd64, bdq, tab)


def _compress_kernel(kc_ref, vc_ref, pe_ref, w1_ref, w2_ref, bd64_ref, g_ref, ko_ref, vo_ref):
    n = kc_ref.shape[1] // C_CMP_STRIDE

    def one(z_ref, idx):
        top = jnp.zeros((n, 2 * C_CMP_HID), F32)
        bot = jnp.zeros((n, 2 * C_CMP_HID), F32)
        for t in range(C_CMP_STRIDE):
            z = z_ref[0, pl.ds(t, n, stride=C_CMP_STRIDE), :]
            cols = slice(t * LANES, (t + 1) * LANES)
            top = top + _dot((z + pe_ref[idx, 0:1, cols]).astype(BF16), w1_ref[idx, 0, cols, :])
            bot = bot + _dot((z + pe_ref[idx, 1:2, cols]).astype(BF16), w1_ref[idx, 1, cols, :])
        pre = top + pltpu.roll(bot, n - 1, axis=0)
        return _dot(_silu(pre).astype(BF16), w2_ref[idx])

    ko_ref[0] = _group_norm(one(kc_ref, 0), bd64_ref, g_ref[...]).astype(BF16)
    vo_ref[0] = one(vc_ref, 1).astype(BF16)


def _compress_call(kc, vc, pe, w1, w2, bd64, g):
    bsz, s, wz = kc.shape
    n = s // C_CMP_STRIDE

    def const(shape):
        return pl.BlockSpec(shape, lambda b: (0,) * len(shape))

    return pl.pallas_call(
        _compress_kernel,
        grid=(bsz,),
        in_specs=[pl.BlockSpec((1, s, wz), lambda b: (b, 0, 0)),
                  pl.BlockSpec((1, s, wz), lambda b: (b, 0, 0)),
                  const(pe.shape), const(w1.shape), const(w2.shape), const(bd64.shape), const(g.shape)],
        out_specs=[pl.BlockSpec((1, n, LANES), lambda b: (b, 0, 0))] * 2,
        out_shape=[jax.ShapeDtypeStruct((bsz, n, LANES), BF16)] * 2,
        compiler_params=_cparams(1),
        name="nsa_compress",
    )(kc, vc, pe, w1, w2, bd64, g)


def _window_kernel(*refs, window, tq, n_slab, n_group, use_sink):
    if use_sink:
        sink_ref, q_ref, k_ref, v_ref, o_ref = refs
    else:
        q_ref, k_ref, v_ref, o_ref = refs
    span = window + tq
    half_masks = _half_masks(BF16)
    lane = lax.broadcasted_iota(jnp.int32, (1, LANES), 1)
    n_sub = q_ref.shape[1] // tq
    for sub in range(n_sub):
        q0 = pl.program_id(1) * (n_sub * tq) + sub * tq
        qrows = slice(sub * tq, (sub + 1) * tq)
        start = pl.multiple_of(jnp.maximum(q0 - window, 0), LANES)
        kwin = k_ref[0, pl.ds(start, span), :]
        rel = (q0 + lax.broadcasted_iota(jnp.int32, (tq, span), 0)) - \
              (start + lax.broadcasted_iota(jnp.int32, (tq, span), 1))
        visible = (rel >= 0) & (rel < window)
        scs = [[_dot_t(q_ref[0, qrows, s * LANES:(s + 1) * LANES] * hm, kwin) for hm in half_masks]
               for s in range(n_slab)]
        for s in range(n_slab):
            outs = []
            for kv in range(2):
                sc = jnp.where(visible, scs[s][kv], NEG_INF)
                m = jnp.max(sc, axis=-1, keepdims=True)
                if use_sink:
                    sink = sink_ref[kv * n_group + s] * LOG2E
                    m = jnp.maximum(m, sink)
                p = jnp.exp2(sc - m)
                acc = _dot(p.astype(BF16), v_ref[0, pl.ds(start, span), kv * LANES:(kv + 1) * LANES])
                denom = jnp.sum(jnp.where(lane == (1 - kv) * HEAD, acc, 0.0), axis=-1, keepdims=True)
                if use_sink:
                    denom = denom + jnp.exp2(sink - m)
                outs.append(acc / denom)
            o_ref[0, qrows, s * LANES:(s + 1) * LANES] = \
                jnp.where(lane < HEAD, outs[0], outs[1]).astype(o_ref.dtype)


def _window_call(q, k, v, sinks, window, out_dtype, name):
    bsz, s, wq = q.shape
    tq = WINDOW_TQ
    tblk = min(WINDOW_BLOCK, s)
    n_slab = wq // LANES
    use_sink = sinks is not None
    kern = functools.partial(_window_kernel, window=window, tq=tq, n_slab=n_slab, n_group=n_slab,
                             use_sink=use_sink)
    in_specs = [pl.BlockSpec((1, tblk, wq), lambda b, i: (b, i, 0)),
                pl.BlockSpec((1, s, LANES), lambda b, i: (b, 0, 0)),
                pl.BlockSpec((1, s, 2 * LANES), lambda b, i: (b, 0, 0))]
    args = [q, k, v]
    if use_sink:
        in_specs = [pl.BlockSpec(memory_space=pltpu.SMEM)] + in_specs
        args = [sinks] + args
    return pl.pallas_call(
        kern,
        grid=(bsz, s // tblk),
        in_specs=in_specs,
        out_specs=pl.BlockSpec((1, tblk, wq), lambda b, i: (b, i, 0)),
        out_shape=jax.ShapeDtypeStruct((bsz, s, wq), out_dtype),
        compiler_params=_cparams(2),
        name=name,
    )(*args)


FLASH_TQ, FLASH_TK = 512, 512
WINDOW_TQ, WINDOW_BLOCK = 128, 1024


def _softmax_state(n_heads, tq):
    return ((jnp.full((tq, 1), NEG_INF, F32), jnp.zeros((tq, LANES), F32)),) * n_heads


def _online_step(state, sc, vt):
    m_prev, acc = state
    m_new = jnp.maximum(m_prev, jnp.max(sc, axis=-1, keepdims=True))
    alpha = jnp.exp2(m_prev - m_new)
    p = jnp.exp2(sc - m_new)
    return m_new, alpha * acc + _dot(p.astype(BF16), vt)


def _normalised(acc, value_half):
    lane = lax.broadcasted_iota(jnp.int32, (1, LANES), 1)
    denom_lane = (1 - value_half) * HEAD
    denom = jnp.sum(jnp.where(lane == denom_lane, acc, 0.0), axis=-1, keepdims=True)
    return acc / denom


def _causal_tile(q0, k0, tq, tk):
    return (k0 + lax.broadcasted_iota(jnp.int32, (tq, tk), 1)) <= (q0 + lax.broadcasted_iota(jnp.int32, (tq, tk), 0))


def _mla_kernel(q_ref, k_ref, v_ref, o_ref, *, tq, tk):
    q0 = pl.program_id(1) * tq
    n_full = q0 // tk
    lane = lax.broadcasted_iota(jnp.int32, (1, LANES), 1)

    def step(j, states, diagonal):
        rows = pl.ds(pl.multiple_of(j * tk, tk), tk)
        if diagonal:
            causal = _causal_tile(q0, j * tk, tq, tk)
        scs = [_dot_t(q_ref[0, :, hh * LANES:(hh + 1) * LANES], k_ref[0, rows, hh * LANES:(hh + 1) * LANES])
               for hh in range(B_HEADS)]
        out = []
        for hh in range(B_HEADS):
            sc = jnp.where(causal, scs[hh], NEG_INF) if diagonal else scs[hh]
            out.append(_online_step(states[hh], sc, v_ref[0, rows, hh * LANES:(hh + 1) * LANES]))
        return tuple(out)

    states = lax.fori_loop(0, n_full, lambda j, st: step(j, st, False), _softmax_state(B_HEADS, tq))
    states = step(n_full, states, True)
    for s in range(B_HEADS // 2):
        o_lo = _normalised(states[2 * s][1], 0)
        o_hi = _normalised(states[2 * s + 1][1], 1)
        o_ref[0, :, s * LANES:(s + 1) * LANES] = jnp.where(lane < HEAD, o_lo, o_hi).astype(o_ref.dtype)


def _mla_call(q, k, v):
    bsz, s, _ = q.shape
    tq, tk = FLASH_TQ, FLASH_TK
    kern = functools.partial(_mla_kernel, tq=tq, tk=tk)
    return pl.pallas_call(
        kern,
        grid=(bsz, s // tq),
        in_specs=[pl.BlockSpec((1, tq, 512), lambda b, i: (b, i, 0)),
                  pl.BlockSpec((1, s, 512), lambda b, i: (b, 0, 0)),
                  pl.BlockSpec((1, s, 512), lambda b, i: (b, 0, 0))],
        out_specs=pl.BlockSpec((1, tq, 256), lambda b, i: (b, i, 0)),
        out_shape=jax.ShapeDtypeStruct((bsz, s, 256), BF16),
        compiler_params=_cparams(2),
        name="mla_attention",
    )(q, k, v)


def _cmp_select_kernel(q_ref, k_ref, v_ref, ov_ref, o_ref, sel_ref, *, tq):
    n = k_ref.shape[1]
    t = pl.program_id(1) * tq + lax.broadcasted_iota(jnp.int32, (tq, n), 0)
    j = lax.broadcasted_iota(jnp.int32, (tq, n), 1)
    visible = (j * C_CMP_STRIDE + (C_CMP_BLOCK - 1)) <= t
    kc, vc = k_ref[0], v_ref[0]
    m_lo, m_hi = _half_masks(BF16)
    lane = lax.broadcasted_iota(jnp.int32, (1, LANES), 1)
    p_sum = [jnp.zeros((tq, n), F32), jnp.zeros((tq, n), F32)]
    scs = [[_dot_t(q_ref[0, :, s * LANES:(s + 1) * LANES] * hm, kc) for hm in (m_lo, m_hi)]
           for s in range(C_HEADS // C_KV)]
    for s in range(C_HEADS // C_KV):
        outs = []
        for kv in range(C_KV):
            sc = jnp.where(visible, scs[s][kv], NEG_INF)
            m = jnp.max(sc, axis=-1, keepdims=True)
            p = jnp.where(visible, jnp.exp(sc - m), 0.0)
            denom = jnp.sum(p, axis=-1, keepdims=True)
            p = p * jnp.where(denom > 0.0, 1.0 / denom, 0.0)
            p_sum[kv] = p_sum[kv] + p
            outs.append(_dot(p.astype(BF16), vc))
        o_ref[0, :, s * LANES:(s + 1) * LANES] = jnp.where(lane < HEAD, outs[0], outs[1]).astype(o_ref.dtype)

    imp = jnp.zeros((tq, LANES), F32)
    for kv in range(C_KV):
        hi = p_sum[kv].astype(BF16)
        lo = (p_sum[kv] - hi.astype(F32)).astype(BF16)
        imp = imp + _dot(hi, ov_ref[kv]) + _dot(lo, ov_ref[kv])
    tl = pl.program_id(1) * tq + lax.broadcasted_iota(jnp.int32, (tq, LANES), 0)
    lane_f = lax.broadcasted_iota(jnp.int32, (tq, LANES), 1)
    blk = lane_f & (HEAD - 1)
    cur = tl >> 6
    forced = (blk == 0) | (blk == cur) | (blk == cur - 1)
    future = blk > cur
    score = jnp.where(future, NEG_INF, jnp.where(forced, FORCE_SCORE, imp))
    n_forced = 3
    sel = jnp.where(forced, 1.0, 0.0)
    score = jnp.where(forced, REMOVED, score)
    lane_id = lane_f.astype(F32)
    for kv in range(C_KV):
        in_group = (lane_f >= kv * HEAD) & (lane_f < (kv + 1) * HEAD)
        sc = jnp.where(in_group, score, REMOVED)
        for _ in range(C_N_SEL - n_forced):
            best = jnp.max(sc, axis=-1, keepdims=True)
            first = jnp.min(jnp.where(sc == best, lane_id, 2.0 * LANES), axis=-1, keepdims=True)
            hit = lane_id == first
            sel = jnp.where(hit, 1.0, sel)
            sc = jnp.where(hit, REMOVED, sc)
    sel_ref[0] = jnp.where(future, 0.0, sel).astype(BF16)


def _cmp_select_call(q, kcmp, vcmp, ov):
    bsz, s, wq = q.shape
    n = kcmp.shape[1]
    tq = min(512, s)
    kern = functools.partial(_cmp_select_kernel, tq=tq)
    return pl.pallas_call(
        kern,
        grid=(bsz, s // tq),
        in_specs=[pl.BlockSpec((1, tq, wq), lambda b, i: (b, i, 0)),
                  pl.BlockSpec((1, n, LANES), lambda b, i: (b, 0, 0)),
                  pl.BlockSpec((1, n, LANES), lambda b, i: (b, 0, 0)),
                  pl.BlockSpec(ov.shape, lambda b, i: (0, 0, 0))],
        out_specs=[pl.BlockSpec((1, tq, wq), lambda b, i: (b, i, 0)),
                   pl.BlockSpec((1, tq, LANES), lambda b, i: (b, i, 0))],
        out_shape=[jax.ShapeDtypeStruct((bsz, s, wq), F32),
                   jax.ShapeDtypeStruct((bsz, s, LANES), BF16)],
        compiler_params=_cparams(2),
        name="nsa_cmp_select",
    )(q, kcmp, vcmp, ov)


def _slc_kernel(q_ref, k_ref, v_ref, sel_ref, e_ref, o_ref, *, tq, tk):
    q0 = pl.program_id(1) * tq
    n_full = q0 // tk
    lane = lax.broadcasted_iota(jnp.int32, (1, LANES), 1)
    m_lo, m_hi = _half_masks(BF16)
    n_slab = C_HEADS // C_KV
    outs = []
    for kv, hm in enumerate((m_lo, m_hi)):
        sel_h = sel_ref[0] * hm
        qm = [q_ref[0, :, s * LANES:(s + 1) * LANES] * hm for s in range(n_slab)]

        def step(j, states, diagonal):
            rows = pl.ds(pl.multiple_of(j * tk, tk), tk)
            kt = k_ref[0, rows, :]
            vt = v_ref[0, rows, kv * LANES:(kv + 1) * LANES]
            chosen = _dot(sel_h, e_ref[j]) > 0.5
            if diagonal:
                chosen = chosen & _causal_tile(q0, j * tk, tq, tk)
            scs = [_dot_t(qm[s], kt) for s in range(n_slab)]
            return tuple(_online_step(states[s], jnp.where(chosen, scs[s], NEG_INF), vt)
                         for s in range(n_slab))

        states = lax.fori_loop(0, n_full, lambda j, st: step(j, st, False), _softmax_state(n_slab, tq))
        states = step(n_full, states, True)
        outs.append([_normalised(states[s][1], kv) for s in range(n_slab)])
    for s in range(n_slab):
        o_ref[0, :, s * LANES:(s + 1) * LANES] = jnp.where(lane < HEAD, outs[0][s], outs[1][s]).astype(o_ref.dtype)


def _slc_call(q, k, v, sel, e):
    bsz, s, wq = q.shape
    tq, tk = FLASH_TQ, e.shape[2]
    kern = functools.partial(_slc_kernel, tq=tq, tk=tk)
    return pl.pallas_call(
        kern,
        grid=(bsz, s // tq),
        in_specs=[pl.BlockSpec((1, tq, wq), lambda b, i: (b, i, 0)),
                  pl.BlockSpec((1, s, LANES), lambda b, i: (b, 0, 0)),
                  pl.BlockSpec((1, s, 2 * LANES), lambda b, i: (b, 0, 0)),
                  pl.BlockSpec((1, tq, LANES), lambda b, i: (b, i, 0)),
                  pl.BlockSpec(e.shape, lambda b, i: (0, 0, 0))],
        out_specs=pl.BlockSpec((1, tq, wq), lambda b, i: (b, i, 0)),
        out_shape=jax.ShapeDtypeStruct((bsz, s, wq), F32),
        compiler_params=_cparams(2),
        name="nsa_selected",
    )(q, k, v, sel, e)


def _merge_kernel(h_ref, mod_ref, oa_ref, ob_ref, ocmp_ref, oslc_ref, owin_ref, gf_ref, wg_ref, bg_ref,
                  wo_ref, o_ref, cat_ref):
    wc = ocmp_ref.shape[2]
    ts = h_ref.shape[1]
    halves = [slice(r, r + ts // 2) for r in (0, ts // 2)]
    pre = [_dot(_silu(gf_ref[0, r, :]).astype(BF16), wg_ref[...]) for r in halves]
    for r, g in zip(halves, pre):
        gates = jax.nn.sigmoid(g + bg_ref[...])
        oc = (gates[:, 0:wc] * ocmp_ref[0, r, :] + gates[:, wc:2 * wc] * oslc_ref[0, r, :]
              + gates[:, 2 * wc:3 * wc] * owin_ref[0, r, :])
        cat_ref[r, 0:256] = oa_ref[0, r, :]
        cat_ref[r, 256:512] = ob_ref[0, r, :]
        cat_ref[r, 512:512 + wc] = oc.astype(BF16)
        o_ref[0, r, :] = h_ref[0, r, :] + mod_ref[0, 5:6, :] * _dot(cat_ref[r, :], wo_ref[...])


def _merge_call(h, mod, oa, ob, ocmp, oslc, owin, gf, wg, bg, wo):
    bsz, s, d = h.shape
    ts = min(512, s)

    def tok(wd):
        return pl.BlockSpec((1, ts, wd), lambda b, i: (b, i, 0))

    def const(shape):
        return pl.BlockSpec(shape, lambda b, i: (0,) * len(shape))

    return pl.pallas_call(
        _merge_kernel,
        grid=(bsz, s // ts),
        in_specs=[tok(d), pl.BlockSpec((1, N_MOD, d), lambda b, i: (b, 0, 0)),
                  tok(256), tok(256), tok(512), tok(512), tok(512), tok(LANES),
                  const(wg.shape), const(bg.shape), const(wo.shape)],
        out_specs=tok(d),
        out_shape=jax.ShapeDtypeStruct(h.shape, F32),
        scratch_shapes=[pltpu.VMEM((ts, wo.shape[0]), BF16)],
        compiler_params=_cparams(2),
        name="mixer_merge",
    )(h, mod, oa, ob, ocmp, oslc, owin, gf, wg, bg, wo)


def _cols(w, pieces):
    out = [jnp.zeros((w.shape[0], p), w.dtype) if isinstance(p, int) else w[:, p[0]:p[1]] for p in pieces]
    return jnp.concatenate(out, axis=1)


def _head_cols(base, order):
    return [(base + hd * HEAD, base + (hd + 1) * HEAD) for hd in order]


A_ORDER = (0, 2, 1, 3)
C_ORDER = (0, 4, 1, 5, 2, 6, 3, 7)


def _proj_weight(w):
    pieces = (_head_cols(0, A_ORDER) + [(256, 384), (384, 512), (512, 768), (768, 896)]
              + [64, (896, 928), 32] + _head_cols(928, C_ORDER)
              + [(1440 + 128 * i, 1568 + 128 * i) for i in range(6)] + [(2208, 2272), 64])
    return _cols(w, pieces).astype(BF16)


def _mla_weights(w_uq, w_ukv):
    dq = B_NOPE + B_ROPE
    uq = []
    for hd in range(B_HEADS):
        uq += [(hd * dq, hd * dq + dq), 32]
    dk = B_NOPE + B_V
    uk = []
    for hd in range(B_HEADS):
        uk += [(hd * dk, hd * dk + B_NOPE), 64]
    uv = [(hd * dk + B_NOPE, (hd + 1) * dk) for hd in range(B_HEADS)]
    return _cols(w_uq, uq).astype(BF16), _cols(w_ukv, uk + uv).astype(BF16)


def _tile_to(v, reps, width=512):
    t = jnp.tile(v, reps)
    return jnp.pad(t, (0, width - t.shape[0]))


def _gain_rows(a_qk_g, b_q_lat_g, b_kv_lat_g, b_qk_nope_g, b_qk_rope_g, c_qk_g):
    z32, z64 = jnp.zeros((32,), F32), jnp.zeros((64,), F32)
    rows = [_tile_to(a_qk_g[0], 4), _tile_to(a_qk_g[1], 2), _tile_to(c_qk_g[0], 8),
            _tile_to(c_qk_g[2], 2), _tile_to(c_qk_g[3], 2),
            jnp.tile(jnp.concatenate([b_qk_nope_g[0], b_qk_rope_g[0], z32]), 4),
            jnp.tile(jnp.concatenate([b_qk_nope_g[1], z64]), 4),
            _tile_to(jnp.concatenate([z64, b_qk_rope_g[1], z32]), 1),
            _tile_to(b_q_lat_g, 1), _tile_to(b_kv_lat_g, 1)]
    rows += [jnp.zeros((512,), F32)] * (16 - len(rows))
    return jnp.stack(rows)


def _block_diag_consts():
    i = np.arange(256)
    bd64 = np.where((i[:, None] // 64) == (i[None, :] // 64), 1.0 / 64, 0.0)
    j = i % 128
    grp = np.where(j < 64, 0, np.where(j < 96, 1, 2)) + 3 * (i // 128)
    size = np.where(j < 64, 64.0, 32.0)
    bdq = np.where(grp[:, None] == grp[None, :], 1.0 / size[None, :], 0.0)
    return jnp.asarray(bd64, BF16), jnp.asarray(bdq, BF16)


def _compress_weights(pe, w1, w2):
    n_tok = C_CMP_STRIDE
    pe_rows = jnp.broadcast_to(pe.reshape(2, 2, n_tok, 1, HEAD), (2, 2, n_tok, C_KV, HEAD))
    pe_rows = pe_rows.reshape(2, 2, n_tok * C_KV * HEAD)
    w1r = w1.reshape(2, 2, n_tok, HEAD, C_CMP_HID)
    eye = jnp.eye(C_KV, dtype=F32)
    w1big = jnp.einsum('xytdj,kq->xytkdqj', w1r, eye).reshape(2, 2, n_tok * C_KV * HEAD, C_KV * C_CMP_HID)
    w2big = jnp.einsum('xjd,kq->xkjqd', w2, eye).reshape(2, C_KV * C_CMP_HID, C_KV * HEAD)
    return pe_rows, w1big.astype(BF16), w2big.astype(BF16)


def _overlap_const(n_chunk):
    n_slc = n_chunk * C_CMP_STRIDE // C_SLC_BLOCK
    ci = np.arange(n_chunk)[:, None] * C_CMP_STRIDE
    sj = np.arange(n_slc)[None, :] * C_SLC_BLOCK
    ov = ((ci <= sj + C_SLC_BLOCK - 1) & (ci + C_CMP_BLOCK - 1 >= sj)).astype(np.float32)
    ov[n_chunk - 1] = 0.0
    out = np.zeros((C_KV, n_chunk, LANES), np.float32)
    for kv in range(C_KV):
        out[kv, :, kv * HEAD:kv * HEAD + n_slc] = ov
    return jnp.asarray(out, BF16)


def _expand_const(s, tk):
    r = np.arange(LANES)[None, :, None] % HEAD
    c = (np.arange(s // tk)[:, None, None] * tk + np.arange(tk)[None, None, :]) // C_SLC_BLOCK
    return jnp.asarray((r == c).astype(np.float32), BF16)


def _gate_weights(c_gate_w, c_gate_b):
    head_of_lane = np.repeat(np.asarray(C_ORDER), HEAD)
    cols = np.concatenate([head_of_lane * 3 + k for k in range(3)])
    wg = jnp.concatenate([c_gate_w[:, cols], jnp.zeros((LANES - C_GATE_HID, cols.size), F32)], axis=0)
    return wg.astype(BF16), c_gate_b[cols].reshape(1, -1)


def _out_weight(w_out):
    rows = ([(hd * HEAD, (hd + 1) * HEAD) for hd in A_ORDER] + [(256, 512)]
            + [(512 + hd * HEAD, 512 + (hd + 1) * HEAD) for hd in C_ORDER])
    return jnp.concatenate([w_out[a:b] for a, b in rows], axis=0).astype(BF16)


def kernel(x, c, positions, ada_w, ada_b, norm_g, ffn_w_in, ffn_w_out, w_in, w_out, a_sinks, a_qk_g,
           b_q_lat_g, b_kv_lat_g, b_w_uq, b_w_ukv, b_qk_nope_g, b_qk_rope_g, c_qk_g, c_cmp_pe, c_cmp_w1,
           c_cmp_w2, c_gate_w, c_gate_b):
    bsz, s, d = x.shape
    depth = ada_w.shape[0]
    n_chunk = s // C_CMP_STRIDE
    tq = min(256, s)
    assert s % tq == 0 and s % FLASH_TK == 0 and s >= C_WINDOW + tq and s // C_SLC_BLOCK <= HEAD

    mod_all = _ada_call(c, ada_w, ada_b).reshape(depth, bsz, N_MOD, d)
    tab = _rope_call(positions)
    bd64, bdq = _block_diag_consts()
    ov = _overlap_const(n_chunk)
    e = _expand_const(s, FLASH_TK)

    h = x
    for l in range(depth):
        mod = mod_all[l]
        h = _ffn_call(h, mod, norm_g[l, 0:1], ffn_w_in[l, 0].astype(BF16), ffn_w_out[l, 0].astype(BF16), 0)

        wuq, wukv = _mla_weights(b_w_uq[l], b_w_ukv[l])
        gv = _gain_rows(a_qk_g[l], b_q_lat_g[l], b_kv_lat_g[l], b_qk_nope_g[l], b_qk_rope_g[l], c_qk_g[l])
        (qa, ka, va, qb, kb, vb, qc, qcr, kc, vc, ks, vs, kw, vw, gf) = _prep_call(
            h, mod, norm_g[l, 1:2], _proj_weight(w_in[l]), wuq, wukv, gv, bd64, bdq, tab)

        pe_rows, w1big, w2big = _compress_weights(c_cmp_pe[l], c_cmp_w1[l], c_cmp_w2[l])
        kcmp, vcmp = _compress_call(kc, vc, pe_rows, w1big, w2big, bd64,
                                    jnp.tile(c_qk_g[l, 1], 2).reshape(1, LANES))

        oa = _window_call(qa, ka, va, a_sinks[l], A_WINDOW, BF16, "swa_attention")
        ob = _mla_call(qb, kb, vb)
        ocmp, sel = _cmp_select_call(qc, kcmp, vcmp, ov)
        oslc = _slc_call(qcr, ks, vs, sel, e)
        owin = _window_call(qcr, kw, vw, None, C_WINDOW, F32, "nsa_window")

        wg, bg = _gate_weights(c_gate_w[l], c_gate_b[l])
        h = _merge_call(h, mod, oa, ob, ocmp, oslc, owin, gf, wg, bg, _out_weight(w_out[l]))

        h = _ffn_call(h, mod, norm_g[l, 2:3], ffn_w_in[l, 1].astype(BF16), ffn_w_out[l, 1].astype(BF16), 6)
    return h
```

```python
import functools

import numpy as np
import jax
import jax.numpy as jnp
from jax import lax
from jax.experimental import pallas as pl
from jax.experimental.pallas import tpu as pltpu

F32 = jnp.float32
BF16 = jnp.bfloat16

LANES = 128
HEAD = 64
EPS = 1e-6
ROPE_THETA = 10000.0
NEG_INF = -1e30
FORCE_SCORE = 1e4
REMOVED = -3e38
N_MOD = 9
LOG2E = 1.4426950408889634

A_HEADS, A_KV, A_WINDOW = 4, 2, 128
B_HEADS, B_Q_LORA, B_KV_LORA, B_NOPE, B_ROPE, B_V = 4, 256, 128, 64, 32, 64
C_HEADS, C_KV, C_WINDOW = 8, 2, 256
C_CMP_BLOCK, C_CMP_STRIDE, C_SLC_BLOCK, C_N_SEL = 32, 16, 64, 8
C_CMP_HID, C_GATE_HID = 128, 64

VMEM_LIMIT = 56 * 1024 * 1024

O_AQ, O_AK, O_AV, O_BCQ, O_BCKV, O_KR, O_CQ = 0, 256, 384, 512, 768, 896, 1024
O_KC, O_VC, O_KS, O_VS, O_KW, O_VW, O_GF, W_PROJ = 1536, 1664, 1792, 1920, 2048, 2176, 2304, 2432


def _cparams(n_grid):
    return pltpu.CompilerParams(dimension_semantics=("arbitrary",) * n_grid,
                                vmem_limit_bytes=VMEM_LIMIT)


def _dot(a, b):
    return jnp.dot(a, b, preferred_element_type=F32)


def _dot_t(a, b):
    return lax.dot_general(a, b, (((1,), (1,)), ((), ())), preferred_element_type=F32)


def _silu(x):
    return x * jax.nn.sigmoid(x)


def _half_masks(dtype):
    lane = lax.broadcasted_iota(jnp.int32, (1, LANES), 1)
    lo = jnp.where(lane < HEAD, 1.0, 0.0).astype(dtype)
    hi = jnp.where(lane >= HEAD, 1.0, 0.0).astype(dtype)
    return lo, hi


def _ada_kernel(c_ref, w_ref, b_ref, o_ref):
    cond = _silu(c_ref[...]).astype(BF16)
    o_ref[0] = _dot(cond, w_ref[0].astype(BF16)) + b_ref[0]


def _ada_call(c, ada_w, ada_b):
    depth, d, nm = ada_w.shape
    bsz = c.shape[0]
    tn = 1024
    return pl.pallas_call(
        _ada_kernel,
        grid=(depth, nm // tn),
        in_specs=[pl.BlockSpec((bsz, d), lambda l, j: (0, 0)),
                  pl.BlockSpec((1, d, tn), lambda l, j: (l, 0, j)),
                  pl.BlockSpec((1, 1, tn), lambda l, j: (l, 0, j))],
        out_specs=pl.BlockSpec((1, bsz, tn), lambda l, j: (l, 0, j)),
        out_shape=jax.ShapeDtypeStruct((depth, bsz, nm), F32),
        compiler_params=_cparams(2),
        name="adaln",
    )(c, ada_w, ada_b.reshape(depth, 1, nm))


def _rope_kernel(pos_ref, f_ref, o_ref):
    pos = pos_ref[0]
    lane = lax.broadcasted_iota(jnp.int32, (1, LANES), 1)
    ang64 = pos * f_ref[0:1, :]
    sign64 = jnp.where((lane & 63) < 32, -1.0, 1.0)
    o_ref[0, :, 0:128] = jnp.cos(ang64)
    o_ref[0, :, 128:256] = jnp.sin(ang64) * sign64
    ang32 = pos * f_ref[1:2, :]
    sign32 = jnp.where(lane < 80, -1.0, 1.0)
    o_ref[0, :, 256:384] = jnp.cos(ang32)
    o_ref[0, :, 384:512] = jnp.sin(ang32) * sign32


def _rope_call(positions):
    bsz, s = positions.shape
    ts = min(512, s)
    inv64 = 1.0 / (ROPE_THETA ** (jnp.arange(0, HEAD, 2, dtype=F32) / HEAD))
    inv32 = 1.0 / (ROPE_THETA ** (jnp.arange(0, B_ROPE, 2, dtype=F32) / B_ROPE))
    z = jnp.zeros((32,), F32)
    f = jnp.stack([jnp.tile(inv64, 4),
                   jnp.concatenate([z, z, inv32, inv32, z])])
    pos = positions.astype(F32).reshape(bsz, s, 1)
    return pl.pallas_call(
        _rope_kernel,
        grid=(bsz, s // ts),
        in_specs=[pl.BlockSpec((1, ts, 1), lambda b, i: (b, i, 0)),
                  pl.BlockSpec((2, LANES), lambda b, i: (0, 0))],
        out_specs=pl.BlockSpec((1, ts, 512), lambda b, i: (b, i, 0)),
        out_shape=jax.ShapeDtypeStruct((bsz, s, 512), F32),
        compiler_params=_cparams(2),
        name="rope_tables",
    )(pos, f)


def _modulated_norm(h, g, scale, shift):
    y = h * lax.rsqrt(jnp.mean(h * h, axis=-1, keepdims=True) + EPS)
    return (y * g) * (1.0 + scale) + shift


def _ffn_kernel(h_ref, mod_ref, g_ref, wi_ref, wo_ref, o_ref, act_ref, *, mod_row, d_ff, chunk):
    h = h_ref[0]
    u = _modulated_norm(h, g_ref[...], mod_ref[0, mod_row + 1:mod_row + 2, :],
                        mod_ref[0, mod_row:mod_row + 1, :]).astype(BF16)
    for c in range(0, d_ff, chunk):
        gate = _dot(u, wi_ref[:, c:c + chunk])
        up = _dot(u, wi_ref[:, d_ff + c:d_ff + c + chunk])
        act_ref[:, c:c + chunk] = (_silu(gate) * up).astype(BF16)
    y = _dot(act_ref[...], wo_ref[...])
    o_ref[0] = h + (0.5 * mod_ref[0, mod_row + 2:mod_row + 3, :]) * y


def _ffn_call(h, mod, g, w_in, w_out, mod_row):
    bsz, s, d = h.shape
    d_ff = w_out.shape[0]
    ts = min(1024, s)
    chunk = 256
    kern = functools.partial(_ffn_kernel, mod_row=mod_row, d_ff=d_ff, chunk=chunk)
    return pl.pallas_call(
        kern,
        grid=(bsz, s // ts),
        in_specs=[pl.BlockSpec((1, ts, d), lambda b, i: (b, i, 0)),
                  pl.BlockSpec((1, N_MOD, d), lambda b, i: (b, 0, 0)),
                  pl.BlockSpec((1, d), lambda b, i: (0, 0)),
                  pl.BlockSpec((d, 2 * d_ff), lambda b, i: (0, 0), pipeline_mode=pl.Buffered(1)),
                  pl.BlockSpec((d_ff, d), lambda b, i: (0, 0), pipeline_mode=pl.Buffered(1))],
        out_specs=pl.BlockSpec((1, ts, d), lambda b, i: (b, i, 0)),
        out_shape=jax.ShapeDtypeStruct(h.shape, F32),
        scratch_shapes=[pltpu.VMEM((ts, d_ff), BF16)],
        compiler_params=_cparams(2),
        name="ffn",
    )(h, mod, g, w_in, w_out)


def _group_mean_sq(x, bd_ref):
    w = x.shape[-1]
    sq = (x * x).astype(BF16)
    pieces = []
    step = 256 if w % 256 == 0 else 128
    for c in range(0, w, step):
        pieces.append(_dot(sq[:, c:c + step], bd_ref[0:step, 0:step]))
    return pieces[0] if len(pieces) == 1 else jnp.concatenate(pieces, axis=-1)


def _group_norm(x, bd_ref, gain):
    return x * lax.rsqrt(_group_mean_sq(x, bd_ref) + EPS) * gain


def _rope_slab(x, cos, sin, first_half, fwd, bwd):
    rot = jnp.where(first_half, pltpu.roll(x, fwd, axis=1), pltpu.roll(x, bwd, axis=1))
    return x * cos + rot * sin


def _rope64(x, cos, sin):
    lane = lax.broadcasted_iota(jnp.int32, (1, LANES), 1)
    first = (lane & 63) < 32
    out = [_rope_slab(x[:, c:c + LANES], cos, sin, first, 96, 32) for c in range(0, x.shape[-1], LANES)]
    return out[0] if len(out) == 1 else jnp.concatenate(out, axis=-1)


def _rope32(x, cos, sin):
    lane = lax.broadcasted_iota(jnp.int32, (1, LANES), 1)
    first = lane < 80
    out = [_rope_slab(x[:, c:c + LANES], cos, sin, first, 112, 16) for c in range(0, x.shape[-1], LANES)]
    return out[0] if len(out) == 1 else jnp.concatenate(out, axis=-1)


def _with_ones(v):
    lane = lax.broadcasted_iota(jnp.int32, (1, LANES), 1)
    out = []
    for c in range(0, v.shape[-1], LANES):
        slab = v[:, c:c + LANES]
        out += [jnp.where(lane < HEAD, slab, 1.0), jnp.where(lane < HEAD, 1.0, slab)]
    return jnp.concatenate(out, axis=-1).astype(BF16)


def _prep_kernel(h_ref, mod_ref, g_ref, w_ref, wuq_ref, wukv_ref, gv_ref, bd64_ref, bdq_ref, tab_ref,
                 qa_ref, ka_ref, va_ref, qb_ref, kb_ref, vb_ref, qc_ref, qcr_ref,
                 kc_ref, vc_ref, ks_ref, vs_ref, kw_ref, vw_ref, gf_ref):
    u = _modulated_norm(h_ref[0], g_ref[...], mod_ref[0, 4:5, :], mod_ref[0, 3:4, :]).astype(BF16)
    cos64, sin64 = tab_ref[0, :, 0:128], tab_ref[0, :, 128:256]
    cos32, sin32 = tab_ref[0, :, 256:384], tab_ref[0, :, 384:512]

    proj_all = _dot(u, w_ref[...])

    def proj(a, b):
        return proj_all[:, a:b]

    qa = _rope64(_group_norm(proj(O_AQ, O_AK), bd64_ref, gv_ref[0:1, 0:256]), cos64, sin64)
    qa_ref[0] = (qa * (HEAD ** -0.5 * LOG2E)).astype(BF16)
    ka = _rope64(_group_norm(proj(O_AK, O_AV), bd64_ref, gv_ref[1:2, 0:128]), cos64, sin64)
    ka_ref[0] = ka.astype(BF16)
    va_ref[0] = _with_ones(proj(O_AV, O_BCQ))

    cq = proj(O_BCQ, O_BCKV)
    cq = cq * lax.rsqrt(jnp.mean(cq * cq, axis=-1, keepdims=True) + EPS) * gv_ref[8:9, 0:256]
    qb = _group_norm(_dot(cq.astype(BF16), wuq_ref[...]), bdq_ref, gv_ref[5:6, :])
    qb_ref[0] = (_rope32(qb, cos32, sin32) * ((B_NOPE + B_ROPE) ** -0.5 * LOG2E)).astype(BF16)
    ckv = proj(O_BCKV, O_KR)
    ckv = (ckv * lax.rsqrt(jnp.mean(ckv * ckv, axis=-1, keepdims=True) + EPS) * gv_ref[9:10, 0:128]).astype(BF16)
    k_nope = _group_norm(_dot(ckv, wukv_ref[:, 0:512]), bd64_ref, gv_ref[6:7, :])
    vb_ref[0] = _with_ones(_dot(ckv, wukv_ref[:, 512:768]))
    k_pe = _rope32(_group_norm(proj(O_KR, O_CQ), bdq_ref, gv_ref[7:8, 0:128]), cos32, sin32)
    for hh in range(B_HEADS):
        kb_ref[0, :, hh * LANES:(hh + 1) * LANES] = (k_nope[:, hh * LANES:(hh + 1) * LANES] + k_pe).astype(BF16)

    qc = _group_norm(proj(O_CQ, O_KC), bd64_ref, gv_ref[2:3, :])
    qc_ref[0] = (qc * HEAD ** -0.5).astype(BF16)
    qcr_ref[0] = (_rope64(qc, cos64, sin64) * (HEAD ** -0.5 * LOG2E)).astype(BF16)
    kc_ref[0] = proj(O_KC, O_VC)
    vc_ref[0] = proj(O_VC, O_KS)
    ks = _rope64(_group_norm(proj(O_KS, O_VS), bd64_ref, gv_ref[3:4, 0:128]), cos64, sin64)
    ks_ref[0] = ks.astype(BF16)
    vs_ref[0] = _with_ones(proj(O_VS, O_KW))
    kw = _rope64(_group_norm(proj(O_KW, O_VW), bd64_ref, gv_ref[4:5, 0:128]), cos64, sin64)
    kw_ref[0] = kw.astype(BF16)
    vw_ref[0] = _with_ones(proj(O_VW, O_GF))
    gf_ref[0] = proj(O_GF, W_PROJ)


def _prep_call(h, mod, g, w, wuq, wukv, gv, bd64, bdq, tab):
    bsz, s, d = h.shape
    ts = min(512, s)
    widths = [(256, BF16), (128, BF16), (256, BF16), (512, BF16), (512, BF16), (512, BF16),
              (512, BF16), (512, BF16), (128, F32), (128, F32), (128, BF16), (256, BF16),
              (128, BF16), (256, BF16), (128, F32)]

    def const(shape):
        return pl.BlockSpec(shape, lambda b, i: (0,) * len(shape))

    return pl.pallas_call(
        _prep_kernel,
        grid=(bsz, s // ts),
        in_specs=[pl.BlockSpec((1, ts, d), lambda b, i: (b, i, 0)),
                  pl.BlockSpec((1, N_MOD, d), lambda b, i: (b, 0, 0)),
                  const((1, d)), const(w.shape), const(wuq.shape), const(wukv.shape),
                  const(gv.shape), const(bd64.shape), const(bdq.shape),
                  pl.BlockSpec((1, ts, 512), lambda b, i: (b, i, 0))],
        out_specs=[pl.BlockSpec((1, ts, wd), lambda b, i: (b, i, 0)) for wd, _ in widths],
        out_shape=[jax.ShapeDtypeStruct((bsz, s, wd), dt) for wd, dt in widths],
        compiler_params=_cparams(2),
        name="mixer_proj",
    )(h, mod, g, w, wuq, wukv, gv, bd64, bdq, tab)


def _compress_kernel(kc_ref, vc_ref, pe_ref, w1_ref, w2_ref, bd64_ref, g_ref, ko_ref, vo_ref):
    n = kc_ref.shape[1] // C_CMP_STRIDE

    def one(z_ref, idx):
        top = jnp.zeros((n, 2 * C_CMP_HID), F32)
        bot = jnp.zeros((n, 2 * C_CMP_HID), F32)
        for t in range(C_CMP_STRIDE):
            z = z_ref[0, pl.ds(t, n, stride=C_CMP_STRIDE), :]
            cols = slice(t * LANES, (t + 1) * LANES)
            top = top + _dot((z + pe_ref[idx, 0:1, cols]).astype(BF16), w1_ref[idx, 0, cols, :])
            bot = bot + _dot((z + pe_ref[idx, 1:2, cols]).astype(BF16), w1_ref[idx, 1, cols, :])
        pre = top + pltpu.roll(bot, n - 1, axis=0)
        return _dot(_silu(pre).astype(BF16), w2_ref[idx])

    ko_ref[0] = _group_norm(one(kc_ref, 0), bd64_ref, g_ref[...]).astype(BF16)
    vo_ref[0] = one(vc_ref, 1).astype(BF16)


def _compress_call(kc, vc, pe, w1, w2, bd64, g):
    bsz, s, wz = kc.shape
    n = s // C_CMP_STRIDE

    def const(shape):
        return pl.BlockSpec(shape, lambda b: (0,) * len(shape))

    return pl.pallas_call(
        _compress_kernel,
        grid=(bsz,),
        in_specs=[pl.BlockSpec((1, s, wz), lambda b: (b, 0, 0)),
                  pl.BlockSpec((1, s, wz), lambda b: (b, 0, 0)),
                  const(pe.shape), const(w1.shape), const(w2.shape), const(bd64.shape), const(g.shape)],
        out_specs=[pl.BlockSpec((1, n, LANES), lambda b: (b, 0, 0))] * 2,
        out_shape=[jax.ShapeDtypeStruct((bsz, n, LANES), BF16)] * 2,
        compiler_params=_cparams(1),
        name="nsa_compress",
    )(kc, vc, pe, w1, w2, bd64, g)


def _window_kernel(*refs, window, tq, n_slab, n_group, use_sink):
    if use_sink:
        sink_ref, q_ref, k_ref, v_ref, o_ref = refs
    else:
        q_ref, k_ref, v_ref, o_ref = refs
    span = window + tq
    half_masks = _half_masks(BF16)
    lane = lax.broadcasted_iota(jnp.int32, (1, LANES), 1)
    n_sub = q_ref.shape[1] // tq
    for sub in range(n_sub):
        q0 = pl.program_id(1) * (n_sub * tq) + sub * tq
        qrows = slice(sub * tq, (sub + 1) * tq)
        start = pl.multiple_of(jnp.maximum(q0 - window, 0), LANES)
        kwin = k_ref[0, pl.ds(start, span), :]
        rel = (q0 + lax.broadcasted_iota(jnp.int32, (tq, span), 0)) - \
              (start + lax.broadcasted_iota(jnp.int32, (tq, span), 1))
        visible = (rel >= 0) & (rel < window)
        scs = [[_dot_t(q_ref[0, qrows, s * LANES:(s + 1) * LANES] * hm, kwin) for hm in half_masks]
               for s in range(n_slab)]
        for s in range(n_slab):
            outs = []
            for kv in range(2):
                sc = jnp.where(visible, scs[s][kv], NEG_INF)
                m = jnp.max(sc, axis=-1, keepdims=True)
                if use_sink:
                    sink = sink_ref[kv * n_group + s] * LOG2E
                    m = jnp.maximum(m, sink)
                p = jnp.exp2(sc - m)
                acc = _dot(p.astype(BF16), v_ref[0, pl.ds(start, span), kv * LANES:(kv + 1) * LANES])
                denom = jnp.sum(jnp.where(lane == (1 - kv) * HEAD, acc, 0.0), axis=-1, keepdims=True)
                if use_sink:
                    denom = denom + jnp.exp2(sink - m)
                outs.append(acc / denom)
            o_ref[0, qrows, s * LANES:(s + 1) * LANES] = \
                jnp.where(lane < HEAD, outs[0], outs[1]).astype(o_ref.dtype)


def _window_call(q, k, v, sinks, window, out_dtype, name):
    bsz, s, wq = q.shape
    tq = WINDOW_TQ
    tblk = min(WINDOW_BLOCK, s)
    n_slab = wq // LANES
    use_sink = sinks is not None
    kern = functools.partial(_window_kernel, window=window, tq=tq, n_slab=n_slab, n_group=n_slab,
                             use_sink=use_sink)
    in_specs = [pl.BlockSpec((1, tblk, wq), lambda b, i: (b, i, 0)),
                pl.BlockSpec((1, s, LANES), lambda b, i: (b, 0, 0)),
                pl.BlockSpec((1, s, 2 * LANES), lambda b, i: (b, 0, 0))]
    args = [q, k, v]
    if use_sink:
        in_specs = [pl.BlockSpec(memory_space=pltpu.SMEM)] + in_specs
        args = [sinks] + args
    return pl.pallas_call(
        kern,
        grid=(bsz, s // tblk),
        in_specs=in_specs,
        out_specs=pl.BlockSpec((1, tblk, wq), lambda b, i: (b, i, 0)),
        out_shape=jax.ShapeDtypeStruct((bsz, s, wq), out_dtype),
        compiler_params=_cparams(2),
        name=name,
    )(*args)


FLASH_TQ, FLASH_TK = 512, 512
WINDOW_TQ, WINDOW_BLOCK = 128, 1024


def _softmax_state(n_heads, tq):
    return ((jnp.full((tq, 1), NEG_INF, F32), jnp.zeros((tq, LANES), F32)),) * n_heads


def _online_step(state, sc, vt):
    m_prev, acc = state
    m_new = jnp.maximum(m_prev, jnp.max(sc, axis=-1, keepdims=True))
    alpha = jnp.exp2(m_prev - m_new)
    p = jnp.exp2(sc - m_new)
    return m_new, alpha * acc + _dot(p.astype(BF16), vt)


def _normalised(acc, value_half):
    lane = lax.broadcasted_iota(jnp.int32, (1, LANES), 1)
    denom_lane = (1 - value_half) * HEAD
    denom = jnp.sum(jnp.where(lane == denom_lane, acc, 0.0), axis=-1, keepdims=True)
    return acc / denom


def _causal_tile(q0, k0, tq, tk):
    return (k0 + lax.broadcasted_iota(jnp.int32, (tq, tk), 1)) <= (q0 + lax.broadcasted_iota(jnp.int32, (tq, tk), 0))


def _mla_kernel(q_ref, k_ref, v_ref, o_ref, *, tq, tk):
    q0 = pl.program_id(1) * tq
    n_full = q0 // tk
    lane = lax.broadcasted_iota(jnp.int32, (1, LANES), 1)

    def step(j, states, diagonal):
        rows = pl.ds(pl.multiple_of(j * tk, tk), tk)
        if diagonal:
            causal = _causal_tile(q0, j * tk, tq, tk)
        scs = [_dot_t(q_ref[0, :, hh * LANES:(hh + 1) * LANES], k_ref[0, rows, hh * LANES:(hh + 1) * LANES])
               for hh in range(B_HEADS)]
        out = []
        for hh in range(B_HEADS):
            sc = jnp.where(causal, scs[hh], NEG_INF) if diagonal else scs[hh]
            out.append(_online_step(states[hh], sc, v_ref[0, rows, hh * LANES:(hh + 1) * LANES]))
        return tuple(out)

    states = lax.fori_loop(0, n_full, lambda j, st: step(j, st, False), _softmax_state(B_HEADS, tq))
    states = step(n_full, states, True)
    for s in range(B_HEADS // 2):
        o_lo = _normalised(states[2 * s][1], 0)
        o_hi = _normalised(states[2 * s + 1][1], 1)
        o_ref[0, :, s * LANES:(s + 1) * LANES] = jnp.where(lane < HEAD, o_lo, o_hi).astype(o_ref.dtype)


def _mla_call(q, k, v):
    bsz, s, _ = q.shape
    tq, tk = FLASH_TQ, FLASH_TK
    kern = functools.partial(_mla_kernel, tq=tq, tk=tk)
    return pl.pallas_call(
        kern,
        grid=(bsz, s // tq),
        in_specs=[pl.BlockSpec((1, tq, 512), lambda b, i: (b, i, 0)),
                  pl.BlockSpec((1, s, 512), lambda b, i: (b, 0, 0)),
                  pl.BlockSpec((1, s, 512), lambda b, i: (b, 0, 0))],
        out_specs=pl.BlockSpec((1, tq, 256), lambda b, i: (b, i, 0)),
        out_shape=jax.ShapeDtypeStruct((bsz, s, 256), BF16),
        compiler_params=_cparams(2),
        name="mla_attention",
    )(q, k, v)


def _cmp_select_kernel(q_ref, k_ref, v_ref, ov_ref, o_ref, sel_ref, *, tq):
    n = k_ref.shape[1]
    t = pl.program_id(1) * tq + lax.broadcasted_iota(jnp.int32, (tq, n), 0)
    j = lax.broadcasted_iota(jnp.int32, (tq, n), 1)
    visible = (j * C_CMP_STRIDE + (C_CMP_BLOCK - 1)) <= t
    kc, vc = k_ref[0], v_ref[0]
    m_lo, m_hi = _half_masks(BF16)
    lane = lax.broadcasted_iota(jnp.int32, (1, LANES), 1)
    p_sum = [jnp.zeros((tq, n), F32), jnp.zeros((tq, n), F32)]
    scs = [[_dot_t(q_ref[0, :, s * LANES:(s + 1) * LANES] * hm, kc) for hm in (m_lo, m_hi)]
           for s in range(C_HEADS // C_KV)]
    for s in range(C_HEADS // C_KV):
        outs = []
        for kv in range(C_KV):
            sc = jnp.where(visible, scs[s][kv], NEG_INF)
            m = jnp.max(sc, axis=-1, keepdims=True)
            p = jnp.where(visible, jnp.exp(sc - m), 0.0)
            denom = jnp.sum(p, axis=-1, keepdims=True)
            p = p * jnp.where(denom > 0.0, 1.0 / denom, 0.0)
            p_sum[kv] = p_sum[kv] + p
            outs.append(_dot(p.astype(BF16), vc))
        o_ref[0, :, s * LANES:(s + 1) * LANES] = jnp.where(lane < HEAD, outs[0], outs[1]).astype(o_ref.dtype)

    imp = jnp.zeros((tq, LANES), F32)
    for kv in range(C_KV):
        hi = p_sum[kv].astype(BF16)
        lo = (p_sum[kv] - hi.astype(F32)).astype(BF16)
        imp = imp + _dot(hi, ov_ref[kv]) + _dot(lo, ov_ref[kv])
    tl = pl.program_id(1) * tq + lax.broadcasted_iota(jnp.int32, (tq, LANES), 0)
    lane_f = lax.broadcasted_iota(jnp.int32, (tq, LANES), 1)
    blk = lane_f & (HEAD - 1)
    cur = tl >> 6
    forced = (blk == 0) | (blk == cur) | (blk == cur - 1)
    future = blk > cur
    score = jnp.where(future, NEG_INF, jnp.where(forced, FORCE_SCORE, imp))
    n_forced = 3
    sel = jnp.where(forced, 1.0, 0.0)
    score = jnp.where(forced, REMOVED, score)
    lane_id = lane_f.astype(F32)
    for kv in range(C_KV):
        in_group = (lane_f >= kv * HEAD) & (lane_f < (kv + 1) * HEAD)
        sc = jnp.where(in_group, score, REMOVED)
        for _ in range(C_N_SEL - n_forced):
            best = jnp.max(sc, axis=-1, keepdims=True)
            first = jnp.min(jnp.where(sc == best, lane_id, 2.0 * LANES), axis=-1, keepdims=True)
            hit = lane_id == first
            sel = jnp.where(hit, 1.0, sel)
            sc = jnp.where(hit, REMOVED, sc)
    sel_ref[0] = jnp.where(future, 0.0, sel).astype(BF16)


def _cmp_select_call(q, kcmp, vcmp, ov):
    bsz, s, wq = q.shape
    n = kcmp.shape[1]
    tq = min(512, s)
    kern = functools.partial(_cmp_select_kernel, tq=tq)
    return pl.pallas_call(
        kern,
        grid=(bsz, s // tq),
        in_specs=[pl.BlockSpec((1, tq, wq), lambda b, i: (b, i, 0)),
                  pl.BlockSpec((1, n, LANES), lambda b, i: (b, 0, 0)),
                  pl.BlockSpec((1, n, LANES), lambda b, i: (b, 0, 0)),
                  pl.BlockSpec(ov.shape, lambda b, i: (0, 0, 0))],
        out_specs=[pl.BlockSpec((1, tq, wq), lambda b, i: (b, i, 0)),
                   pl.BlockSpec((1, tq, LANES), lambda b, i: (b, i, 0))],
        out_shape=[jax.ShapeDtypeStruct((bsz, s, wq), F32),
                   jax.ShapeDtypeStruct((bsz, s, LANES), BF16)],
        compiler_params=_cparams(2),
        name="nsa_cmp_select",
    )(q, kcmp, vcmp, ov)


def _slc_kernel(q_ref, k_ref, v_ref, sel_ref, e_ref, o_ref, *, tq, tk):
    q0 = pl.program_id(1) * tq
    n_full = q0 // tk
    lane = lax.broadcasted_iota(jnp.int32, (1, LANES), 1)
    m_lo, m_hi = _half_masks(BF16)
    n_slab = C_HEADS // C_KV
    outs = []
    for kv, hm in enumerate((m_lo, m_hi)):
        sel_h = sel_ref[0] * hm
        qm = [q_ref[0, :, s * LANES:(s + 1) * LANES] * hm for s in range(n_slab)]

        def step(j, states, diagonal):
            rows = pl.ds(pl.multiple_of(j * tk, tk), tk)
            kt = k_ref[0, rows, :]
            vt = v_ref[0, rows, kv * LANES:(kv + 1) * LANES]
            chosen = _dot(sel_h, e_ref[j]) > 0.5
            if diagonal:
                chosen = chosen & _causal_tile(q0, j * tk, tq, tk)
            scs = [_dot_t(qm[s], kt) for s in range(n_slab)]
            return tuple(_online_step(states[s], jnp.where(chosen, scs[s], NEG_INF), vt)
                         for s in range(n_slab))

        states = lax.fori_loop(0, n_full, lambda j, st: step(j, st, False), _softmax_state(n_slab, tq))
        states = step(n_full, states, True)
        outs.append([_normalised(states[s][1], kv) for s in range(n_slab)])
    for s in range(n_slab):
        o_ref[0, :, s * LANES:(s + 1) * LANES] = jnp.where(lane < HEAD, outs[0][s], outs[1][s]).astype(o_ref.dtype)


def _slc_call(q, k, v, sel, e):
    bsz, s, wq = q.shape
    tq, tk = FLASH_TQ, e.shape[2]
    kern = functools.partial(_slc_kernel, tq=tq, tk=tk)
    return pl.pallas_call(
        kern,
        grid=(bsz, s // tq),
        in_specs=[pl.BlockSpec((1, tq, wq), lambda b, i: (b, i, 0)),
                  pl.BlockSpec((1, s, LANES), lambda b, i: (b, 0, 0)),
                  pl.BlockSpec((1, s, 2 * LANES), lambda b, i: (b, 0, 0)),
                  pl.BlockSpec((1, tq, LANES), lambda b, i: (b, i, 0)),
                  pl.BlockSpec(e.shape, lambda b, i: (0, 0, 0))],
        out_specs=pl.BlockSpec((1, tq, wq), lambda b, i: (b, i, 0)),
        out_shape=jax.ShapeDtypeStruct((bsz, s, wq), F32),
        compiler_params=_cparams(2),
        name="nsa_selected",
    )(q, k, v, sel, e)


def _merge_kernel(h_ref, mod_ref, oa_ref, ob_ref, ocmp_ref, oslc_ref, owin_ref, gf_ref, wg_ref, bg_ref,
                  wo_ref, o_ref, cat_ref):
    wc = ocmp_ref.shape[2]
    ts = h_ref.shape[1]
    halves = [slice(r, r + ts // 2) for r in (0, ts // 2)]
    pre = [_dot(_silu(gf_ref[0, r, :]).astype(BF16), wg_ref[...]) for r in halves]
    for r, g in zip(halves, pre):
        gates = jax.nn.sigmoid(g + bg_ref[...])
        oc = (gates[:, 0:wc] * ocmp_ref[0, r, :] + gates[:, wc:2 * wc] * oslc_ref[0, r, :]
              + gates[:, 2 * wc:3 * wc] * owin_ref[0, r, :])
        cat_ref[r, 0:256] = oa_ref[0, r, :]
        cat_ref[r, 256:512] = ob_ref[0, r, :]
        cat_ref[r, 512:512 + wc] = oc.astype(BF16)
        o_ref[0, r, :] = h_ref[0, r, :] + mod_ref[0, 5:6, :] * _dot(cat_ref[r, :], wo_ref[...])


def _merge_call(h, mod, oa, ob, ocmp, oslc, owin, gf, wg, bg, wo):
    bsz, s, d = h.shape
    ts = min(512, s)

    def tok(wd):
        return pl.BlockSpec((1, ts, wd), lambda b, i: (b, i, 0))

    def const(shape):
        return pl.BlockSpec(shape, lambda b, i: (0,) * len(shape))

    return pl.pallas_call(
        _merge_kernel,
        grid=(bsz, s // ts),
        in_specs=[tok(d), pl.BlockSpec((1, N_MOD, d), lambda b, i: (b, 0, 0)),
                  tok(256), tok(256), tok(512), tok(512), tok(512), tok(LANES),
                  const(wg.shape), const(bg.shape), const(wo.shape)],
        out_specs=tok(d),
        out_shape=jax.ShapeDtypeStruct(h.shape, F32),
        scratch_shapes=[pltpu.VMEM((ts, wo.shape[0]), BF16)],
        compiler_params=_cparams(2),
        name="mixer_merge",
    )(h, mod, oa, ob, ocmp, oslc, owin, gf, wg, bg, wo)


def _cols(w, pieces):
    out = [jnp.zeros((w.shape[0], p), w.dtype) if isinstance(p, int) else w[:, p[0]:p[1]] for p in pieces]
    return jnp.concatenate(out, axis=1)


def _head_cols(base, order):
    return [(base + hd * HEAD, base + (hd + 1) * HEAD) for hd in order]


A_ORDER = (0, 2, 1, 3)
C_ORDER = (0, 4, 1, 5, 2, 6, 3, 7)


def _proj_weight(w):
    pieces = (_head_cols(0, A_ORDER) + [(256, 384), (384, 512), (512, 768), (768, 896)]
              + [64, (896, 928), 32] + _head_cols(928, C_ORDER)
              + [(1440 + 128 * i, 1568 + 128 * i) for i in range(6)] + [(2208, 2272), 64])
    return _cols(w, pieces).astype(BF16)


def _mla_weights(w_uq, w_ukv):
    dq = B_NOPE + B_ROPE
    uq = []
    for hd in range(B_HEADS):
        uq += [(hd * dq, hd * dq + dq), 32]
    dk = B_NOPE + B_V
    uk = []
    for hd in range(B_HEADS):
        uk += [(hd * dk, hd * dk + B_NOPE), 64]
    uv = [(hd * dk + B_NOPE, (hd + 1) * dk) for hd in range(B_HEADS)]
    return _cols(w_uq, uq).astype(BF16), _cols(w_ukv, uk + uv).astype(BF16)


def _tile_to(v, reps, width=512):
    t = jnp.tile(v, reps)
    return jnp.pad(t, (0, width - t.shape[0]))


def _gain_rows(a_qk_g, b_q_lat_g, b_kv_lat_g, b_qk_nope_g, b_qk_rope_g, c_qk_g):
    z32, z64 = jnp.zeros((32,), F32), jnp.zeros((64,), F32)
    rows = [_tile_to(a_qk_g[0], 4), _tile_to(a_qk_g[1], 2), _tile_to(c_qk_g[0], 8),
            _tile_to(c_qk_g[2], 2), _tile_to(c_qk_g[3], 2),
            jnp.tile(jnp.concatenate([b_qk_nope_g[0], b_qk_rope_g[0], z32]), 4),
            jnp.tile(jnp.concatenate([b_qk_nope_g[1], z64]), 4),
            _tile_to(jnp.concatenate([z64, b_qk_rope_g[1], z32]), 1),
            _tile_to(b_q_lat_g, 1), _tile_to(b_kv_lat_g, 1)]
    rows += [jnp.zeros((512,), F32)] * (16 - len(rows))
    return jnp.stack(rows)


def _block_diag_consts():
    i = np.arange(256)
    bd64 = np.where((i[:, None] // 64) == (i[None, :] // 64), 1.0 / 64, 0.0)
    j = i % 128
    grp = np.where(j < 64, 0, np.where(j < 96, 1, 2)) + 3 * (i // 128)
    size = np.where(j < 64, 64.0, 32.0)
    bdq = np.where(grp[:, None] == grp[None, :], 1.0 / size[None, :], 0.0)
    return jnp.asarray(bd64, BF16), jnp.asarray(bdq, BF16)


def _compress_weights(pe, w1, w2):
    n_tok = C_CMP_STRIDE
    pe_rows = jnp.broadcast_to(pe.reshape(2, 2, n_tok, 1, HEAD), (2, 2, n_tok, C_KV, HEAD))
    pe_rows = pe_rows.reshape(2, 2, n_tok * C_KV * HEAD)
    w1r = w1.reshape(2, 2, n_tok, HEAD, C_CMP_HID)
    eye = jnp.eye(C_KV, dtype=F32)
    w1big = jnp.einsum('xytdj,kq->xytkdqj', w1r, eye).reshape(2, 2, n_tok * C_KV * HEAD, C_KV * C_CMP_HID)
    w2big = jnp.einsum('xjd,kq->xkjqd', w2, eye).reshape(2, C_KV * C_CMP_HID, C_KV * HEAD)
    return pe_rows, w1big.astype(BF16), w2big.astype(BF16)


def _overlap_const(n_chunk):
    n_slc = n_chunk * C_CMP_STRIDE // C_SLC_BLOCK
    ci = np.arange(n_chunk)[:, None] * C_CMP_STRIDE
    sj = np.arange(n_slc)[None, :] * C_SLC_BLOCK
    ov = ((ci <= sj + C_SLC_BLOCK - 1) & (ci + C_CMP_BLOCK - 1 >= sj)).astype(np.float32)
    ov[n_chunk - 1] = 0.0
    out = np.zeros((C_KV, n_chunk, LANES), np.float32)
    for kv in range(C_KV):
        out[kv, :, kv * HEAD:kv * HEAD + n_slc] = ov
    return jnp.asarray(out, BF16)


def _expand_const(s, tk):
    r = np.arange(LANES)[None, :, None] % HEAD
    c = (np.arange(s // tk)[:, None, None] * tk + np.arange(tk)[None, None, :]) // C_SLC_BLOCK
    return jnp.asarray((r == c).astype(np.float32), BF16)


def _gate_weights(c_gate_w, c_gate_b):
    head_of_lane = np.repeat(np.asarray(C_ORDER), HEAD)
    cols = np.concatenate([head_of_lane * 3 + k for k in range(3)])
    wg = jnp.concatenate([c_gate_w[:, cols], jnp.zeros((LANES - C_GATE_HID, cols.size), F32)], axis=0)
    return wg.astype(BF16), c_gate_b[cols].reshape(1, -1)


def _out_weight(w_out):
    rows = ([(hd * HEAD, (hd + 1) * HEAD) for hd in A_ORDER] + [(256, 512)]
            + [(512 + hd * HEAD, 512 + (hd + 1) * HEAD) for hd in C_ORDER])
    return jnp.concatenate([w_out[a:b] for a, b in rows], axis=0).astype(BF16)


def kernel(x, c, positions, ada_w, ada_b, norm_g, ffn_w_in, ffn_w_out, w_in, w_out, a_sinks, a_qk_g,
           b_q_lat_g, b_kv_lat_g, b_w_uq, b_w_ukv, b_qk_nope_g, b_qk_rope_g, c_qk_g, c_cmp_pe, c_cmp_w1,
           c_cmp_w2, c_gate_w, c_gate_b):
    bsz, s, d = x.shape
    depth = ada_w.shape[0]
    n_chunk = s // C_CMP_STRIDE
    tq = min(256, s)
    assert s % tq == 0 and s % FLASH_TK == 0 and s >= C_WINDOW + tq and s // C_SLC_BLOCK <= HEAD

    mod_all = _ada_call(c, ada_w, ada_b).reshape(depth, bsz, N_MOD, d)
    tab = _rope_call(positions)
    bd64, bdq = _block_diag_consts()
    ov = _overlap_const(n_chunk)
    e = _expand_const(s, FLASH_TK)

    h = x
    for l in range(depth):
        mod = mod_all[l]
        h = _ffn_call(h, mod, norm_g[l, 0:1], ffn_w_in[l, 0].astype(BF16), ffn_w_out[l, 0].astype(BF16), 0)

        wuq, wukv = _mla_weights(b_w_uq[l], b_w_ukv[l])
        gv = _gain_rows(a_qk_g[l], b_q_lat_g[l], b_kv_lat_g[l], b_qk_nope_g[l], b_qk_rope_g[l], c_qk_g[l])
        (qa, ka, va, qb, kb, vb, qc, qcr, kc, vc, ks, vs, kw, vw, gf) = _prep_call(
            h, mod, norm_g[l, 1:2], _proj_weight(w_in[l]), wuq, wukv, gv, bd64, bdq, tab)

        pe_rows, w1big, w2big = _compress_weights(c_cmp_pe[l], c_cmp_w1[l], c_cmp_w2[l])
        kcmp, vcmp = _compress_call(kc, vc, pe_rows, w1big, w2big, bd64,
                                    jnp.tile(c_qk_g[l, 1], 2).reshape(1, LANES))

        oa = _window_call(qa, ka, va, a_sinks[l], A_WINDOW, BF16, "swa_attention")
        ob = _mla_call(qb, kb, vb)
        ocmp, sel = _cmp_select_call(qc, kcmp, vcmp, ov)
        oslc = _slc_call(qcr, ks, vs, sel, e)
        owin = _window_call(qcr, kw, vw, None, C_WINDOW, F32, "nsa_window")

        wg, bg = _gate_weights(c_gate_w[l], c_gate_b[l])
        h = _merge_call(h, mod, oa, ob, ocmp, oslc, owin, gf, wg, bg, _out_weight(w_out[l]))

        h = _ffn_call(h, mod, norm_g[l, 2:3], ffn_w_in[l, 1].astype(BF16), ffn_w_out[l, 1].astype(BF16), 6)
    return h
```

```python
import functools

import numpy as np
import jax
import jax.numpy as jnp
from jax import lax
from jax.experimental import pallas as pl
from jax.experimental.pallas import tpu as pltpu

F32 = jnp.float32
BF16 = jnp.bfloat16

LANES = 128
HEAD = 64
EPS = 1e-6
ROPE_THETA = 10000.0
NEG_INF = -1e30
FORCE_SCORE = 1e4
REMOVED = -3e38
N_MOD = 9
LOG2E = 1.4426950408889634

A_HEADS, A_KV, A_WINDOW = 4, 2, 128
B_HEADS, B_Q_LORA, B_KV_LORA, B_NOPE, B_ROPE, B_V = 4, 256, 128, 64, 32, 64
C_HEADS, C_KV, C_WINDOW = 8, 2, 256
C_CMP_BLOCK, C_CMP_STRIDE, C_SLC_BLOCK, C_N_SEL = 32, 16, 64, 8
C_CMP_HID, C_GATE_HID = 128, 64

VMEM_LIMIT = 56 * 1024 * 1024

O_AQ, O_AK, O_AV, O_BCQ, O_BCKV, O_KR, O_CQ = 0, 256, 384, 512, 768, 896, 1024
O_KC, O_VC, O_KS, O_VS, O_KW, O_VW, O_GF, W_PROJ = 1536, 1664, 1792, 1920, 2048, 2176, 2304, 2432


def _cparams(n_grid):
    return pltpu.CompilerParams(dimension_semantics=("arbitrary",) * n_grid,
                                vmem_limit_bytes=VMEM_LIMIT)


def _dot(a, b):
    return jnp.dot(a, b, preferred_element_type=F32)


def _dot_t(a, b):
    return lax.dot_general(a, b, (((1,), (1,)), ((), ())), preferred_element_type=F32)


def _silu(x):
    return x * jax.nn.sigmoid(x)


def _half_masks(dtype):
    lane = lax.broadcasted_iota(jnp.int32, (1, LANES), 1)
    lo = jnp.where(lane < HEAD, 1.0, 0.0).astype(dtype)
    hi = jnp.where(lane >= HEAD, 1.0, 0.0).astype(dtype)
    return lo, hi


def _ada_kernel(c_ref, w_ref, b_ref, o_ref):
    cond = _silu(c_ref[...]).astype(BF16)
    o_ref[0] = _dot(cond, w_ref[0].astype(BF16)) + b_ref[0]


def _ada_call(c, ada_w, ada_b):
    depth, d, nm = ada_w.shape
    bsz = c.shape[0]
    tn = 1024
    return pl.pallas_call(
        _ada_kernel,
        grid=(depth, nm // tn),
        in_specs=[pl.BlockSpec((bsz, d), lambda l, j: (0, 0)),
                  pl.BlockSpec((1, d, tn), lambda l, j: (l, 0, j)),
                  pl.BlockSpec((1, 1, tn), lambda l, j: (l, 0, j))],
        out_specs=pl.BlockSpec((1, bsz, tn), lambda l, j: (l, 0, j)),
        out_shape=jax.ShapeDtypeStruct((depth, bsz, nm), F32),
        compiler_params=_cparams(2),
        name="adaln",
    )(c, ada_w, ada_b.reshape(depth, 1, nm))


def _rope_kernel(pos_ref, f_ref, o_ref):
    pos = pos_ref[0]
    lane = lax.broadcasted_iota(jnp.int32, (1, LANES), 1)
    ang64 = pos * f_ref[0:1, :]
    sign64 = jnp.where((lane & 63) < 32, -1.0, 1.0)
    o_ref[0, :, 0:128] = jnp.cos(ang64)
    o_ref[0, :, 128:256] = jnp.sin(ang64) * sign64
    ang32 = pos * f_ref[1:2, :]
    sign32 = jnp.where(lane < 80, -1.0, 1.0)
    o_ref[0, :, 256:384] = jnp.cos(ang32)
    o_ref[0, :, 384:512] = jnp.sin(ang32) * sign32


def _rope_call(positions):
    bsz, s = positions.shape
    ts = min(512, s)
    inv64 = 1.0 / (ROPE_THETA ** (jnp.arange(0, HEAD, 2, dtype=F32) / HEAD))
    inv32 = 1.0 / (ROPE_THETA ** (jnp.arange(0, B_ROPE, 2, dtype=F32) / B_ROPE))
    z = jnp.zeros((32,), F32)
    f = jnp.stack([jnp.tile(inv64, 4),
                   jnp.concatenate([z, z, inv32, inv32, z])])
    pos = positions.astype(F32).reshape(bsz, s, 1)
    return pl.pallas_call(
        _rope_kernel,
        grid=(bsz, s // ts),
        in_specs=[pl.BlockSpec((1, ts, 1), lambda b, i: (b, i, 0)),
                  pl.BlockSpec((2, LANES), lambda b, i: (0, 0))],
        out_specs=pl.BlockSpec((1, ts, 512), lambda b, i: (b, i, 0)),
        out_shape=jax.ShapeDtypeStruct((bsz, s, 512), F32),
        compiler_params=_cparams(2),
        name="rope_tables",
    )(pos, f)


def _modulated_norm(h, g, scale, shift):
    y = h * lax.rsqrt(jnp.mean(h * h, axis=-1, keepdims=True) + EPS)
    return (y * g) * (1.0 + scale) + shift


def _ffn_kernel(h_ref, mod_ref, g_ref, wi_ref, wo_ref, o_ref, act_ref, *, mod_row, d_ff, chunk):
    h = h_ref[0]
    u = _modulated_norm(h, g_ref[...], mod_ref[0, mod_row + 1:mod_row + 2, :],
                        mod_ref[0, mod_row:mod_row + 1, :]).astype(BF16)
    for c in range(0, d_ff, chunk):
        gate = _dot(u, wi_ref[:, c:c + chunk])
        up = _dot(u, wi_ref[:, d_ff + c:d_ff + c + chunk])
        act_ref[:, c:c + chunk] = (_silu(gate) * up).astype(BF16)
    y = _dot(act_ref[...], wo_ref[...])
    o_ref[0] = h + (0.5 * mod_ref[0, mod_row + 2:mod_row + 3, :]) * y


def _ffn_call(h, mod, g, w_in, w_out, mod_row):
    bsz, s, d = h.shape
    d_ff = w_out.shape[0]
    ts = min(1024, s)
    chunk = 256
    kern = functools.partial(_ffn_kernel, mod_row=mod_row, d_ff=d_ff, chunk=chunk)
    return pl.pallas_call(
        kern,
        grid=(bsz, s // ts),
        in_specs=[pl.BlockSpec((1, ts, d), lambda b, i: (b, i, 0)),
                  pl.BlockSpec((1, N_MOD, d), lambda b, i: (b, 0, 0)),
                  pl.BlockSpec((1, d), lambda b, i: (0, 0)),
                  pl.BlockSpec((d, 2 * d_ff), lambda b, i: (0, 0), pipeline_mode=pl.Buffered(1)),
                  pl.BlockSpec((d_ff, d), lambda b, i: (0, 0), pipeline_mode=pl.Buffered(1))],
        out_specs=pl.BlockSpec((1, ts, d), lambda b, i: (b, i, 0)),
        out_shape=jax.ShapeDtypeStruct(h.shape, F32),
        scratch_shapes=[pltpu.VMEM((ts, d_ff), BF16)],
        compiler_params=_cparams(2),
        name="ffn",
    )(h, mod, g, w_in, w_out)


def _group_mean_sq(x, bd_ref):
    w = x.shape[-1]
    sq = (x * x).astype(BF16)
    pieces = []
    step = 256 if w % 256 == 0 else 128
    for c in range(0, w, step):
        pieces.append(_dot(sq[:, c:c + step], bd_ref[0:step, 0:step]))
    return pieces[0] if len(pieces) == 1 else jnp.concatenate(pieces, axis=-1)


def _group_norm(x, bd_ref, gain):
    return x * lax.rsqrt(_group_mean_sq(x, bd_ref) + EPS) * gain


def _rope_slab(x, cos, sin, first_half, fwd, bwd):
    rot = jnp.where(first_half, pltpu.roll(x, fwd, axis=1), pltpu.roll(x, bwd, axis=1))
    return x * cos + rot * sin


def _rope64(x, cos, sin):
    lane = lax.broadcasted_iota(jnp.int32, (1, LANES), 1)
    first = (lane & 63) < 32
    out = [_rope_slab(x[:, c:c + LANES], cos, sin, first, 96, 32) for c in range(0, x.shape[-1], LANES)]
    return out[0] if len(out) == 1 else jnp.concatenate(out, axis=-1)


def _rope32(x, cos, sin):
    lane = lax.broadcasted_iota(jnp.int32, (1, LANES), 1)
    first = lane < 80
    out = [_rope_slab(x[:, c:c + LANES], cos, sin, first, 112, 16) for c in range(0, x.shape[-1], LANES)]
    return out[0] if len(out) == 1 else jnp.concatenate(out, axis=-1)


def _with_ones(v):
    lane = lax.broadcasted_iota(jnp.int32, (1, LANES), 1)
    out = []
    for c in range(0, v.shape[-1], LANES):
        slab = v[:, c:c + LANES]
        out += [jnp.where(lane < HEAD, slab, 1.0), jnp.where(lane < HEAD, 1.0, slab)]
    return jnp.concatenate(out, axis=-1).astype(BF16)


def _prep_kernel(h_ref, mod_ref, g_ref, w_ref, wuq_ref, wukv_ref, gv_ref, bd64_ref, bdq_ref, tab_ref,
                 qa_ref, ka_ref, va_ref, qb_ref, kb_ref, vb_ref, qc_ref, qcr_ref,
                 kc_ref, vc_ref, ks_ref, vs_ref, kw_ref, vw_ref, gf_ref):
    u = _modulated_norm(h_ref[0], g_ref[...], mod_ref[0, 4:5, :], mod_ref[0, 3:4, :]).astype(BF16)
    cos64, sin64 = tab_ref[0, :, 0:128], tab_ref[0, :, 128:256]
    cos32, sin32 = tab_ref[0, :, 256:384], tab_ref[0, :, 384:512]

    proj_all = _dot(u, w_ref[...])

    def proj(a, b):
        return proj_all[:, a:b]

    qa = _rope64(_group_norm(proj(O_AQ, O_AK), bd64_ref, gv_ref[0:1, 0:256]), cos64, sin64)
    qa_ref[0] = (qa * (HEAD ** -0.5 * LOG2E)).astype(BF16)
    ka = _rope64(_group_norm(proj(O_AK, O_AV), bd64_ref, gv_ref[1:2, 0:128]), cos64, sin64)
    ka_ref[0] = ka.astype(BF16)
    va_ref[0] = _with_ones(proj(O_AV, O_BCQ))

    cq = proj(O_BCQ, O_BCKV)
    cq = cq * lax.rsqrt(jnp.mean(cq * cq, axis=-1, keepdims=True) + EPS) * gv_ref[8:9, 0:256]
    qb = _group_norm(_dot(cq.astype(BF16), wuq_ref[...]), bdq_ref, gv_ref[5:6, :])
    qb_ref[0] = (_rope32(qb, cos32, sin32) * ((B_NOPE + B_ROPE) ** -0.5 * LOG2E)).astype(BF16)
    ckv = proj(O_BCKV, O_KR)
    ckv = (ckv * lax.rsqrt(jnp.mean(ckv * ckv, axis=-1, keepdims=True) + EPS) * gv_ref[9:10, 0:128]).astype(BF16)
    k_nope = _group_norm(_dot(ckv, wukv_ref[:, 0:512]), bd64_ref, gv_ref[6:7, :])
    vb_ref[0] = _with_ones(_dot(ckv, wukv_ref[:, 512:768]))
    k_pe = _rope32(_group_norm(proj(O_KR, O_CQ), bdq_ref, gv_ref[7:8, 0:128]), cos32, sin32)
    for hh in range(B_HEADS):
        kb_ref[0, :, hh * LANES:(hh + 1) * LANES] = (k_nope[:, hh * LANES:(hh + 1) * LANES] + k_pe).astype(BF16)

    qc = _group_norm(proj(O_CQ, O_KC), bd64_ref, gv_ref[2:3, :])
    qc_ref[0] = (qc * HEAD ** -0.5).astype(BF16)
    qcr_ref[0] = (_rope64(qc, cos64, sin64) * (HEAD ** -0.5 * LOG2E)).astype(BF16)
    kc_ref[0] = proj(O_KC, O_VC)
    vc_ref[0] = proj(O_VC, O_KS)
    ks = _rope64(_group_norm(proj(O_KS, O_VS), bd64_ref, gv_ref[3:4, 0:128]), cos64, sin64)
    ks_ref[0] = ks.astype(BF16)
    vs_ref[0] = _with_ones(proj(O_VS, O_KW))
    kw = _rope64(_group_norm(proj(O_KW, O_VW), bd64_ref, gv_ref[4:5, 0:128]), cos64, sin64)
    kw_ref[0] = kw.astype(BF16)
    vw_ref[0] = _with_ones(proj(O_VW, O_GF))
    gf_ref[0] = proj(O_GF, W_PROJ)


def _prep_call(h, mod, g, w, wuq, wukv, gv, bd64, bdq, tab):
    bsz, s, d = h.shape
    ts = min(512, s)
    widths = [(256, BF16), (128, BF16), (256, BF16), (512, BF16), (512, BF16), (512, BF16),
              (512, BF16), (512, BF16), (128, F32), (128, F32), (128, BF16), (256, BF16),
              (128, BF16), (256, BF16), (128, F32)]

    def const(shape):
        return pl.BlockSpec(shape, lambda b, i: (0,) * len(shape))

    return pl.pallas_call(
        _prep_kernel,
        grid=(bsz, s // ts),
        in_specs=[pl.BlockSpec((1, ts, d), lambda b, i: (b, i, 0)),
                  pl.BlockSpec((1, N_MOD, d), lambda b, i: (b, 0, 0)),
                  const((1, d)), const(w.shape), const(wuq.shape), const(wukv.shape),
                  const(gv.shape), const(bd64.shape), const(bdq.shape),
                  pl.BlockSpec((1, ts, 512), lambda b, i: (b, i, 0))],
        out_specs=[pl.BlockSpec((1, ts, wd), lambda b, i: (b, i, 0)) for wd, _ in widths],
        out_shape=[jax.ShapeDtypeStruct((bsz, s, wd), dt) for wd, dt in widths],
        compiler_params=_cparams(2),
        name="mixer_proj",
    )(h, mod, g, w, wuq, wukv, gv, bd64, bdq, tab)


def _compress_kernel(kc_ref, vc_ref, pe_ref, w1_ref, w2_ref, bd64_ref, g_ref, ko_ref, vo_ref):
    n = kc_ref.shape[1] // C_CMP_STRIDE

    def one(z_ref, idx):
        top = jnp.zeros((n, 2 * C_CMP_HID), F32)
        bot = jnp.zeros((n, 2 * C_CMP_HID), F32)
        for t in range(C_CMP_STRIDE):
            z = z_ref[0, pl.ds(t, n, stride=C_CMP_STRIDE), :]
            cols = slice(t * LANES, (t + 1) * LANES)
            top = top + _dot((z + pe_ref[idx, 0:1, cols]).astype(BF16), w1_ref[idx, 0, cols, :])
            bot = bot + _dot((z + pe_ref[idx, 1:2, cols]).astype(BF16), w1_ref[idx, 1, cols, :])
        pre = top + pltpu.roll(bot, n - 1, axis=0)
        return _dot(_silu(pre).astype(BF16), w2_ref[idx])

    ko_ref[0] = _group_norm(one(kc_ref, 0), bd64_ref, g_ref[...]).astype(BF16)
    vo_ref[0] = one(vc_ref, 1).astype(BF16)


def _compress_call(kc, vc, pe, w1, w2, bd64, g):
    bsz, s, wz = kc.shape
    n = s // C_CMP_STRIDE

    def const(shape):
        return pl.BlockSpec(shape, lambda b: (0,) * len(shape))

    return pl.pallas_call(
        _compress_kernel,
        grid=(bsz,),
        in_specs=[pl.BlockSpec((1, s, wz), lambda b: (b, 0, 0)),
                  pl.BlockSpec((1, s, wz), lambda b: (b, 0, 0)),
                  const(pe.shape), const(w1.shape), const(w2.shape), const(bd64.shape), const(g.shape)],
        out_specs=[pl.BlockSpec((1, n, LANES), lambda b: (b, 0, 0))] * 2,
        out_shape=[jax.ShapeDtypeStruct((bsz, n, LANES), BF16)] * 2,
        compiler_params=_cparams(1),
        name="nsa_compress",
    )(kc, vc, pe, w1, w2, bd64, g)


def _window_kernel(*refs, window, tq, n_slab, n_group, use_sink):
    if use_sink:
        sink_ref, q_ref, k_ref, v_ref, o_ref = refs
    else:
        q_ref, k_ref, v_ref, o_ref = refs
    span = window + tq
    half_masks = _half_masks(BF16)
    lane = lax.broadcasted_iota(jnp.int32, (1, LANES), 1)
    n_sub = q_ref.shape[1] // tq
    for sub in range(n_sub):
        q0 = pl.program_id(1) * (n_sub * tq) + sub * tq
        qrows = slice(sub * tq, (sub + 1) * tq)
        start = pl.multiple_of(jnp.maximum(q0 - window, 0), LANES)
        kwin = k_ref[0, pl.ds(start, span), :]
        rel = (q0 + lax.broadcasted_iota(jnp.int32, (tq, span), 0)) - \
              (start + lax.broadcasted_iota(jnp.int32, (tq, span), 1))
        visible = (rel >= 0) & (rel < window)
        scs = [[_dot_t(q_ref[0, qrows, s * LANES:(s + 1) * LANES] * hm, kwin) for hm in half_masks]
               for s in range(n_slab)]
        for s in range(n_slab):
            outs = []
            for kv in range(2):
                sc = jnp.where(visible, scs[s][kv], NEG_INF)
                m = jnp.max(sc, axis=-1, keepdims=True)
                if use_sink:
                    sink = sink_ref[kv * n_group + s] * LOG2E
                    m = jnp.maximum(m, sink)
                p = jnp.exp2(sc - m)
                acc = _dot(p.astype(BF16), v_ref[0, pl.ds(start, span), kv * LANES:(kv + 1) * LANES])
                denom = jnp.sum(jnp.where(lane == (1 - kv) * HEAD, acc, 0.0), axis=-1, keepdims=True)
                if use_sink:
                    denom = denom + jnp.exp2(sink - m)
                outs.append(acc / denom)
            o_ref[0, qrows, s * LANES:(s + 1) * LANES] = \
                jnp.where(lane < HEAD, outs[0], outs[1]).astype(o_ref.dtype)


def _window_call(q, k, v, sinks, window, out_dtype, name):
    bsz, s, wq = q.shape
    tq = WINDOW_TQ
    tblk = min(WINDOW_BLOCK, s)
    n_slab = wq // LANES
    use_sink = sinks is not None
    kern = functools.partial(_window_kernel, window=window, tq=tq, n_slab=n_slab, n_group=n_slab,
                             use_sink=use_sink)
    in_specs = [pl.BlockSpec((1, tblk, wq), lambda b, i: (b, i, 0)),
                pl.BlockSpec((1, s, LANES), lambda b, i: (b, 0, 0)),
                pl.BlockSpec((1, s, 2 * LANES), lambda b, i: (b, 0, 0))]
    args = [q, k, v]
    if use_sink:
        in_specs = [pl.BlockSpec(memory_space=pltpu.SMEM)] + in_specs
        args = [sinks] + args
    return pl.pallas_call(
        kern,
        grid=(bsz, s // tblk),
        in_specs=in_specs,
        out_specs=pl.BlockSpec((1, tblk, wq), lambda b, i: (b, i, 0)),
        out_shape=jax.ShapeDtypeStruct((bsz, s, wq), out_dtype),
        compiler_params=_cparams(2),
        name=name,
    )(*args)


FLASH_TQ, FLASH_TK = 512, 512
WINDOW_TQ, WINDOW_BLOCK = 128, 1024


def _softmax_state(n_heads, tq):
    return ((jnp.full((tq, 1), NEG_INF, F32), jnp.zeros((tq, LANES), F32)),) * n_heads


def _online_step(state, sc, vt):
    m_prev, acc = state
    m_new = jnp.maximum(m_prev, jnp.max(sc, axis=-1, keepdims=True))
    alpha = jnp.exp2(m_prev - m_new)
    p = jnp.exp2(sc - m_new)
    return m_new, alpha * acc + _dot(p.astype(BF16), vt)


def _normalised(acc, value_half):
    lane = lax.broadcasted_iota(jnp.int32, (1, LANES), 1)
    denom_lane = (1 - value_half) * HEAD
    denom = jnp.sum(jnp.where(lane == denom_lane, acc, 0.0), axis=-1, keepdims=True)
    return acc / denom


def _causal_tile(q0, k0, tq, tk):
    return (k0 + lax.broadcasted_iota(jnp.int32, (tq, tk), 1)) <= (q0 + lax.broadcasted_iota(jnp.int32, (tq, tk), 0))


def _mla_kernel(q_ref, k_ref, v_ref, o_ref, *, tq, tk):
    q0 = pl.program_id(1) * tq
    n_full = q0 // tk
    lane = lax.broadcasted_iota(jnp.int32, (1, LANES), 1)

    def tile(states, qsl, k0, nk, diagonal):
        rows = pl.ds(pl.multiple_of(k0, nk), nk)
        if diagonal:
            causal = _causal_tile(q0 + qsl.start, k0, qsl.stop - qsl.start, nk)
        scs = [_dot_t(q_ref[0, qsl, hh * LANES:(hh + 1) * LANES], k_ref[0, rows, hh * LANES:(hh + 1) * LANES])
               for hh in range(B_HEADS)]
        out = []
        for hh in range(B_HEADS):
            sc = jnp.where(causal, scs[hh], NEG_INF) if diagonal else scs[hh]
            out.append(_online_step(states[hh], sc, v_ref[0, rows, hh * LANES:(hh + 1) * LANES]))
        return tuple(out)

    half = tq // 2
    states = lax.fori_loop(0, n_full, lambda j, st: tile(st, slice(0, tq), j * tk, tk, False),
                           _softmax_state(B_HEADS, tq))
    states = tile(states, slice(0, tq), q0, half, True)
    lower = tile(tuple((m[half:], acc[half:]) for m, acc in states), slice(half, tq), q0 + half, half, True)
    states = tuple((jnp.concatenate([m[:half], ml], axis=0), jnp.concatenate([acc[:half], al], axis=0))
                   for (m, acc), (ml, al) in zip(states, lower))
    for s in range(B_HEADS // 2):
        o_lo = _normalised(states[2 * s][1], 0)
        o_hi = _normalised(states[2 * s + 1][1], 1)
        o_ref[0, :, s * LANES:(s + 1) * LANES] = jnp.where(lane < HEAD, o_lo, o_hi).astype(o_ref.dtype)


def _mla_call(q, k, v):
    bsz, s, _ = q.shape
    tq, tk = FLASH_TQ, FLASH_TK
    kern = functools.partial(_mla_kernel, tq=tq, tk=tk)
    return pl.pallas_call(
        kern,
        grid=(bsz, s // tq),
        in_specs=[pl.BlockSpec((1, tq, 512), lambda b, i: (b, i, 0)),
                  pl.BlockSpec((1, s, 512), lambda b, i: (b, 0, 0)),
                  pl.BlockSpec((1, s, 512), lambda b, i: (b, 0, 0))],
        out_specs=pl.BlockSpec((1, tq, 256), lambda b, i: (b, i, 0)),
        out_shape=jax.ShapeDtypeStruct((bsz, s, 256), BF16),
        compiler_params=_cparams(2),
        name="mla_attention",
    )(q, k, v)


def _cmp_select_kernel(q_ref, k_ref, v_ref, ov_ref, o_ref, sel_ref, *, tq):
    n = k_ref.shape[1]
    t = pl.program_id(1) * tq + lax.broadcasted_iota(jnp.int32, (tq, n), 0)
    j = lax.broadcasted_iota(jnp.int32, (tq, n), 1)
    visible = (j * C_CMP_STRIDE + (C_CMP_BLOCK - 1)) <= t
    kc, vc = k_ref[0], v_ref[0]
    m_lo, m_hi = _half_masks(BF16)
    lane = lax.broadcasted_iota(jnp.int32, (1, LANES), 1)
    p_sum = [jnp.zeros((tq, n), F32), jnp.zeros((tq, n), F32)]
    scs = [[_dot_t(q_ref[0, :, s * LANES:(s + 1) * LANES] * hm, kc) for hm in (m_lo, m_hi)]
           for s in range(C_HEADS // C_KV)]
    for s in range(C_HEADS // C_KV):
        outs = []
        for kv in range(C_KV):
            sc = jnp.where(visible, scs[s][kv], NEG_INF)
            m = jnp.max(sc, axis=-1, keepdims=True)
            p = jnp.where(visible, jnp.exp(sc - m), 0.0)
            denom = jnp.sum(p, axis=-1, keepdims=True)
            p = p * jnp.where(denom > 0.0, 1.0 / denom, 0.0)
            p_sum[kv] = p_sum[kv] + p
            outs.append(_dot(p.astype(BF16), vc))
        o_ref[0, :, s * LANES:(s + 1) * LANES] = jnp.where(lane < HEAD, outs[0], outs[1]).astype(o_ref.dtype)

    imp = jnp.zeros((tq, LANES), F32)
    for kv in range(C_KV):
        hi = p_sum[kv].astype(BF16)
        lo = (p_sum[kv] - hi.astype(F32)).astype(BF16)
        imp = imp + _dot(hi, ov_ref[kv]) + _dot(lo, ov_ref[kv])
    tl = pl.program_id(1) * tq + lax.broadcasted_iota(jnp.int32, (tq, LANES), 0)
    lane_f = lax.broadcasted_iota(jnp.int32, (tq, LANES), 1)
    blk = lane_f & (HEAD - 1)
    cur = tl >> 6
    forced = (blk == 0) | (blk == cur) | (blk == cur - 1)
    future = blk > cur
    score = jnp.where(future, NEG_INF, jnp.where(forced, FORCE_SCORE, imp))
    n_forced = 3
    sel = jnp.where(forced, 1.0, 0.0)
    score = jnp.where(forced, REMOVED, score)
    lane_id = lane_f.astype(F32)
    for kv in range(C_KV):
        in_group = (lane_f >= kv * HEAD) & (lane_f < (kv + 1) * HEAD)
        sc = jnp.where(in_group, score, REMOVED)
        for _ in range(C_N_SEL - n_forced):
            best = jnp.max(sc, axis=-1, keepdims=True)
            first = jnp.min(jnp.where(sc == best, lane_id, 2.0 * LANES), axis=-1, keepdims=True)
            hit = lane_id == first
            sel = jnp.where(hit, 1.0, sel)
            sc = jnp.where(hit, REMOVED, sc)
    sel_ref[0] = jnp.where(future, 0.0, sel).astype(BF16)


def _cmp_select_call(q, kcmp, vcmp, ov):
    bsz, s, wq = q.shape
    n = kcmp.shape[1]
    tq = min(512, s)
    kern = functools.partial(_cmp_select_kernel, tq=tq)
    return pl.pallas_call(
        kern,
        grid=(bsz, s // tq),
        in_specs=[pl.BlockSpec((1, tq, wq), lambda b, i: (b, i, 0)),
                  pl.BlockSpec((1, n, LANES), lambda b, i: (b, 0, 0)),
                  pl.BlockSpec((1, n, LANES), lambda b, i: (b, 0, 0)),
                  pl.BlockSpec(ov.shape, lambda b, i: (0, 0, 0))],
        out_specs=[pl.BlockSpec((1, tq, wq), lambda b, i: (b, i, 0)),
                   pl.BlockSpec((1, tq, LANES), lambda b, i: (b, i, 0))],
        out_shape=[jax.ShapeDtypeStruct((bsz, s, wq), F32),
                   jax.ShapeDtypeStruct((bsz, s, LANES), BF16)],
        compiler_params=_cparams(2),
        name="nsa_cmp_select",
    )(q, kcmp, vcmp, ov)


def _slc_kernel(q_ref, k_ref, v_ref, sel_ref, e_ref, o_ref, *, tq, tk):
    q0 = pl.program_id(1) * tq
    n_full = q0 // tk
    lane = lax.broadcasted_iota(jnp.int32, (1, LANES), 1)
    m_lo, m_hi = _half_masks(BF16)
    n_slab = C_HEADS // C_KV
    outs = []
    for kv, hm in enumerate((m_lo, m_hi)):
        sel_h = sel_ref[0] * hm
        qm = [q_ref[0, :, s * LANES:(s + 1) * LANES] * hm for s in range(n_slab)]

        def tile(states, qsl, j, c0, nk, diagonal):
            rows = pl.ds(pl.multiple_of(j * tk + c0, nk), nk)
            kt = k_ref[0, rows, :]
            vt = v_ref[0, rows, kv * LANES:(kv + 1) * LANES]
            chosen = _dot(sel_h[qsl, :], e_ref[j, :, c0:c0 + nk]) > 0.5
            if diagonal:
                chosen = chosen & _causal_tile(q0 + qsl.start, j * tk + c0, qsl.stop - qsl.start, nk)
            scs = [_dot_t(qm[s][qsl, :], kt) for s in range(n_slab)]
            return tuple(_online_step(states[s], jnp.where(chosen, scs[s], NEG_INF), vt)
                         for s in range(n_slab))

        half = tq // 2
        states = lax.fori_loop(0, n_full, lambda j, st: tile(st, slice(0, tq), j, 0, tk, False),
                               _softmax_state(n_slab, tq))
        states = tile(states, slice(0, tq), n_full, 0, half, True)
        lower = tile(tuple((m[half:], acc[half:]) for m, acc in states), slice(half, tq), n_full, half, half, True)
        states = tuple((jnp.concatenate([m[:half], ml], axis=0), jnp.concatenate([acc[:half], al], axis=0))
                       for (m, acc), (ml, al) in zip(states, lower))
        outs.append([_normalised(states[s][1], kv) for s in range(n_slab)])
    for s in range(n_slab):
        o_ref[0, :, s * LANES:(s + 1) * LANES] = jnp.where(lane < HEAD, outs[0][s], outs[1][s]).astype(o_ref.dtype)


def _slc_call(q, k, v, sel, e):
    bsz, s, wq = q.shape
    tq, tk = FLASH_TQ, e.shape[2]
    kern = functools.partial(_slc_kernel, tq=tq, tk=tk)
    return pl.pallas_call(
        kern,
        grid=(bsz, s // tq),
        in_specs=[pl.BlockSpec((1, tq, wq), lambda b, i: (b, i, 0)),
                  pl.BlockSpec((1, s, LANES), lambda b, i: (b, 0, 0)),
                  pl.BlockSpec((1, s, 2 * LANES), lambda b, i: (b, 0, 0)),
                  pl.BlockSpec((1, tq, LANES), lambda b, i: (b, i, 0)),
                  pl.BlockSpec(e.shape, lambda b, i: (0, 0, 0))],
        out_specs=pl.BlockSpec((1, tq, wq), lambda b, i: (b, i, 0)),
        out_shape=jax.ShapeDtypeStruct((bsz, s, wq), F32),
        compiler_params=_cparams(2),
        name="nsa_selected",
    )(q, k, v, sel, e)


def _merge_kernel(h_ref, mod_ref, oa_ref, ob_ref, ocmp_ref, oslc_ref, owin_ref, gf_ref, wg_ref, bg_ref,
                  wo_ref, o_ref, cat_ref):
    wc = ocmp_ref.shape[2]
    ts = h_ref.shape[1]
    halves = [slice(r, r + ts // 2) for r in (0, ts // 2)]
    pre = [_dot(_silu(gf_ref[0, r, :]).astype(BF16), wg_ref[...]) for r in halves]
    for r, g in zip(halves, pre):
        gates = jax.nn.sigmoid(g + bg_ref[...])
        oc = (gates[:, 0:wc] * ocmp_ref[0, r, :] + gates[:, wc:2 * wc] * oslc_ref[0, r, :]
              + gates[:, 2 * wc:3 * wc] * owin_ref[0, r, :])
        cat_ref[r, 0:256] = oa_ref[0, r, :]
        cat_ref[r, 256:512] = ob_ref[0, r, :]
        cat_ref[r, 512:512 + wc] = oc.astype(BF16)
        o_ref[0, r, :] = h_ref[0, r, :] + mod_ref[0, 5:6, :] * _dot(cat_ref[r, :], wo_ref[...])


def _merge_call(h, mod, oa, ob, ocmp, oslc, owin, gf, wg, bg, wo):
    bsz, s, d = h.shape
    ts = min(512, s)

    def tok(wd):
        return pl.BlockSpec((1, ts, wd), lambda b, i: (b, i, 0))

    def const(shape):
        return pl.BlockSpec(shape, lambda b, i: (0,) * len(shape))

    return pl.pallas_call(
        _merge_kernel,
        grid=(bsz, s // ts),
        in_specs=[tok(d), pl.BlockSpec((1, N_MOD, d), lambda b, i: (b, 0, 0)),
                  tok(256), tok(256), tok(512), tok(512), tok(512), tok(LANES),
                  const(wg.shape), const(bg.shape), const(wo.shape)],
        out_specs=tok(d),
        out_shape=jax.ShapeDtypeStruct(h.shape, F32),
        scratch_shapes=[pltpu.VMEM((ts, wo.shape[0]), BF16)],
        compiler_params=_cparams(2),
        name="mixer_merge",
    )(h, mod, oa, ob, ocmp, oslc, owin, gf, wg, bg, wo)


def _cols(w, pieces):
    out = [jnp.zeros((w.shape[0], p), w.dtype) if isinstance(p, int) else w[:, p[0]:p[1]] for p in pieces]
    return jnp.concatenate(out, axis=1)


def _head_cols(base, order):
    return [(base + hd * HEAD, base + (hd + 1) * HEAD) for hd in order]


A_ORDER = (0, 2, 1, 3)
C_ORDER = (0, 4, 1, 5, 2, 6, 3, 7)


def _proj_weight(w):
    pieces = (_head_cols(0, A_ORDER) + [(256, 384), (384, 512), (512, 768), (768, 896)]
              + [64, (896, 928), 32] + _head_cols(928, C_ORDER)
              + [(1440 + 128 * i, 1568 + 128 * i) for i in range(6)] + [(2208, 2272), 64])
    return _cols(w, pieces).astype(BF16)


def _mla_weights(w_uq, w_ukv):
    dq = B_NOPE + B_ROPE
    uq = []
    for hd in range(B_HEADS):
        uq += [(hd * dq, hd * dq + dq), 32]
    dk = B_NOPE + B_V
    uk = []
    for hd in range(B_HEADS):
        uk += [(hd * dk, hd * dk + B_NOPE), 64]
    uv = [(hd * dk + B_NOPE, (hd + 1) * dk) for hd in range(B_HEADS)]
    return _cols(w_uq, uq).astype(BF16), _cols(w_ukv, uk + uv).astype(BF16)


def _tile_to(v, reps, width=512):
    t = jnp.tile(v, reps)
    return jnp.pad(t, (0, width - t.shape[0]))


def _gain_rows(a_qk_g, b_q_lat_g, b_kv_lat_g, b_qk_nope_g, b_qk_rope_g, c_qk_g):
    z32, z64 = jnp.zeros((32,), F32), jnp.zeros((64,), F32)
    rows = [_tile_to(a_qk_g[0], 4), _tile_to(a_qk_g[1], 2), _tile_to(c_qk_g[0], 8),
            _tile_to(c_qk_g[2], 2), _tile_to(c_qk_g[3], 2),
            jnp.tile(jnp.concatenate([b_qk_nope_g[0], b_qk_rope_g[0], z32]), 4),
            jnp.tile(jnp.concatenate([b_qk_nope_g[1], z64]), 4),
            _tile_to(jnp.concatenate([z64, b_qk_rope_g[1], z32]), 1),
            _tile_to(b_q_lat_g, 1), _tile_to(b_kv_lat_g, 1)]
    rows += [jnp.zeros((512,), F32)] * (16 - len(rows))
    return jnp.stack(rows)


def _block_diag_consts():
    i = np.arange(256)
    bd64 = np.where((i[:, None] // 64) == (i[None, :] // 64), 1.0 / 64, 0.0)
    j = i % 128
    grp = np.where(j < 64, 0, np.where(j < 96, 1, 2)) + 3 * (i // 128)
    size = np.where(j < 64, 64.0, 32.0)
    bdq = np.where(grp[:, None] == grp[None, :], 1.0 / size[None, :], 0.0)
    return jnp.asarray(bd64, BF16), jnp.asarray(bdq, BF16)


def _compress_weights(pe, w1, w2):
    n_tok = C_CMP_STRIDE
    pe_rows = jnp.broadcast_to(pe.reshape(2, 2, n_tok, 1, HEAD), (2, 2, n_tok, C_KV, HEAD))
    pe_rows = pe_rows.reshape(2, 2, n_tok * C_KV * HEAD)
    w1r = w1.reshape(2, 2, n_tok, HEAD, C_CMP_HID)
    eye = jnp.eye(C_KV, dtype=F32)
    w1big = jnp.einsum('xytdj,kq->xytkdqj', w1r, eye).reshape(2, 2, n_tok * C_KV * HEAD, C_KV * C_CMP_HID)
    w2big = jnp.einsum('xjd,kq->xkjqd', w2, eye).reshape(2, C_KV * C_CMP_HID, C_KV * HEAD)
    return pe_rows, w1big.astype(BF16), w2big.astype(BF16)


def _overlap_const(n_chunk):
    n_slc = n_chunk * C_CMP_STRIDE // C_SLC_BLOCK
    ci = np.arange(n_chunk)[:, None] * C_CMP_STRIDE
    sj = np.arange(n_slc)[None, :] * C_SLC_BLOCK
    ov = ((ci <= sj + C_SLC_BLOCK - 1) & (ci + C_CMP_BLOCK - 1 >= sj)).astype(np.float32)
    ov[n_chunk - 1] = 0.0
    out = np.zeros((C_KV, n_chunk, LANES), np.float32)
    for kv in range(C_KV):
        out[kv, :, kv * HEAD:kv * HEAD + n_slc] = ov
    return jnp.asarray(out, BF16)


def _expand_const(s, tk):
    r = np.arange(LANES)[None, :, None] % HEAD
    c = (np.arange(s // tk)[:, None, None] * tk + np.arange(tk)[None, None, :]) // C_SLC_BLOCK
    return jnp.asarray((r == c).astype(np.float32), BF16)


def _gate_weights(c_gate_w, c_gate_b):
    head_of_lane = np.repeat(np.asarray(C_ORDER), HEAD)
    cols = np.concatenate([head_of_lane * 3 + k for k in range(3)])
    wg = jnp.concatenate([c_gate_w[:, cols], jnp.zeros((LANES - C_GATE_HID, cols.size), F32)], axis=0)
    return wg.astype(BF16), c_gate_b[cols].reshape(1, -1)


def _out_weight(w_out):
    rows = ([(hd * HEAD, (hd + 1) * HEAD) for hd in A_ORDER] + [(256, 512)]
            + [(512 + hd * HEAD, 512 + (hd + 1) * HEAD) for hd in C_ORDER])
    return jnp.concatenate([w_out[a:b] for a, b in rows], axis=0).astype(BF16)


def kernel(x, c, positions, ada_w, ada_b, norm_g, ffn_w_in, ffn_w_out, w_in, w_out, a_sinks, a_qk_g,
           b_q_lat_g, b_kv_lat_g, b_w_uq, b_w_ukv, b_qk_nope_g, b_qk_rope_g, c_qk_g, c_cmp_pe, c_cmp_w1,
           c_cmp_w2, c_gate_w, c_gate_b):
    bsz, s, d = x.shape
    depth = ada_w.shape[0]
    n_chunk = s // C_CMP_STRIDE
    tq = min(256, s)
    assert FLASH_TQ == FLASH_TK and s % tq == 0 and s % FLASH_TK == 0 and s >= C_WINDOW + tq and s // C_SLC_BLOCK <= HEAD

    mod_all = _ada_call(c, ada_w, ada_b).reshape(depth, bsz, N_MOD, d)
    tab = _rope_call(positions)
    bd64, bdq = _block_diag_consts()
    ov = _overlap_const(n_chunk)
    e = _expand_const(s, FLASH_TK)

    h = x
    for l in range(depth):
        mod = mod_all[l]
        h = _ffn_call(h, mod, norm_g[l, 0:1], ffn_w_in[l, 0].astype(BF16), ffn_w_out[l, 0].astype(BF16), 0)

        wuq, wukv = _mla_weights(b_w_uq[l], b_w_ukv[l])
        gv = _gain_rows(a_qk_g[l], b_q_lat_g[l], b_kv_lat_g[l], b_qk_nope_g[l], b_qk_rope_g[l], c_qk_g[l])
        (qa, ka, va, qb, kb, vb, qc, qcr, kc, vc, ks, vs, kw, vw, gf) = _prep_call(
            h, mod, norm_g[l, 1:2], _proj_weight(w_in[l]), wuq, wukv, gv, bd64, bdq, tab)

        pe_rows, w1big, w2big = _compress_weights(c_cmp_pe[l], c_cmp_w1[l], c_cmp_w2[l])
        kcmp, vcmp = _compress_call(kc, vc, pe_rows, w1big, w2big, bd64,
                                    jnp.tile(c_qk_g[l, 1], 2).reshape(1, LANES))

        oa = _window_call(qa, ka, va, a_sinks[l], A_WINDOW, BF16, "swa_attention")
        ob = _mla_call(qb, kb, vb)
        ocmp, sel = _cmp_select_call(qc, kcmp, vcmp, ov)
        oslc = _slc_call(qcr, ks, vs, sel, e)
        owin = _window_call(qcr, kw, vw, None, C_WINDOW, F32, "nsa_window")

        wg, bg = _gate_weights(c_gate_w[l], c_gate_b[l])
        h = _merge_call(h, mod, oa, ob, ocmp, oslc, owin, gf, wg, bg, _out_weight(w_out[l]))

        h = _ffn_call(h, mod, norm_g[l, 2:3], ffn_w_in[l, 1].astype(BF16), ffn_w_out[l, 1].astype(BF16), 6)
    return h
```
